```python
import math
import jax, jax.numpy as jnp
from jax import lax
import numpy as np

D_MODEL = 1024
BATCH = 8
SEQ = 2048
DEPTH = 1
DEC_BATCH = 128
DEC_SEQ = 8
PAST_LEN = 16384
PAGE_SIZE = 128

N_META = 16
D_SSM = D_MODEL
SSM_GROUP_CH = 16
SSM_GROUPS = D_SSM // SSM_GROUP_CH
SSM_STATE = 64
D_POOL = D_MODEL
POOL_WINDOWS = (2, 4, 8, 16)
POOL_GROUPS = len(POOL_WINDOWS)
POOL_GROUP_CH = D_POOL // POOL_GROUPS
POOL_BUF = max(POOL_WINDOWS) - 1
D_IN = 2 * D_SSM + 2 * D_POOL + 2 * D_MODEL
EPS = 1e-6
DT_MIN = 1e-3
DT_MAX = 1e-1

kernel_name = "gated_s5_pool_hybrid_step"


def rmsnorm(x, g):
    xf = x.astype(jnp.float32)
    y = xf * lax.rsqrt(jnp.mean(xf * xf, axis=-1, keepdims=True) + EPS)
    return (y * g.astype(jnp.float32)).astype(x.dtype)


def _complex_scan_combine(e1, e2):
    a1r, a1i, b1r, b1i = e1
    a2r, a2i, b2r, b2i = e2
    ar = a2r * a1r - a2i * a1i
    ai = a2r * a1i + a2i * a1r
    br = a2r * b1r - a2i * b1i + b2r
    bi = a2r * b1i + a2i * b1r + b2i
    return (ar, ai, br, bi)


def ssm_branch(u, s0_re, s0_im, a_re, a_im, log_dt, b_re, b_im, c_re, c_im, d, w_glu, b_glu):
    bsz, seq_len, _ = u.shape
    f32 = jnp.float32
    uf = u.astype(f32).reshape(bsz, seq_len, SSM_GROUPS, SSM_GROUP_CH)
    dt = jnp.exp(log_dt.astype(f32))[:, None]
    ar = a_re.astype(f32)
    ai = a_im.astype(f32)
    mag = jnp.exp(dt * ar)
    ang = dt * ai
    abar_re = mag * jnp.cos(ang)
    abar_im = mag * jnp.sin(ang)
    den = ar * ar + ai * ai
    nr = abar_re - 1.0
    ni = abar_im
    q_re = (nr * ar + ni * ai) / den
    q_im = (ni * ar - nr * ai) / den
    br = b_re.astype(f32)
    bi = b_im.astype(f32)
    bbar_re = q_re[..., None] * br - q_im[..., None] * bi
    bbar_im = q_re[..., None] * bi + q_im[..., None] * br
    bu_re = jnp.einsum('blgc,gpc->lbgp', uf, bbar_re)
    bu_im = jnp.einsum('blgc,gpc->lbgp', uf, bbar_im)
    a_seq_re = jnp.broadcast_to(abar_re[None, None], (seq_len, 1, SSM_GROUPS, SSM_STATE))
    a_seq_im = jnp.broadcast_to(abar_im[None, None], (seq_len, 1, SSM_GROUPS, SSM_STATE))
    _, _, s_re, s_im = lax.associative_scan(
        _complex_scan_combine, (a_seq_re, a_seq_im, bu_re, bu_im), axis=0)
    t = jnp.arange(1, seq_len + 1, dtype=f32)[:, None, None]
    pmag = jnp.exp(t * dt[None] * ar[None])
    pang = t * dt[None] * ai[None]
    pw_re = (pmag * jnp.cos(pang))[:, None]
    pw_im = (pmag * jnp.sin(pang))[:, None]
    s0r = s0_re.astype(f32)[None]
    s0i = s0_im.astype(f32)[None]
    s_re = s_re + pw_re * s0r - pw_im * s0i
    s_im = s_im + pw_re * s0i + pw_im * s0r
    y = (jnp.einsum('lbgp,gcp->blgc', s_re, c_re.astype(f32))
         - jnp.einsum('lbgp,gcp->blgc', s_im, c_im.astype(f32)))
    y = y.reshape(bsz, seq_len, D_SSM) + d.astype(f32) * uf.reshape(bsz, seq_len, D_SSM)
    y = jax.nn.gelu(y)
    y = y * jax.nn.sigmoid(y @ w_glu.astype(f32) + b_glu.astype(f32))
    return y.astype(u.dtype), s_re[-1].astype(s0_re.dtype), s_im[-1].astype(s0_im.dtype)


def pool_branch(u, buf, pos0, pool_mix, pool_scale):
    bsz, seq_len, _ = u.shape
    f32 = jnp.float32
    ext = jnp.concatenate([buf.astype(u.dtype), u], axis=1)
    uf = ext.astype(f32)
    cs = jnp.concatenate([jnp.zeros((bsz, 1, D_POOL), f32), jnp.cumsum(uf, axis=1)], axis=1)
    pos = pos0 + jnp.arange(seq_len)
    means = []
    for gi, w in enumerate(POOL_WINDOWS):
        lo, hi = gi * POOL_GROUP_CH, (gi + 1) * POOL_GROUP_CH
        win = (cs[:, POOL_BUF + 1:POOL_BUF + 1 + seq_len, lo:hi]
               - cs[:, POOL_BUF + 1 - w:POOL_BUF + 1 - w + seq_len, lo:hi])
        cnt = jnp.minimum(pos + 1, w).astype(f32)[None, :, None]
        means.append(win / cnt)
    pooled = jnp.concatenate(means, axis=-1) - uf[:, POOL_BUF:]
    mixed = jnp.einsum('blgc,gcd->blgd',
                       pooled.reshape(bsz, seq_len, POOL_GROUPS, POOL_GROUP_CH),
                       pool_mix.astype(f32)).reshape(bsz, seq_len, D_POOL)
    mixed = mixed * pool_scale.astype(f32)
    return mixed.astype(u.dtype), ext[:, -POOL_BUF:]


def hybrid_layer(h, s0_re, s0_im, buf, pos0, norm_gain, w_in, b_gate, ssm_a_re, ssm_a_im,
                 ssm_log_dt, ssm_b_re, ssm_b_im, ssm_c_re, ssm_c_im, ssm_d, w_glu, b_glu,
                 pool_mix, pool_scale, w_branch_ssm, w_branch_pool, w_out):
    xn = rmsnorm(h, norm_gain)
    proj = xn @ w_in
    o1 = D_SSM
    o2 = o1 + D_SSM
    o3 = o2 + D_POOL
    o4 = o3 + D_POOL
    u_s = proj[..., :o1]
    z_s = proj[..., o1:o2]
    u_p = proj[..., o2:o3]
    z_p = proj[..., o3:o4]
    gates = jax.nn.sigmoid(proj[..., o4:] + b_gate)
    g_s = gates[..., :D_MODEL]
    g_p = gates[..., D_MODEL:]
    y_s, new_re, new_im = ssm_branch(u_s, s0_re, s0_im, ssm_a_re, ssm_a_im, ssm_log_dt,
                                     ssm_b_re, ssm_b_im, ssm_c_re, ssm_c_im, ssm_d, w_glu, b_glu)
    y_p, new_buf = pool_branch(u_p, buf, pos0, pool_mix, pool_scale)
    a = y_s * jax.nn.silu(z_s)
    b = y_p * jax.nn.silu(z_p)
    merged = g_s * (a @ w_branch_ssm) + g_p * (b @ w_branch_pool)
    return h + merged @ w_out, new_re, new_im, new_buf


def setup_inputs(seed: int = 0) -> dict:
    key = jax.random.key(seed)
    ks = jax.random.split(key, 32)
    f32 = jnp.float32
    nrm = lambda k, shape, s: (jax.random.normal(k, shape, f32) * s)
    n_idx = jnp.arange(SSM_STATE, dtype=f32)
    a_re = -0.5 + 0.01 * jax.random.normal(ks[8], (DEPTH, SSM_GROUPS, SSM_STATE), f32)
    a_im = math.pi * n_idx[None, None, :] + 0.01 * jax.random.normal(ks[9], (DEPTH, SSM_GROUPS, SSM_STATE), f32)
    log_dt = jax.random.uniform(ks[10], (DEPTH, SSM_GROUPS), f32,
                                minval=math.log(DT_MIN), maxval=math.log(DT_MAX))
    return {
        "x_prompt": nrm(ks[0], (BATCH, SEQ, D_MODEL), 1.0),
        "x_sample": nrm(ks[1], (DEC_BATCH, DEC_SEQ, D_MODEL), 1.0),
        "state_ssm_re": nrm(ks[2], (DEPTH, DEC_BATCH, SSM_GROUPS, SSM_STATE), 0.1),
        "state_ssm_im": nrm(ks[3], (DEPTH, DEC_BATCH, SSM_GROUPS, SSM_STATE), 0.1),
        "state_pool": nrm(ks[4], (DEPTH, DEC_BATCH, POOL_BUF, D_POOL), 1.0),
        "meta_tokens": nrm(ks[5], (N_META, D_MODEL), 1.0),
        "norm_gain": 1.0 + nrm(ks[6], (DEPTH, D_MODEL), 0.01),
        "w_in": nrm(ks[7], (DEPTH, D_MODEL, D_IN), D_MODEL ** -0.5),
        "b_gate": nrm(ks[11], (DEPTH, 2 * D_MODEL), 0.01),
        "ssm_a_re": a_re,
        "ssm_a_im": a_im,
        "ssm_log_dt": log_dt,
        "ssm_b_re": nrm(ks[12], (DEPTH, SSM_GROUPS, SSM_STATE, SSM_GROUP_CH), (2.0 * SSM_GROUP_CH) ** -0.5),
        "ssm_b_im": nrm(ks[13], (DEPTH, SSM_GROUPS, SSM_STATE, SSM_GROUP_CH), (2.0 * SSM_GROUP_CH) ** -0.5),
        "ssm_c_re": nrm(ks[14], (DEPTH, SSM_GROUPS, SSM_GROUP_CH, SSM_STATE), (2.0 * SSM_STATE) ** -0.5),
        "ssm_c_im": nrm(ks[15], (DEPTH, SSM_GROUPS, SSM_GROUP_CH, SSM_STATE), (2.0 * SSM_STATE) ** -0.5),
        "ssm_d": 1.0 + nrm(ks[16], (DEPTH, D_SSM), 0.1),
        "w_glu": nrm(ks[17], (DEPTH, D_SSM, D_SSM), D_SSM ** -0.5),
        "b_glu": nrm(ks[18], (DEPTH, D_SSM), 0.01),
        "pool_mix": nrm(ks[19], (DEPTH, POOL_GROUPS, POOL_GROUP_CH, POOL_GROUP_CH), POOL_GROUP_CH ** -0.5),
        "pool_scale": 1.0 + nrm(ks[20], (DEPTH, D_POOL), 0.02),
        "w_branch_ssm": nrm(ks[21], (DEPTH, D_SSM, D_MODEL), D_SSM ** -0.5),
        "w_branch_pool": nrm(ks[22], (DEPTH, D_POOL, D_MODEL), D_POOL ** -0.5),
        "w_out": nrm(ks[23], (DEPTH, D_MODEL, D_MODEL), D_MODEL ** -0.5),
        "final_norm_gain": 1.0 + nrm(ks[24], (D_MODEL,), 0.01),
    }


def reference(x_prompt, x_sample, state_ssm_re, state_ssm_im, state_pool, meta_tokens,
              norm_gain, w_in, b_gate, ssm_a_re, ssm_a_im, ssm_log_dt, ssm_b_re, ssm_b_im,
              ssm_c_re, ssm_c_im, ssm_d, w_glu, b_glu, pool_mix, pool_scale,
              w_branch_ssm, w_branch_pool, w_out, final_norm_gain):
    meta = jnp.broadcast_to(meta_tokens[None].astype(x_prompt.dtype), (BATCH, N_META, D_MODEL))
    h_p = jnp.concatenate([meta, x_prompt], axis=1)
    h_s = x_sample
    p_re, p_im, p_buf = [], [], []
    s_re, s_im, s_buf = [], [], []
    for l in range(DEPTH):
        lw = (norm_gain[l], w_in[l], b_gate[l], ssm_a_re[l], ssm_a_im[l], ssm_log_dt[l],
              ssm_b_re[l], ssm_b_im[l], ssm_c_re[l], ssm_c_im[l], ssm_d[l], w_glu[l], b_glu[l],
              pool_mix[l], pool_scale[l], w_branch_ssm[l], w_branch_pool[l], w_out[l])
        zero_state = jnp.zeros((BATCH, SSM_GROUPS, SSM_STATE), state_ssm_re.dtype)
        zero_buf = jnp.zeros((BATCH, POOL_BUF, D_POOL), x_prompt.dtype)
        h_p, nr, ni, nb = hybrid_layer(h_p, zero_state, zero_state, zero_buf, 0, *lw)
        p_re.append(nr)
        p_im.append(ni)
        p_buf.append(nb)
        h_s, nr, ni, nb = hybrid_layer(h_s, state_ssm_re[l], state_ssm_im[l], state_pool[l],
                                       PAST_LEN, *lw)
        s_re.append(nr)
        s_im.append(ni)
        s_buf.append(nb)
    y_prompt = rmsnorm(h_p, final_norm_gain)[:, N_META:]
    y_sample = rmsnorm(h_s, final_norm_gain)
    new_ssm_re_prompt = jnp.stack(p_re, axis=0)
    new_ssm_im_prompt = jnp.stack(p_im, axis=0)
    new_pool_prompt = jnp.stack(p_buf, axis=0)
    new_ssm_re_sample = jnp.stack(s_re, axis=0)
    new_ssm_im_sample = jnp.stack(s_im, axis=0)
    new_pool_sample = jnp.stack(s_buf, axis=0)
    return (y_prompt, y_sample, new_ssm_re_prompt, new_ssm_im_prompt, new_pool_prompt,
            new_ssm_re_sample, new_ssm_im_sample, new_pool_sample)
```

```python
import functools

import jax
import jax.numpy as jnp
from jax import lax
from jax.experimental import pallas as pl
from jax.experimental.pallas import tpu as pltpu

D_MODEL = 1024
N_META = 16
SSM_GROUP_CH = 16
SSM_GROUPS = D_MODEL // SSM_GROUP_CH
SSM_STATE = 64
POOL_WINDOWS = (2, 4, 8, 16)
POOL_GROUP_CH = D_MODEL // len(POOL_WINDOWS)
POOL_BUF = max(POOL_WINDOWS) - 1
EPS = 1e-6
PAST_LEN = 16384

SUBLANES = 8
LANES = 128
GROUPS_PER_BLOCK = LANES // SSM_GROUP_CH
N_BLOCKS = SSM_GROUPS // GROUPS_PER_BLOCK
BLOCK_STATE = GROUPS_PER_BLOCK * SSM_STATE
STATE_W = SSM_GROUPS * SSM_STATE
VMEM_LIMIT_BYTES = 56 * 1024 * 1024

F32 = jnp.float32
BF16 = jnp.bfloat16


def _rmsnorm(x, gain):
    ms = jnp.mean(x * x, axis=-1, keepdims=True)
    return (x * lax.rsqrt(ms + EPS)) * gain


def _silu(x):
    return x * jax.nn.sigmoid(x)


def _layer_kernel(
        h_ref, s0re_ref, s0im_ref, buf0_ref,
        gain_ref, w_in_ref, b_gate_ref, abar_re_ref, abar_im_ref, wb_ref, wc_ref, d_ref,
        w_glu_ref, b_glu_ref, pmix_ref, pscale_ref, w_bs_ref, w_bp_ref, w_out_ref, fgain_ref,
        y_ref, ore_ref, oim_ref, obuf_ref,
        sre_ref, sim_ref, ext_ref, bu_ref, xn_ref, us_ref, tmp_ref, merged_ref,
        *, n_t, n_b, carry, pos0):
    m = n_t * n_b
    step = pl.program_id(0)
    hist_rows = POOL_BUF * n_b

    def load_state():
        sre_ref[...] = s0re_ref[...]
        sim_ref[...] = s0im_ref[...]
        ext_ref[0:hist_rows, :] = buf0_ref[...]

    if carry:
        pl.when(step == 0)(load_state)
    else:
        load_state()

    xn_ref[...] = _rmsnorm(h_ref[...], gain_ref[...]).astype(BF16)

    def proj(k):
        return jnp.dot(xn_ref[...], w_in_ref[:, k * D_MODEL:(k + 1) * D_MODEL],
                       preferred_element_type=F32)

    us_ref[...] = proj(0)
    for blk in range(N_BLOCKS):
        lanes = slice(blk * LANES, (blk + 1) * LANES)
        st = slice(blk * BLOCK_STATE, (blk + 1) * BLOCK_STATE)
        bu_ref[...] = jnp.dot(us_ref[:, lanes].astype(BF16), wb_ref[blk],
                              preferred_element_type=F32)
        a_re = jnp.broadcast_to(abar_re_ref[:, st], (SUBLANES, BLOCK_STATE))
        a_im = jnp.broadcast_to(abar_im_ref[:, st], (SUBLANES, BLOCK_STATE))
        for j in range(n_b // SUBLANES):
            rows = slice(j * SUBLANES, (j + 1) * SUBLANES)

            def scan_step(t, state, j=j, a_re=a_re, a_im=a_im):
                s_re, s_im = state
                r0 = pl.multiple_of(t * n_b + j * SUBLANES, SUBLANES)
                n_re = a_re * s_re - a_im * s_im + bu_ref[pl.ds(r0, SUBLANES), 0:BLOCK_STATE]
                n_im = a_re * s_im + a_im * s_re + bu_ref[pl.ds(r0, SUBLANES), BLOCK_STATE:]
                bu_ref[pl.ds(r0, SUBLANES), 0:BLOCK_STATE] = n_re
                bu_ref[pl.ds(r0, SUBLANES), BLOCK_STATE:] = n_im
                return n_re, n_im

            s_re, s_im = lax.fori_loop(0, n_t, scan_step, (sre_ref[rows, st], sim_ref[rows, st]),
                                       unroll=8)
            sre_ref[rows, st] = s_re
            sim_ref[rows, st] = s_im
        tmp_ref[:, lanes] = jnp.dot(bu_ref[...].astype(BF16), wc_ref[blk],
                                    preferred_element_type=F32)

    y = tmp_ref[...] + d_ref[...] * us_ref[...]
    y = jax.nn.gelu(y)
    y = y * jax.nn.sigmoid(
        jnp.dot(y.astype(BF16), w_glu_ref[...], preferred_element_type=F32) + b_glu_ref[...])
    a = y * _silu(proj(1))
    acc = jnp.dot(a.astype(BF16), w_bs_ref[...], preferred_element_type=F32)
    merged_ref[...] = jax.nn.sigmoid(proj(4) + b_gate_ref[:, 0:D_MODEL]) * acc

    ext_ref[hist_rows:hist_rows + m, :] = proj(2)
    if pos0 + 1 >= max(POOL_WINDOWS):
        pos = None
    else:
        t_loc = lax.broadcasted_iota(jnp.int32, (m, 1), 0) // n_b
        pos = pos0 + step * n_t + t_loc
    for gi, w in enumerate(POOL_WINDOWS):
        cols = slice(gi * POOL_GROUP_CH, (gi + 1) * POOL_GROUP_CH)
        s = ext_ref[(POOL_BUF + 1 - w) * n_b:hist_rows + m, cols]
        span = 1
        while span < w:
            keep = s.shape[0] - span * n_b
            s = s[:keep] + s[span * n_b:]
            span *= 2
        if pos is None:
            mean = s * (1.0 / w)
        else:
            mean = s * (1.0 / jnp.minimum(pos + 1, w).astype(F32))
        pooled = mean - ext_ref[hist_rows:hist_rows + m, cols]
        tmp_ref[:, cols] = jnp.dot(pooled.astype(BF16), pmix_ref[gi],
                                   preferred_element_type=F32) * pscale_ref[:, cols]
    b = tmp_ref[...] * _silu(proj(3))
    acc = jnp.dot(b.astype(BF16), w_bp_ref[...], preferred_element_type=F32)
    merged = merged_ref[...] + jax.nn.sigmoid(proj(5) + b_gate_ref[:, D_MODEL:]) * acc

    out = h_ref[...] + jnp.dot(merged.astype(BF16), w_out_ref[...], preferred_element_type=F32)
    y_ref[...] = _rmsnorm(out, fgain_ref[...])

    for k in range(POOL_BUF):
        ext_ref[k * n_b:(k + 1) * n_b, :] = ext_ref[(k + n_t) * n_b:(k + n_t + 1) * n_b, :]

    def write_state():
        ore_ref[...] = sre_ref[...]
        oim_ref[...] = sim_ref[...]
        obuf_ref[...] = ext_ref[0:hist_rows, :]

    if carry:
        pl.when(step == pl.num_programs(0) - 1)(write_state)
    else:
        write_state()


def _layer_call(h, s0_re, s0_im, buf0, weights, *, n_steps, n_t, n_b, carry, pos0, name):
    m = n_t * n_b
    hist_rows = POOL_BUF * n_b
    state_steps = 1 if carry else n_steps
    per_step = (lambda i: (0, 0)) if carry else (lambda i: (i, 0))
    whole = pl.BlockSpec(memory_space=pltpu.VMEM)
    kernel = functools.partial(_layer_kernel, n_t=n_t, n_b=n_b, carry=carry, pos0=pos0)
    return pl.pallas_call(
        kernel,
        grid=(n_steps,),
        in_specs=[
            pl.BlockSpec((m, D_MODEL), lambda i: (i, 0)),
            pl.BlockSpec((n_b, STATE_W), per_step),
            pl.BlockSpec((n_b, STATE_W), per_step),
            pl.BlockSpec((hist_rows, D_MODEL), per_step),
        ] + [whole] * len(weights),
        out_specs=[
            pl.BlockSpec((m, D_MODEL), lambda i: (i, 0)),
            pl.BlockSpec((n_b, STATE_W), per_step),
            pl.BlockSpec((n_b, STATE_W), per_step),
            pl.BlockSpec((hist_rows, D_MODEL), per_step),
        ],
        out_shape=[
            jax.ShapeDtypeStruct((n_steps * m, D_MODEL), F32),
            jax.ShapeDtypeStruct((state_steps * n_b, STATE_W), F32),
            jax.ShapeDtypeStruct((state_steps * n_b, STATE_W), F32),
            jax.ShapeDtypeStruct((state_steps * hist_rows, D_MODEL), F32),
        ],
        scratch_shapes=[
            pltpu.VMEM((n_b, STATE_W), F32),
            pltpu.VMEM((n_b, STATE_W), F32),
            pltpu.VMEM(((POOL_BUF + n_t) * n_b, D_MODEL), F32),
            pltpu.VMEM((m, 2 * BLOCK_STATE), F32),
            pltpu.VMEM((m, D_MODEL), BF16),
            pltpu.VMEM((m, D_MODEL), F32),
            pltpu.VMEM((m, D_MODEL), F32),
            pltpu.VMEM((m, D_MODEL), F32),
        ],
        compiler_params=pltpu.CompilerParams(
            dimension_semantics=("arbitrary",), vmem_limit_bytes=VMEM_LIMIT_BYTES),
        name=name,
    )(h, s0_re, s0_im, buf0, *weights)


def _discretise(a_re, a_im, log_dt, b_re, b_im):
    dt = jnp.exp(log_dt)[:, None]
    mag = jnp.exp(dt * a_re)
    ang = dt * a_im
    abar_re = mag * jnp.cos(ang)
    abar_im = mag * jnp.sin(ang)
    den = a_re * a_re + a_im * a_im
    nr = abar_re - 1.0
    ni = abar_im
    q_re = (nr * a_re + ni * a_im) / den
    q_im = (ni * a_re - nr * a_im) / den
    bbar_re = q_re[..., None] * b_re - q_im[..., None] * b_im
    bbar_im = q_re[..., None] * b_im + q_im[..., None] * b_re
    return abar_re, abar_im, bbar_re, bbar_im


def _block_diag_weights(bbar_re, bbar_im, c_re, c_im):
    eye = jnp.eye(GROUPS_PER_BLOCK, dtype=F32)

    def expand(bb):
        bb = bb.reshape(N_BLOCKS, GROUPS_PER_BLOCK, SSM_STATE, SSM_GROUP_CH)
        return jnp.einsum('bgpc,gh->bgchp', bb, eye).reshape(N_BLOCKS, LANES, BLOCK_STATE)

    def contract(cc):
        cc = cc.reshape(N_BLOCKS, GROUPS_PER_BLOCK, SSM_GROUP_CH, SSM_STATE)
        return jnp.einsum('bgcp,gh->bgphc', cc, eye).reshape(N_BLOCKS, BLOCK_STATE, LANES)

    wb = jnp.concatenate([expand(bbar_re), expand(bbar_im)], axis=-1).astype(BF16)
    wc = jnp.concatenate([contract(c_re), contract(-c_im)], axis=1).astype(BF16)
    return wb, wc


PROMPT_T = 48
SAMPLE_B = 32


def kernel(x_prompt, x_sample, state_ssm_re, state_ssm_im, state_pool, meta_tokens, norm_gain, w_in, b_gate, ssm_a_re, ssm_a_im, ssm_log_dt, ssm_b_re, ssm_b_im, ssm_c_re, ssm_c_im, ssm_d, w_glu, b_glu, pool_mix, pool_scale, w_branch_ssm, w_branch_pool, w_out, final_norm_gain):
    batch, seq, _ = x_prompt.shape
    dec_batch, dec_seq, _ = x_sample.shape
    depth = norm_gain.shape[0]
    assert depth == 1
    l = 0
    abar_re, abar_im, bbar_re, bbar_im = _discretise(
        ssm_a_re[l], ssm_a_im[l], ssm_log_dt[l], ssm_b_re[l], ssm_b_im[l])
    wb, wc = _block_diag_weights(bbar_re, bbar_im, ssm_c_re[l], ssm_c_im[l])
    row = lambda v: v.reshape(1, -1).astype(F32)
    weights = (
        row(norm_gain[l]), w_in[l].astype(BF16), row(b_gate[l]),
        row(abar_re), row(abar_im), wb, wc, row(ssm_d[l]),
        w_glu[l].astype(BF16), row(b_glu[l]), pool_mix[l].astype(BF16), row(pool_scale[l]),
        w_branch_ssm[l].astype(BF16), w_branch_pool[l].astype(BF16), w_out[l].astype(BF16),
        row(final_norm_gain),
    )

    total = N_META + seq
    assert total % PROMPT_T == 0 and batch == SUBLANES
    meta = jnp.broadcast_to(meta_tokens[:, None, :].astype(x_prompt.dtype), (N_META, batch, D_MODEL))
    h_p = jnp.concatenate([meta, jnp.transpose(x_prompt, (1, 0, 2))], axis=0)
    y_p, p_re, p_im, p_buf = _layer_call(
        h_p.reshape(total * batch, D_MODEL),
        jnp.zeros((batch, STATE_W), F32), jnp.zeros((batch, STATE_W), F32),
        jnp.zeros((POOL_BUF * batch, D_MODEL), F32), weights,
        n_steps=total // PROMPT_T, n_t=PROMPT_T, n_b=batch, carry=True, pos0=0, name="prompt_layer")
    y_prompt = jnp.transpose(y_p.reshape(total, batch, D_MODEL)[N_META:], (1, 0, 2))
    new_pool_prompt = jnp.transpose(p_buf.reshape(POOL_BUF, batch, D_MODEL), (1, 0, 2))[None]

    assert dec_batch % SAMPLE_B == 0
    tiles = dec_batch // SAMPLE_B
    h_s = jnp.transpose(x_sample.reshape(tiles, SAMPLE_B, dec_seq, D_MODEL), (0, 2, 1, 3))
    buf_s = jnp.transpose(state_pool[l].reshape(tiles, SAMPLE_B, POOL_BUF, D_MODEL), (0, 2, 1, 3))
    y_s, s_re, s_im, s_buf = _layer_call(
        h_s.reshape(dec_batch * dec_seq, D_MODEL),
        state_ssm_re[l].reshape(dec_batch, STATE_W), state_ssm_im[l].reshape(dec_batch, STATE_W),
        buf_s.reshape(dec_batch * POOL_BUF, D_MODEL), weights,
        n_steps=tiles, n_t=dec_seq, n_b=SAMPLE_B, carry=False, pos0=PAST_LEN, name="sample_layer")
    y_sample = jnp.transpose(y_s.reshape(tiles, dec_seq, SAMPLE_B, D_MODEL), (0, 2, 1, 3))
    y_sample = y_sample.reshape(dec_batch, dec_seq, D_MODEL)
    new_pool_sample = jnp.transpose(s_buf.reshape(tiles, POOL_BUF, SAMPLE_B, D_MODEL), (0, 2, 1, 3))
    new_pool_sample = new_pool_sample.reshape(1, dec_batch, POOL_BUF, D_MODEL)

    state_shape = (1, -1, SSM_GROUPS, SSM_STATE)
    return (y_prompt, y_sample,
            p_re.reshape(state_shape), p_im.reshape(state_shape), new_pool_prompt,
            s_re.reshape(state_shape), s_im.reshape(state_shape), new_pool_sample)
```

```python
import functools

import jax
import jax.numpy as jnp
from jax import lax
from jax.experimental import pallas as pl
from jax.experimental.pallas import tpu as pltpu

D_MODEL = 1024
N_META = 16
SSM_GROUP_CH = 16
SSM_GROUPS = D_MODEL // SSM_GROUP_CH
SSM_STATE = 64
POOL_WINDOWS = (2, 4, 8, 16)
POOL_GROUP_CH = D_MODEL // len(POOL_WINDOWS)
POOL_BUF = max(POOL_WINDOWS) - 1
EPS = 1e-6
PAST_LEN = 16384

SUBLANES = 8
LANES = 128
GROUPS_PER_BLOCK = LANES // SSM_GROUP_CH
N_BLOCKS = SSM_GROUPS // GROUPS_PER_BLOCK
BLOCK_STATE = GROUPS_PER_BLOCK * SSM_STATE
STATE_W = SSM_GROUPS * SSM_STATE
VMEM_LIMIT_BYTES = 56 * 1024 * 1024

F32 = jnp.float32
BF16 = jnp.bfloat16


def _rmsnorm(x, gain):
    ms = jnp.mean(x * x, axis=-1, keepdims=True)
    return (x * lax.rsqrt(ms + EPS)) * gain


def _silu(x):
    return x * jax.nn.sigmoid(x)


def _layer_kernel(
        x_ref, s0re_ref, s0im_ref, buf0_ref,
        gain_ref, w_in_ref, b_gate_ref, abar_re_ref, abar_im_ref, wb_ref, wc_ref, d_ref,
        w_glu_ref, b_glu_ref, pmix_ref, pscale_ref, w_bs_ref, w_bp_ref, w_out_ref, fgain_ref,
        y_ref, ore_ref, oim_ref, obuf_ref,
        sre_ref, sim_ref, ext_ref, bu_ref, h_ref, xn_ref, us_ref, tmp_ref, merged_ref,
        *, n_t, n_b, carry, pos0):
    m = n_t * n_b
    step = pl.program_id(0)
    hist_rows = POOL_BUF * n_b

    def load_state():
        sre_ref[...] = s0re_ref[...]
        sim_ref[...] = s0im_ref[...]
        ext_ref[0:hist_rows, :] = buf0_ref[...]

    if carry:
        pl.when(step == 0)(load_state)
    else:
        load_state()

    h_ref[...] = jnp.swapaxes(x_ref[...], 0, 1).reshape(m, D_MODEL)
    xn_ref[...] = _rmsnorm(h_ref[...], gain_ref[...]).astype(BF16)

    def proj(k):
        return jnp.dot(xn_ref[...], w_in_ref[:, k * D_MODEL:(k + 1) * D_MODEL],
                       preferred_element_type=F32)

    us_ref[...] = proj(0)
    for blk in range(N_BLOCKS):
        slot = blk % 2
        lanes = slice(blk * LANES, (blk + 1) * LANES)
        st = slice(blk * BLOCK_STATE, (blk + 1) * BLOCK_STATE)
        bu_ref[slot] = jnp.dot(us_ref[:, lanes].astype(BF16), wb_ref[blk],
                               preferred_element_type=F32)
        a_re = jnp.broadcast_to(abar_re_ref[:, st], (SUBLANES, BLOCK_STATE))
        a_im = jnp.broadcast_to(abar_im_ref[:, st], (SUBLANES, BLOCK_STATE))
        for j in range(n_b // SUBLANES):
            tile = slice(j * SUBLANES, (j + 1) * SUBLANES)
            s_re = sre_ref[tile, st]
            s_im = sim_ref[tile, st]
            for t in range(n_t):
                rows = slice(t * n_b + j * SUBLANES, t * n_b + (j + 1) * SUBLANES)
                n_re = a_re * s_re - a_im * s_im + bu_ref[slot, rows, 0:BLOCK_STATE]
                n_im = a_re * s_im + a_im * s_re + bu_ref[slot, rows, BLOCK_STATE:]
                bu_ref[slot, rows, 0:BLOCK_STATE] = n_re
                bu_ref[slot, rows, BLOCK_STATE:] = n_im
                s_re, s_im = n_re, n_im
            sre_ref[tile, st] = s_re
            sim_ref[tile, st] = s_im
        tmp_ref[:, lanes] = jnp.dot(bu_ref[slot].astype(BF16), wc_ref[blk],
                                    preferred_element_type=F32)

    y = tmp_ref[...] + d_ref[...] * us_ref[...]
    y = jax.nn.gelu(y)
    y = y * jax.nn.sigmoid(
        jnp.dot(y.astype(BF16), w_glu_ref[...], preferred_element_type=F32) + b_glu_ref[...])
    a = y * _silu(proj(1))
    acc = jnp.dot(a.astype(BF16), w_bs_ref[...], preferred_element_type=F32)
    merged_ref[...] = jax.nn.sigmoid(proj(4) + b_gate_ref[:, 0:D_MODEL]) * acc

    ext_ref[hist_rows:hist_rows + m, :] = proj(2)
    if pos0 + 1 >= max(POOL_WINDOWS):
        pos = None
    else:
        t_loc = lax.broadcasted_iota(jnp.int32, (m, 1), 0) // n_b
        pos = pos0 + step * n_t + t_loc
    for gi, w in enumerate(POOL_WINDOWS):
        cols = slice(gi * POOL_GROUP_CH, (gi + 1) * POOL_GROUP_CH)
        s = ext_ref[(POOL_BUF + 1 - w) * n_b:hist_rows + m, cols]
        span = 1
        while span < w:
            keep = s.shape[0] - span * n_b
            s = s[:keep] + s[span * n_b:]
            span *= 2
        if pos is None:
            mean = s * (1.0 / w)
        else:
            mean = s * (1.0 / jnp.minimum(pos + 1, w).astype(F32))
        pooled = mean - ext_ref[hist_rows:hist_rows + m, cols]
        tmp_ref[:, cols] = jnp.dot(pooled.astype(BF16), pmix_ref[gi],
                                   preferred_element_type=F32) * pscale_ref[:, cols]
    b = tmp_ref[...] * _silu(proj(3))
    acc = jnp.dot(b.astype(BF16), w_bp_ref[...], preferred_element_type=F32)
    merged = merged_ref[...] + jax.nn.sigmoid(proj(5) + b_gate_ref[:, D_MODEL:]) * acc

    out = h_ref[...] + jnp.dot(merged.astype(BF16), w_out_ref[...], preferred_element_type=F32)
    y_rows = _rmsnorm(out, fgain_ref[...])
    y_ref[...] = jnp.swapaxes(y_rows.reshape(n_t, n_b, D_MODEL), 0, 1)

    for k in range(POOL_BUF):
        ext_ref[k * n_b:(k + 1) * n_b, :] = ext_ref[(k + n_t) * n_b:(k + n_t + 1) * n_b, :]

    def write_state():
        ore_ref[...] = sre_ref[...]
        oim_ref[...] = sim_ref[...]
        obuf_ref[...] = ext_ref[0:hist_rows, :]

    if carry:
        pl.when(step == pl.num_programs(0) - 1)(write_state)
    else:
        write_state()


def _layer_call(x, s0_re, s0_im, buf0, weights, *, n_t, n_b, carry, pos0, name):
    batch, time, _ = x.shape
    m = n_t * n_b
    hist_rows = POOL_BUF * n_b
    if carry:
        assert batch == n_b and time % n_t == 0
        n_steps = time // n_t
        x_map = lambda i: (0, i, 0)
        per_step = lambda i: (0, 0)
    else:
        assert time == n_t and batch % n_b == 0
        n_steps = batch // n_b
        x_map = lambda i: (i, 0, 0)
        per_step = lambda i: (i, 0)
    whole = pl.BlockSpec(memory_space=pltpu.VMEM)
    kernel = functools.partial(_layer_kernel, n_t=n_t, n_b=n_b, carry=carry, pos0=pos0)
    return pl.pallas_call(
        kernel,
        grid=(n_steps,),
        in_specs=[
            pl.BlockSpec((n_b, n_t, D_MODEL), x_map),
            pl.BlockSpec((n_b, STATE_W), per_step),
            pl.BlockSpec((n_b, STATE_W), per_step),
            pl.BlockSpec((hist_rows, D_MODEL), per_step),
        ] + [whole] * len(weights),
        out_specs=[
            pl.BlockSpec((n_b, n_t, D_MODEL), x_map),
            pl.BlockSpec((n_b, STATE_W), per_step),
            pl.BlockSpec((n_b, STATE_W), per_step),
            pl.BlockSpec((hist_rows, D_MODEL), per_step),
        ],
        out_shape=[
            jax.ShapeDtypeStruct((batch, time, D_MODEL), F32),
            jax.ShapeDtypeStruct((batch, STATE_W), F32),
            jax.ShapeDtypeStruct((batch, STATE_W), F32),
            jax.ShapeDtypeStruct((batch // n_b * hist_rows, D_MODEL), F32),
        ],
        scratch_shapes=[
            pltpu.VMEM((n_b, STATE_W), F32),
            pltpu.VMEM((n_b, STATE_W), F32),
            pltpu.VMEM(((POOL_BUF + n_t) * n_b, D_MODEL), F32),
            pltpu.VMEM((2, m, 2 * BLOCK_STATE), F32),
            pltpu.VMEM((m, D_MODEL), F32),
            pltpu.VMEM((m, D_MODEL), BF16),
            pltpu.VMEM((m, D_MODEL), F32),
            pltpu.VMEM((m, D_MODEL), F32),
            pltpu.VMEM((m, D_MODEL), F32),
        ],
        compiler_params=pltpu.CompilerParams(
            dimension_semantics=("arbitrary",), vmem_limit_bytes=VMEM_LIMIT_BYTES),
        name=name,
    )(x, s0_re, s0_im, buf0, *weights)


def _discretise(a_re, a_im, log_dt, b_re, b_im):
    dt = jnp.exp(log_dt)[:, None]
    mag = jnp.exp(dt * a_re)
    ang = dt * a_im
    abar_re = mag * jnp.cos(ang)
    abar_im = mag * jnp.sin(ang)
    den = a_re * a_re + a_im * a_im
    nr = abar_re - 1.0
    ni = abar_im
    q_re = (nr * a_re + ni * a_im) / den
    q_im = (ni * a_re - nr * a_im) / den
    bbar_re = q_re[..., None] * b_re - q_im[..., None] * b_im
    bbar_im = q_re[..., None] * b_im + q_im[..., None] * b_re
    return abar_re, abar_im, bbar_re, bbar_im


def _block_diag_weights(bbar_re, bbar_im, c_re, c_im):
    gpb = GROUPS_PER_BLOCK
    eye = jnp.eye(gpb, dtype=F32)

    def expand(bb):
        bb = jnp.transpose(bb, (0, 2, 1)).reshape(N_BLOCKS, gpb, SSM_GROUP_CH, 1, SSM_STATE)
        bb = bb * eye.reshape(1, gpb, 1, gpb, 1)
        return bb.reshape(N_BLOCKS, LANES, BLOCK_STATE)

    def contract(cc):
        cc = jnp.transpose(cc, (0, 2, 1)).reshape(N_BLOCKS, gpb, SSM_STATE, 1, SSM_GROUP_CH)
        cc = cc * eye.reshape(1, gpb, 1, gpb, 1)
        return cc.reshape(N_BLOCKS, BLOCK_STATE, LANES)

    wb = jnp.concatenate([expand(bbar_re), expand(bbar_im)], axis=-1).astype(BF16)
    wc = jnp.concatenate([contract(c_re), contract(-c_im)], axis=1).astype(BF16)
    return wb, wc


def _rows_kb(buf, n_b):
    batch = buf.shape[0]
    buf = jnp.transpose(buf.reshape(batch // n_b, n_b, POOL_BUF, D_MODEL), (0, 2, 1, 3))
    return buf.reshape(batch * POOL_BUF, D_MODEL)


def _rows_bk(rows, n_b):
    batch = rows.shape[0] // POOL_BUF
    rows = jnp.transpose(rows.reshape(batch // n_b, POOL_BUF, n_b, D_MODEL), (0, 2, 1, 3))
    return rows.reshape(batch, POOL_BUF, D_MODEL)


PROMPT_T = 32
SAMPLE_B = 32


def kernel(x_prompt, x_sample, state_ssm_re, state_ssm_im, state_pool, meta_tokens, norm_gain, w_in, b_gate, ssm_a_re, ssm_a_im, ssm_log_dt, ssm_b_re, ssm_b_im, ssm_c_re, ssm_c_im, ssm_d, w_glu, b_glu, pool_mix, pool_scale, w_branch_ssm, w_branch_pool, w_out, final_norm_gain):
    batch, seq, _ = x_prompt.shape
    dec_batch, dec_seq, _ = x_sample.shape
    depth = norm_gain.shape[0]
    assert depth == 1
    l = 0
    abar_re, abar_im, bbar_re, bbar_im = _discretise(
        ssm_a_re[l], ssm_a_im[l], ssm_log_dt[l], ssm_b_re[l], ssm_b_im[l])
    wb, wc = _block_diag_weights(bbar_re, bbar_im, ssm_c_re[l], ssm_c_im[l])
    row = lambda v: v.reshape(1, -1).astype(F32)
    weights = (
        row(norm_gain[l]), w_in[l].astype(BF16), row(b_gate[l]),
        row(abar_re), row(abar_im), wb, wc, row(ssm_d[l]),
        w_glu[l].astype(BF16), row(b_glu[l]), pool_mix[l].astype(BF16), row(pool_scale[l]),
        w_branch_ssm[l].astype(BF16), w_branch_pool[l].astype(BF16), w_out[l].astype(BF16),
        row(final_norm_gain),
    )

    assert batch == SUBLANES
    meta = jnp.broadcast_to(meta_tokens[None].astype(x_prompt.dtype), (batch, N_META, D_MODEL))
    zero_state = jnp.zeros((batch, STATE_W), F32)
    _, m_re, m_im, m_buf = _layer_call(
        meta, zero_state, zero_state, jnp.zeros((POOL_BUF * batch, D_MODEL), F32), weights,
        n_t=N_META, n_b=batch, carry=False, pos0=0, name="meta_layer")
    y_prompt, p_re, p_im, p_buf = _layer_call(
        x_prompt, m_re, m_im, m_buf, weights,
        n_t=PROMPT_T, n_b=batch, carry=True, pos0=N_META, name="prompt_layer")
    new_pool_prompt = _rows_bk(p_buf, batch)[None]

    y_sample, s_re, s_im, s_buf = _layer_call(
        x_sample, state_ssm_re[l].reshape(dec_batch, STATE_W),
        state_ssm_im[l].reshape(dec_batch, STATE_W), _rows_kb(state_pool[l], SAMPLE_B), weights,
        n_t=dec_seq, n_b=SAMPLE_B, carry=False, pos0=PAST_LEN, name="sample_layer")
    new_pool_sample = _rows_bk(s_buf, SAMPLE_B)[None]

    state_shape = (1, -1, SSM_GROUPS, SSM_STATE)
    return (y_prompt, y_sample,
            p_re.reshape(state_shape), p_im.reshape(state_shape), new_pool_prompt,
            s_re.reshape(state_shape), s_im.reshape(state_shape), new_pool_sample)
```

```python
import functools

import jax
import jax.numpy as jnp
from jax import lax
from jax.experimental import pallas as pl
from jax.experimental.pallas import tpu as pltpu

D_MODEL = 1024
N_META = 16
SSM_GROUP_CH = 16
SSM_GROUPS = D_MODEL // SSM_GROUP_CH
SSM_STATE = 64
POOL_WINDOWS = (2, 4, 8, 16)
POOL_GROUP_CH = D_MODEL // len(POOL_WINDOWS)
POOL_BUF = max(POOL_WINDOWS) - 1
EPS = 1e-6
PAST_LEN = 16384

SUBLANES = 8
LANES = 128
GROUPS_PER_BLOCK = LANES // SSM_GROUP_CH
N_BLOCKS = SSM_GROUPS // GROUPS_PER_BLOCK
BLOCK_STATE = GROUPS_PER_BLOCK * SSM_STATE
STATE_W = SSM_GROUPS * SSM_STATE
VMEM_LIMIT_BYTES = 56 * 1024 * 1024

F32 = jnp.float32
BF16 = jnp.bfloat16


def _rmsnorm(x, gain):
    ms = jnp.mean(x * x, axis=-1, keepdims=True)
    return (x * lax.rsqrt(ms + EPS)) * gain


def _silu(x):
    return x * jax.nn.sigmoid(x)


def _layer_kernel(
        x_ref, s0re_ref, s0im_ref, buf0_ref,
        gain_ref, w_in_ref, b_gate_ref, abar_re_ref, abar_im_ref, wb_ref, wc_ref, d_ref,
        w_glu_ref, b_glu_ref, pmix_ref, pscale_ref, w_bs_ref, w_bp_ref, w_out_ref, fgain_ref,
        y_ref, ore_ref, oim_ref, obuf_ref,
        sre_ref, sim_ref, ext_ref, bu_ref, h_ref, xn_ref, us_ref, zs_ref, gs_ref, zp_ref, gp_ref,
        ys_ref, yp_ref,
        *, n_t, n_b, carry, pos0):
    m = n_t * n_b
    step = pl.program_id(0)
    hist_rows = POOL_BUF * n_b

    def load_state():
        sre_ref[...] = s0re_ref[...]
        sim_ref[...] = s0im_ref[...]
        ext_ref[0:hist_rows, :] = buf0_ref[...]

    if carry:
        pl.when(step == 0)(load_state)
    else:
        load_state()

    h_ref[...] = jnp.swapaxes(x_ref[...], 0, 1).reshape(m, D_MODEL)
    xn_ref[...] = _rmsnorm(h_ref[...], gain_ref[...]).astype(BF16)

    def proj(k):
        return jnp.dot(xn_ref[...], w_in_ref[:, k * D_MODEL:(k + 1) * D_MODEL],
                       preferred_element_type=F32)

    us_ref[...] = proj(0)
    for blk in range(N_BLOCKS):
        lanes = slice(blk * LANES, (blk + 1) * LANES)
        bu_ref[blk] = jnp.dot(us_ref[:, lanes].astype(BF16), wb_ref[blk],
                              preferred_element_type=F32)
    zs_ref[...] = _silu(proj(1))
    gs_ref[...] = jax.nn.sigmoid(proj(4) + b_gate_ref[:, 0:D_MODEL])
    ext_ref[hist_rows:hist_rows + m, :] = proj(2)
    zp_ref[...] = _silu(proj(3))
    gp_ref[...] = jax.nn.sigmoid(proj(5) + b_gate_ref[:, D_MODEL:])

    for blk in range(N_BLOCKS):
        lanes = slice(blk * LANES, (blk + 1) * LANES)
        st = slice(blk * BLOCK_STATE, (blk + 1) * BLOCK_STATE)
        a_re = jnp.broadcast_to(abar_re_ref[:, st], (SUBLANES, BLOCK_STATE))
        a_im = jnp.broadcast_to(abar_im_ref[:, st], (SUBLANES, BLOCK_STATE))
        for j in range(n_b // SUBLANES):
            tile = slice(j * SUBLANES, (j + 1) * SUBLANES)
            s_re = sre_ref[tile, st]
            s_im = sim_ref[tile, st]
            for t in range(n_t):
                rows = slice(t * n_b + j * SUBLANES, t * n_b + (j + 1) * SUBLANES)
                n_re = a_re * s_re - a_im * s_im + bu_ref[blk, rows, 0:BLOCK_STATE]
                n_im = a_re * s_im + a_im * s_re + bu_ref[blk, rows, BLOCK_STATE:]
                bu_ref[blk, rows, 0:BLOCK_STATE] = n_re
                bu_ref[blk, rows, BLOCK_STATE:] = n_im
                s_re, s_im = n_re, n_im
            sre_ref[tile, st] = s_re
            sim_ref[tile, st] = s_im
        ys_ref[:, lanes] = jnp.dot(bu_ref[blk].astype(BF16), wc_ref[blk],
                                   preferred_element_type=F32)

    if pos0 + 1 >= max(POOL_WINDOWS):
        pos = None
    else:
        t_loc = lax.broadcasted_iota(jnp.int32, (m, 1), 0) // n_b
        pos = pos0 + step * n_t + t_loc
    for gi, w in enumerate(POOL_WINDOWS):
        cols = slice(gi * POOL_GROUP_CH, (gi + 1) * POOL_GROUP_CH)
        s = ext_ref[(POOL_BUF + 1 - w) * n_b:hist_rows + m, cols]
        span = 1
        while span < w:
            keep = s.shape[0] - span * n_b
            s = s[:keep] + s[span * n_b:]
            span *= 2
        if pos is None:
            mean = s * (1.0 / w)
        else:
            mean = s * (1.0 / jnp.minimum(pos + 1, w).astype(F32))
        pooled = mean - ext_ref[hist_rows:hist_rows + m, cols]
        yp_ref[:, cols] = jnp.dot(pooled.astype(BF16), pmix_ref[gi],
                                  preferred_element_type=F32) * pscale_ref[:, cols]
    b = yp_ref[...] * zp_ref[...]
    acc_p = jnp.dot(b.astype(BF16), w_bp_ref[...], preferred_element_type=F32)

    y = ys_ref[...] + d_ref[...] * us_ref[...]
    y = jax.nn.gelu(y)
    y = y * jax.nn.sigmoid(
        jnp.dot(y.astype(BF16), w_glu_ref[...], preferred_element_type=F32) + b_glu_ref[...])
    a = y * zs_ref[...]
    acc_s = jnp.dot(a.astype(BF16), w_bs_ref[...], preferred_element_type=F32)
    merged = gs_ref[...] * acc_s + gp_ref[...] * acc_p

    out = h_ref[...] + jnp.dot(merged.astype(BF16), w_out_ref[...], preferred_element_type=F32)
    y_rows = _rmsnorm(out, fgain_ref[...])
    y_ref[...] = jnp.swapaxes(y_rows.reshape(n_t, n_b, D_MODEL), 0, 1)

    for k in range(POOL_BUF):
        ext_ref[k * n_b:(k + 1) * n_b, :] = ext_ref[(k + n_t) * n_b:(k + n_t + 1) * n_b, :]

    def write_state():
        ore_ref[...] = sre_ref[...]
        oim_ref[...] = sim_ref[...]
        obuf_ref[...] = ext_ref[0:hist_rows, :]

    if carry:
        pl.when(step == pl.num_programs(0) - 1)(write_state)
    else:
        write_state()


def _layer_call(x, s0_re, s0_im, buf0, weights, *, n_t, n_b, carry, pos0, name):
    batch, time, _ = x.shape
    m = n_t * n_b
    hist_rows = POOL_BUF * n_b
    if carry:
        assert batch == n_b and time % n_t == 0
        n_steps = time // n_t
        x_map = lambda i: (0, i, 0)
        per_step = lambda i: (0, 0)
    else:
        assert time == n_t and batch % n_b == 0
        n_steps = batch // n_b
        x_map = lambda i: (i, 0, 0)
        per_step = lambda i: (i, 0)
    whole = pl.BlockSpec(memory_space=pltpu.VMEM)
    kernel = functools.partial(_layer_kernel, n_t=n_t, n_b=n_b, carry=carry, pos0=pos0)
    return pl.pallas_call(
        kernel,
        grid=(n_steps,),
        in_specs=[
            pl.BlockSpec((n_b, n_t, D_MODEL), x_map),
            pl.BlockSpec((n_b, STATE_W), per_step),
            pl.BlockSpec((n_b, STATE_W), per_step),
            pl.BlockSpec((hist_rows, D_MODEL), per_step),
        ] + [whole] * len(weights),
        out_specs=[
            pl.BlockSpec((n_b, n_t, D_MODEL), x_map),
            pl.BlockSpec((n_b, STATE_W), per_step),
            pl.BlockSpec((n_b, STATE_W), per_step),
            pl.BlockSpec((hist_rows, D_MODEL), per_step),
        ],
        out_shape=[
            jax.ShapeDtypeStruct((batch, time, D_MODEL), F32),
            jax.ShapeDtypeStruct((batch, STATE_W), F32),
            jax.ShapeDtypeStruct((batch, STATE_W), F32),
            jax.ShapeDtypeStruct((batch // n_b * hist_rows, D_MODEL), F32),
        ],
        scratch_shapes=[
            pltpu.VMEM((n_b, STATE_W), F32),
            pltpu.VMEM((n_b, STATE_W), F32),
            pltpu.VMEM(((POOL_BUF + n_t) * n_b, D_MODEL), F32),
            pltpu.VMEM((N_BLOCKS, m, 2 * BLOCK_STATE), F32),
            pltpu.VMEM((m, D_MODEL), F32),
            pltpu.VMEM((m, D_MODEL), BF16),
            pltpu.VMEM((m, D_MODEL), F32),
            pltpu.VMEM((m, D_MODEL), F32),
            pltpu.VMEM((m, D_MODEL), F32),
            pltpu.VMEM((m, D_MODEL), F32),
            pltpu.VMEM((m, D_MODEL), F32),
            pltpu.VMEM((m, D_MODEL), F32),
            pltpu.VMEM((m, D_MODEL), F32),
        ],
        compiler_params=pltpu.CompilerParams(
            dimension_semantics=("arbitrary",), vmem_limit_bytes=VMEM_LIMIT_BYTES),
        name=name,
    )(x, s0_re, s0_im, buf0, *weights)


def _discretise(a_re, a_im, log_dt, b_re, b_im):
    dt = jnp.exp(log_dt)[:, None]
    mag = jnp.exp(dt * a_re)
    ang = dt * a_im
    abar_re = mag * jnp.cos(ang)
    abar_im = mag * jnp.sin(ang)
    den = a_re * a_re + a_im * a_im
    nr = abar_re - 1.0
    ni = abar_im
    q_re = (nr * a_re + ni * a_im) / den
    q_im = (ni * a_re - nr * a_im) / den
    bbar_re = q_re[..., None] * b_re - q_im[..., None] * b_im
    bbar_im = q_re[..., None] * b_im + q_im[..., None] * b_re
    return abar_re, abar_im, bbar_re, bbar_im


def _block_diag_weights(bbar_re, bbar_im, c_re, c_im):
    gpb = GROUPS_PER_BLOCK
    eye = jnp.eye(gpb, dtype=F32)

    def expand(bb):
        bb = jnp.transpose(bb, (0, 2, 1)).reshape(N_BLOCKS, gpb, SSM_GROUP_CH, 1, SSM_STATE)
        bb = bb * eye.reshape(1, gpb, 1, gpb, 1)
        return bb.reshape(N_BLOCKS, LANES, BLOCK_STATE)

    def contract(cc):
        cc = jnp.transpose(cc, (0, 2, 1)).reshape(N_BLOCKS, gpb, SSM_STATE, 1, SSM_GROUP_CH)
        cc = cc * eye.reshape(1, gpb, 1, gpb, 1)
        return cc.reshape(N_BLOCKS, BLOCK_STATE, LANES)

    wb = jnp.concatenate([expand(bbar_re), expand(bbar_im)], axis=-1).astype(BF16)
    wc = jnp.concatenate([contract(c_re), contract(-c_im)], axis=1).astype(BF16)
    return wb, wc


def _rows_kb(buf, n_b):
    batch = buf.shape[0]
    buf = jnp.transpose(buf.reshape(batch // n_b, n_b, POOL_BUF, D_MODEL), (0, 2, 1, 3))
    return buf.reshape(batch * POOL_BUF, D_MODEL)


def _rows_bk(rows, n_b):
    batch = rows.shape[0] // POOL_BUF
    rows = jnp.transpose(rows.reshape(batch // n_b, POOL_BUF, n_b, D_MODEL), (0, 2, 1, 3))
    return rows.reshape(batch, POOL_BUF, D_MODEL)


PROMPT_T = 32
SAMPLE_B = 32


def kernel(x_prompt, x_sample, state_ssm_re, state_ssm_im, state_pool, meta_tokens, norm_gain, w_in, b_gate, ssm_a_re, ssm_a_im, ssm_log_dt, ssm_b_re, ssm_b_im, ssm_c_re, ssm_c_im, ssm_d, w_glu, b_glu, pool_mix, pool_scale, w_branch_ssm, w_branch_pool, w_out, final_norm_gain):
    batch, seq, _ = x_prompt.shape
    dec_batch, dec_seq, _ = x_sample.shape
    depth = norm_gain.shape[0]
    assert depth == 1
    l = 0
    abar_re, abar_im, bbar_re, bbar_im = _discretise(
        ssm_a_re[l], ssm_a_im[l], ssm_log_dt[l], ssm_b_re[l], ssm_b_im[l])
    wb, wc = _block_diag_weights(bbar_re, bbar_im, ssm_c_re[l], ssm_c_im[l])
    row = lambda v: v.reshape(1, -1).astype(F32)
    weights = (
        row(norm_gain[l]), w_in[l].astype(BF16), row(b_gate[l]),
        row(abar_re), row(abar_im), wb, wc, row(ssm_d[l]),
        w_glu[l].astype(BF16), row(b_glu[l]), pool_mix[l].astype(BF16), row(pool_scale[l]),
        w_branch_ssm[l].astype(BF16), w_branch_pool[l].astype(BF16), w_out[l].astype(BF16),
        row(final_norm_gain),
    )

    assert batch == SUBLANES
    meta = jnp.broadcast_to(meta_tokens[None].astype(x_prompt.dtype), (batch, N_META, D_MODEL))
    zero_state = jnp.zeros((batch, STATE_W), F32)
    _, m_re, m_im, m_buf = _layer_call(
        meta, zero_state, zero_state, jnp.zeros((POOL_BUF * batch, D_MODEL), F32), weights,
        n_t=N_META, n_b=batch, carry=False, pos0=0, name="meta_layer")
    y_prompt, p_re, p_im, p_buf = _layer_call(
        x_prompt, m_re, m_im, m_buf, weights,
        n_t=PROMPT_T, n_b=batch, carry=True, pos0=N_META, name="prompt_layer")
    new_pool_prompt = _rows_bk(p_buf, batch)[None]

    y_sample, s_re, s_im, s_buf = _layer_call(
        x_sample, state_ssm_re[l].reshape(dec_batch, STATE_W),
        state_ssm_im[l].reshape(dec_batch, STATE_W), _rows_kb(state_pool[l], SAMPLE_B), weights,
        n_t=dec_seq, n_b=SAMPLE_B, carry=False, pos0=PAST_LEN, name="sample_layer")
    new_pool_sample = _rows_bk(s_buf, SAMPLE_B)[None]

    state_shape = (1, -1, SSM_GROUPS, SSM_STATE)
    return (y_prompt, y_sample,
            p_re.reshape(state_shape), p_im.reshape(state_shape), new_pool_prompt,
            s_re.reshape(state_shape), s_im.reshape(state_shape), new_pool_sample)
```

```python
import functools

import jax
import jax.numpy as jnp
from jax import lax
from jax.experimental import pallas as pl
from jax.experimental.pallas import tpu as pltpu

D_MODEL = 1024
N_META = 16
SSM_GROUP_CH = 16
SSM_GROUPS = D_MODEL // SSM_GROUP_CH
SSM_STATE = 64
POOL_WINDOWS = (2, 4, 8, 16)
POOL_GROUP_CH = D_MODEL // len(POOL_WINDOWS)
POOL_BUF = max(POOL_WINDOWS) - 1
EPS = 1e-6
PAST_LEN = 16384

SUBLANES = 8
LANES = 128
GROUPS_PER_BLOCK = LANES // SSM_GROUP_CH
N_BLOCKS = SSM_GROUPS // GROUPS_PER_BLOCK
BLOCK_STATE = GROUPS_PER_BLOCK * SSM_STATE
STATE_W = SSM_GROUPS * SSM_STATE
VMEM_LIMIT_BYTES = 56 * 1024 * 1024

F32 = jnp.float32
BF16 = jnp.bfloat16


def _rmsnorm(x, gain):
    ms = jnp.mean(x * x, axis=-1, keepdims=True)
    return (x * lax.rsqrt(ms + EPS)) * gain


def _silu(x):
    return x * jax.nn.sigmoid(x)


def _layer_kernel(
        x_ref, s0re_ref, s0im_ref, buf0_ref,
        gain_ref, w_in_ref, b_gate_ref, a2_re_ref, a2_im_ref, wb_ref, wc_ref, wd_ref, d_ref,
        w_glu_ref, b_glu_ref, pmix_ref, pscale_ref, w_bs_ref, w_bp_ref, w_out_ref, fgain_ref,
        y_ref, ore_ref, oim_ref, obuf_ref,
        sre_ref, sim_ref, ext_ref, bu_ref, u2_ref, h_ref, xn_ref, us_ref, zs_ref, gs_ref, zp_ref,
        gp_ref, ys_ref, yp_ref,
        *, n_t, n_b, carry, pos0):
    m = n_t * n_b
    n_pairs = n_t // 2
    half = n_pairs * n_b
    step = pl.program_id(0)
    hist_rows = POOL_BUF * n_b

    def load_state():
        sre_ref[...] = s0re_ref[...]
        sim_ref[...] = s0im_ref[...]
        ext_ref[0:hist_rows, :] = buf0_ref[...]

    if carry:
        pl.when(step == 0)(load_state)
    else:
        load_state()

    h_ref[...] = jnp.swapaxes(x_ref[...], 0, 1).reshape(m, D_MODEL)
    xn_ref[...] = _rmsnorm(h_ref[...], gain_ref[...]).astype(BF16)

    def proj(k):
        return jnp.dot(xn_ref[...], w_in_ref[:, k * D_MODEL:(k + 1) * D_MODEL],
                       preferred_element_type=F32)

    us_ref[...] = proj(0).reshape(n_pairs, 2 * n_b, D_MODEL)
    for blk in range(N_BLOCKS):
        lanes = slice(blk * LANES, (blk + 1) * LANES)
        st = slice(blk * BLOCK_STATE, (blk + 1) * BLOCK_STATE)
        u_t0 = us_ref[:, 0:n_b, lanes].reshape(half, LANES)
        u_t1 = us_ref[:, n_b:, lanes].reshape(half, LANES)
        u2_ref[blk] = jnp.concatenate([u_t0, u_t1], axis=1).astype(BF16)
        bu_ref[blk, 0:n_b, 0:BLOCK_STATE] = sre_ref[:, st]
        bu_ref[blk, 0:n_b, BLOCK_STATE:] = sim_ref[:, st]
        bu_ref[blk, n_b:, :] = jnp.dot(u2_ref[blk], wb_ref[blk], preferred_element_type=F32)
    zs_ref[...] = _silu(proj(1))
    gs_ref[...] = jax.nn.sigmoid(proj(4) + b_gate_ref[:, 0:D_MODEL])
    ext_ref[hist_rows:hist_rows + m, :] = proj(2)
    zp_ref[...] = _silu(proj(3))
    gp_ref[...] = jax.nn.sigmoid(proj(5) + b_gate_ref[:, D_MODEL:])

    for blk in range(N_BLOCKS):
        lanes = slice(blk * LANES, (blk + 1) * LANES)
        st = slice(blk * BLOCK_STATE, (blk + 1) * BLOCK_STATE)
        a_re = jnp.broadcast_to(a2_re_ref[:, st], (SUBLANES, BLOCK_STATE))
        a_im = jnp.broadcast_to(a2_im_ref[:, st], (SUBLANES, BLOCK_STATE))
        for j in range(n_b // SUBLANES):
            tile = slice(j * SUBLANES, (j + 1) * SUBLANES)
            s_re = sre_ref[tile, st]
            s_im = sim_ref[tile, st]
            for k in range(n_pairs):
                rows = slice((k + 1) * n_b + j * SUBLANES, (k + 1) * n_b + (j + 1) * SUBLANES)
                n_re = a_re * s_re - a_im * s_im + bu_ref[blk, rows, 0:BLOCK_STATE]
                n_im = a_re * s_im + a_im * s_re + bu_ref[blk, rows, BLOCK_STATE:]
                bu_ref[blk, rows, 0:BLOCK_STATE] = n_re
                bu_ref[blk, rows, BLOCK_STATE:] = n_im
                s_re, s_im = n_re, n_im
            sre_ref[tile, st] = s_re
            sim_ref[tile, st] = s_im
        y2 = (jnp.dot(bu_ref[blk, 0:half, :].astype(BF16), wc_ref[blk], preferred_element_type=F32)
              + jnp.dot(u2_ref[blk], wd_ref[blk], preferred_element_type=F32))
        ys_ref[:, 0:n_b, lanes] = y2[:, 0:LANES].reshape(n_pairs, n_b, LANES)
        ys_ref[:, n_b:, lanes] = y2[:, LANES:].reshape(n_pairs, n_b, LANES)

    if pos0 + 1 >= max(POOL_WINDOWS):
        pos = None
    else:
        t_loc = lax.broadcasted_iota(jnp.int32, (m, 1), 0) // n_b
        pos = pos0 + step * n_t + t_loc
    for gi, w in enumerate(POOL_WINDOWS):
        cols = slice(gi * POOL_GROUP_CH, (gi + 1) * POOL_GROUP_CH)
        s = ext_ref[(POOL_BUF + 1 - w) * n_b:hist_rows + m, cols]
        span = 1
        while span < w:
            keep = s.shape[0] - span * n_b
            s = s[:keep] + s[span * n_b:]
            span *= 2
        if pos is None:
            mean = s * (1.0 / w)
        else:
            mean = s * (1.0 / jnp.minimum(pos + 1, w).astype(F32))
        pooled = mean - ext_ref[hist_rows:hist_rows + m, cols]
        yp_ref[:, cols] = jnp.dot(pooled.astype(BF16), pmix_ref[gi],
                                  preferred_element_type=F32) * pscale_ref[:, cols]
    b = yp_ref[...] * zp_ref[...]
    acc_p = jnp.dot(b.astype(BF16), w_bp_ref[...], preferred_element_type=F32)

    y = ys_ref[...].reshape(m, D_MODEL) + d_ref[...] * us_ref[...].reshape(m, D_MODEL)
    y = jax.nn.gelu(y)
    y = y * jax.nn.sigmoid(
        jnp.dot(y.astype(BF16), w_glu_ref[...], preferred_element_type=F32) + b_glu_ref[...])
    a = y * zs_ref[...]
    acc_s = jnp.dot(a.astype(BF16), w_bs_ref[...], preferred_element_type=F32)
    merged = gs_ref[...] * acc_s + gp_ref[...] * acc_p

    out = h_ref[...] + jnp.dot(merged.astype(BF16), w_out_ref[...], preferred_element_type=F32)
    y_rows = _rmsnorm(out, fgain_ref[...])
    y_ref[...] = jnp.swapaxes(y_rows.reshape(n_t, n_b, D_MODEL), 0, 1)

    for k in range(POOL_BUF):
        ext_ref[k * n_b:(k + 1) * n_b, :] = ext_ref[(k + n_t) * n_b:(k + n_t + 1) * n_b, :]

    def write_state():
        ore_ref[...] = sre_ref[...]
        oim_ref[...] = sim_ref[...]
        obuf_ref[...] = ext_ref[0:hist_rows, :]

    if carry:
        pl.when(step == pl.num_programs(0) - 1)(write_state)
    else:
        write_state()


def _layer_call(x, s0_re, s0_im, buf0, weights, *, n_t, n_b, carry, pos0, name):
    batch, time, _ = x.shape
    m = n_t * n_b
    hist_rows = POOL_BUF * n_b
    if carry:
        assert batch == n_b and time % n_t == 0
        n_steps = time // n_t
        x_map = lambda i: (0, i, 0)
        per_step = lambda i: (0, 0)
    else:
        assert time == n_t and batch % n_b == 0
        n_steps = batch // n_b
        x_map = lambda i: (i, 0, 0)
        per_step = lambda i: (i, 0)
    whole = pl.BlockSpec(memory_space=pltpu.VMEM)
    kernel = functools.partial(_layer_kernel, n_t=n_t, n_b=n_b, carry=carry, pos0=pos0)
    return pl.pallas_call(
        kernel,
        grid=(n_steps,),
        in_specs=[
            pl.BlockSpec((n_b, n_t, D_MODEL), x_map),
            pl.BlockSpec((n_b, STATE_W), per_step),
            pl.BlockSpec((n_b, STATE_W), per_step),
            pl.BlockSpec((hist_rows, D_MODEL), per_step),
        ] + [whole] * len(weights),
        out_specs=[
            pl.BlockSpec((n_b, n_t, D_MODEL), x_map),
            pl.BlockSpec((n_b, STATE_W), per_step),
            pl.BlockSpec((n_b, STATE_W), per_step),
            pl.BlockSpec((hist_rows, D_MODEL), per_step),
        ],
        out_shape=[
            jax.ShapeDtypeStruct((batch, time, D_MODEL), F32),
            jax.ShapeDtypeStruct((batch, STATE_W), F32),
            jax.ShapeDtypeStruct((batch, STATE_W), F32),
            jax.ShapeDtypeStruct((batch // n_b * hist_rows, D_MODEL), F32),
        ],
        scratch_shapes=[
            pltpu.VMEM((n_b, STATE_W), F32),
            pltpu.VMEM((n_b, STATE_W), F32),
            pltpu.VMEM(((POOL_BUF + n_t) * n_b, D_MODEL), F32),
            pltpu.VMEM((N_BLOCKS, m // 2 + n_b, 2 * BLOCK_STATE), F32),
            pltpu.VMEM((N_BLOCKS, m // 2, 2 * LANES), BF16),
            pltpu.VMEM((m, D_MODEL), F32),
            pltpu.VMEM((m, D_MODEL), BF16),
            pltpu.VMEM((n_t // 2, 2 * n_b, D_MODEL), F32),
            pltpu.VMEM((m, D_MODEL), F32),
            pltpu.VMEM((m, D_MODEL), F32),
            pltpu.VMEM((m, D_MODEL), F32),
            pltpu.VMEM((m, D_MODEL), F32),
            pltpu.VMEM((n_t // 2, 2 * n_b, D_MODEL), F32),
            pltpu.VMEM((m, D_MODEL), F32),
        ],
        compiler_params=pltpu.CompilerParams(
            dimension_semantics=("arbitrary",), vmem_limit_bytes=VMEM_LIMIT_BYTES),
        name=name,
    )(x, s0_re, s0_im, buf0, *weights)


def _discretise(a_re, a_im, log_dt, b_re, b_im):
    dt = jnp.exp(log_dt)[:, None]
    mag = jnp.exp(dt * a_re)
    ang = dt * a_im
    abar_re = mag * jnp.cos(ang)
    abar_im = mag * jnp.sin(ang)
    den = a_re * a_re + a_im * a_im
    nr = abar_re - 1.0
    ni = abar_im
    q_re = (nr * a_re + ni * a_im) / den
    q_im = (ni * a_re - nr * a_im) / den
    bbar_re = q_re[..., None] * b_re - q_im[..., None] * b_im
    bbar_im = q_re[..., None] * b_im + q_im[..., None] * b_re
    return abar_re, abar_im, bbar_re, bbar_im


def _pair_weights(abar_re, abar_im, bbar_re, bbar_im, c_re, c_im):
    gpb = GROUPS_PER_BLOCK
    eye = jnp.eye(gpb, dtype=F32).reshape(1, gpb, 1, gpb, 1)
    exact = functools.partial(jnp.einsum, precision=lax.Precision.HIGHEST)

    def block_diag(mat, rows, cols):
        mat = jnp.transpose(mat, (0, 2, 1)).reshape(N_BLOCKS, gpb, rows, 1, cols)
        return (mat * eye).reshape(N_BLOCKS, gpb * rows, gpb * cols)

    expand = functools.partial(block_diag, rows=SSM_GROUP_CH, cols=SSM_STATE)
    contract = functools.partial(block_diag, rows=SSM_STATE, cols=SSM_GROUP_CH)
    direct = functools.partial(block_diag, rows=SSM_GROUP_CH, cols=SSM_GROUP_CH)

    a2_re = abar_re * abar_re - abar_im * abar_im
    a2_im = 2.0 * abar_re * abar_im
    ab_re = abar_re[..., None] * bbar_re - abar_im[..., None] * bbar_im
    ab_im = abar_re[..., None] * bbar_im + abar_im[..., None] * bbar_re
    ca_re = c_re * abar_re[:, None, :] - c_im * abar_im[:, None, :]
    ca_im = c_re * abar_im[:, None, :] + c_im * abar_re[:, None, :]
    ca2_re = c_re * a2_re[:, None, :] - c_im * a2_im[:, None, :]
    ca2_im = c_re * a2_im[:, None, :] + c_im * a2_re[:, None, :]
    cb = exact('gop,gpi->goi', c_re, bbar_re) - exact('gop,gpi->goi', c_im, bbar_im)
    cab = exact('gop,gpi->goi', ca_re, bbar_re) - exact('gop,gpi->goi', ca_im, bbar_im)

    wb = jnp.concatenate([
        jnp.concatenate([expand(ab_re), expand(ab_im)], axis=-1),
        jnp.concatenate([expand(bbar_re), expand(bbar_im)], axis=-1)], axis=1)
    wc = jnp.concatenate([
        jnp.concatenate([contract(ca_re), contract(ca2_re)], axis=-1),
        jnp.concatenate([contract(-ca_im), contract(-ca2_im)], axis=-1)], axis=1)
    wd = jnp.concatenate([
        jnp.concatenate([direct(cb), direct(cab)], axis=-1),
        jnp.concatenate([jnp.zeros_like(direct(cb)), direct(cb)], axis=-1)], axis=1)
    return a2_re, a2_im, wb.astype(BF16), wc.astype(BF16), wd.astype(BF16)


def _rows_kb(buf, n_b):
    batch = buf.shape[0]
    buf = jnp.transpose(buf.reshape(batch // n_b, n_b, POOL_BUF, D_MODEL), (0, 2, 1, 3))
    return buf.reshape(batch * POOL_BUF, D_MODEL)


def _rows_bk(rows, n_b):
    batch = rows.shape[0] // POOL_BUF
    rows = jnp.transpose(rows.reshape(batch // n_b, POOL_BUF, n_b, D_MODEL), (0, 2, 1, 3))
    return rows.reshape(batch, POOL_BUF, D_MODEL)


PROMPT_T = 32
SAMPLE_B = 32


def kernel(x_prompt, x_sample, state_ssm_re, state_ssm_im, state_pool, meta_tokens, norm_gain, w_in, b_gate, ssm_a_re, ssm_a_im, ssm_log_dt, ssm_b_re, ssm_b_im, ssm_c_re, ssm_c_im, ssm_d, w_glu, b_glu, pool_mix, pool_scale, w_branch_ssm, w_branch_pool, w_out, final_norm_gain):
    batch, seq, _ = x_prompt.shape
    dec_batch, dec_seq, _ = x_sample.shape
    depth = norm_gain.shape[0]
    assert depth == 1
    l = 0
    abar_re, abar_im, bbar_re, bbar_im = _discretise(
        ssm_a_re[l], ssm_a_im[l], ssm_log_dt[l], ssm_b_re[l], ssm_b_im[l])
    a2_re, a2_im, wb, wc, wd = _pair_weights(
        abar_re, abar_im, bbar_re, bbar_im, ssm_c_re[l].astype(F32), ssm_c_im[l].astype(F32))
    row = lambda v: v.reshape(1, -1).astype(F32)
    weights = (
        row(norm_gain[l]), w_in[l].astype(BF16), row(b_gate[l]),
        row(a2_re), row(a2_im), wb, wc, wd, row(ssm_d[l]),
        w_glu[l].astype(BF16), row(b_glu[l]), pool_mix[l].astype(BF16), row(pool_scale[l]),
        w_branch_ssm[l].astype(BF16), w_branch_pool[l].astype(BF16), w_out[l].astype(BF16),
        row(final_norm_gain),
    )

    assert batch == SUBLANES
    meta = jnp.broadcast_to(meta_tokens[None].astype(x_prompt.dtype), (batch, N_META, D_MODEL))
    zero_state = jnp.zeros((batch, STATE_W), F32)
    _, m_re, m_im, m_buf = _layer_call(
        meta, zero_state, zero_state, jnp.zeros((POOL_BUF * batch, D_MODEL), F32), weights,
        n_t=N_META, n_b=batch, carry=False, pos0=0, name="meta_layer")
    y_prompt, p_re, p_im, p_buf = _layer_call(
        x_prompt, m_re, m_im, m_buf, weights,
        n_t=PROMPT_T, n_b=batch, carry=True, pos0=N_META, name="prompt_layer")
    new_pool_prompt = _rows_bk(p_buf, batch)[None]

    y_sample, s_re, s_im, s_buf = _layer_call(
        x_sample, state_ssm_re[l].reshape(dec_batch, STATE_W),
        state_ssm_im[l].reshape(dec_batch, STATE_W), _rows_kb(state_pool[l], SAMPLE_B), weights,
        n_t=dec_seq, n_b=SAMPLE_B, carry=False, pos0=PAST_LEN, name="sample_layer")
    new_pool_sample = _rows_bk(s_buf, SAMPLE_B)[None]

    state_shape = (1, -1, SSM_GROUPS, SSM_STATE)
    return (y_prompt, y_sample,
            p_re.reshape(state_shape), p_im.reshape(state_shape), new_pool_prompt,
            s_re.reshape(state_shape), s_im.reshape(state_shape), new_pool_sample)
```

```python
import functools

import jax
import jax.numpy as jnp
from jax import lax
from jax.experimental import pallas as pl
from jax.experimental.pallas import tpu as pltpu

D_MODEL = 1024
N_META = 16
SSM_GROUP_CH = 16
SSM_GROUPS = D_MODEL // SSM_GROUP_CH
SSM_STATE = 64
POOL_WINDOWS = (2, 4, 8, 16)
POOL_GROUP_CH = D_MODEL // len(POOL_WINDOWS)
POOL_BUF = max(POOL_WINDOWS) - 1
EPS = 1e-6
PAST_LEN = 16384

SUBLANES = 8
LANES = 128
GROUPS_PER_BLOCK = LANES // SSM_GROUP_CH
N_BLOCKS = SSM_GROUPS // GROUPS_PER_BLOCK
BLOCK_STATE = GROUPS_PER_BLOCK * SSM_STATE
STATE_W = SSM_GROUPS * SSM_STATE
VMEM_LIMIT_BYTES = 56 * 1024 * 1024

F32 = jnp.float32
BF16 = jnp.bfloat16


def _rmsnorm(x, gain):
    ms = jnp.mean(x * x, axis=-1, keepdims=True)
    return (x * lax.rsqrt(ms + EPS)) * gain


def _silu(x):
    return x * jax.nn.sigmoid(x)


def _layer_kernel(*refs, n_t, n_b, carry, pos0):
    if carry:
        x_ref, x_next_ref, *refs = refs
    else:
        x_ref, *refs = refs
    (s0re_ref, s0im_ref, buf0_ref,
     gain_ref, w_in_ref, b_gate_ref, a2_re_ref, a2_im_ref, wb_ref, wc_ref, wd_ref, d_ref,
     w_glu_ref, b_glu_ref, pmix_ref, pscale_ref, w_bs_ref, w_bp_ref, w_out_ref, fgain_ref,
     y_ref, ore_ref, oim_ref, obuf_ref,
     sre_ref, sim_ref, ext_ref, bu_ref, u2_ref, xn_ref, us_ref, zs_ref, gs_ref, zp_ref,
     gp_ref, ys_ref, yp_ref) = refs
    m = n_t * n_b
    n_pairs = n_t // 2
    half = n_pairs * n_b
    step = pl.program_id(0)
    hist_rows = POOL_BUF * n_b

    def load_state():
        sre_ref[...] = s0re_ref[...]
        sim_ref[...] = s0im_ref[...]
        ext_ref[0:hist_rows, :] = buf0_ref[...]

    def rows_tb(src_ref):
        return jnp.swapaxes(src_ref[...], 0, 1).reshape(m, D_MODEL)

    def proj(k):
        return jnp.dot(xn_ref[...], w_in_ref[:, k * D_MODEL:(k + 1) * D_MODEL],
                       preferred_element_type=F32)

    def prepare_norm(src_ref):
        xn_ref[...] = _rmsnorm(rows_tb(src_ref), gain_ref[...]).astype(BF16)

    def prepare_u():
        us_ref[...] = proj(0).reshape(n_pairs, 2 * n_b, D_MODEL)

    def prepare_slabs():
        for blk in range(N_BLOCKS):
            lanes = slice(blk * LANES, (blk + 1) * LANES)
            st = slice(blk * BLOCK_STATE, (blk + 1) * BLOCK_STATE)
            u_t0 = us_ref[:, 0:n_b, lanes].reshape(half, LANES)
            u_t1 = us_ref[:, n_b:, lanes].reshape(half, LANES)
            u2_ref[blk] = jnp.concatenate([u_t0, u_t1], axis=1).astype(BF16)
            bu_ref[blk, 0:n_b, 0:BLOCK_STATE] = sre_ref[:, st]
            bu_ref[blk, 0:n_b, BLOCK_STATE:] = sim_ref[:, st]
            bu_ref[blk, n_b:, :] = jnp.dot(u2_ref[blk], wb_ref[blk], preferred_element_type=F32)

    def prepare(src_ref):
        prepare_norm(src_ref)
        prepare_u()
        prepare_slabs()

    if carry:
        @pl.when(step == 0)
        def _():
            load_state()
            prepare(x_ref)
    else:
        load_state()
        prepare(x_ref)

    piece = 2 * LANES

    def proj_piece(k, p):
        cols = slice(p * piece, (p + 1) * piece)
        v = jnp.dot(xn_ref[...], w_in_ref[:, k * D_MODEL + p * piece:k * D_MODEL + (p + 1) * piece],
                    preferred_element_type=F32)
        if k == 1:
            zs_ref[:, cols] = _silu(v)
        elif k == 4:
            gs_ref[:, cols] = jax.nn.sigmoid(v + b_gate_ref[:, cols])
        elif k == 2:
            ext_ref[hist_rows:hist_rows + m, cols] = v
        elif k == 3:
            zp_ref[:, cols] = _silu(v)
        else:
            gp_ref[:, cols] = jax.nn.sigmoid(
                v + b_gate_ref[:, D_MODEL + p * piece:D_MODEL + (p + 1) * piece])

    pieces = [(k, p) for k in (1, 4, 2, 3, 5) for p in range(D_MODEL // piece)]

    for blk in range(N_BLOCKS):
        lanes = slice(blk * LANES, (blk + 1) * LANES)
        st = slice(blk * BLOCK_STATE, (blk + 1) * BLOCK_STATE)
        for k, p in pieces[blk * len(pieces) // N_BLOCKS:(blk + 1) * len(pieces) // N_BLOCKS]:
            proj_piece(k, p)
        a_re = jnp.broadcast_to(a2_re_ref[:, st], (SUBLANES, BLOCK_STATE))
        a_im = jnp.broadcast_to(a2_im_ref[:, st], (SUBLANES, BLOCK_STATE))
        for j in range(n_b // SUBLANES):
            tile = slice(j * SUBLANES, (j + 1) * SUBLANES)
            s_re = sre_ref[tile, st]
            s_im = sim_ref[tile, st]
            for k in range(n_pairs):
                rows = slice((k + 1) * n_b + j * SUBLANES, (k + 1) * n_b + (j + 1) * SUBLANES)
                n_re = a_re * s_re - a_im * s_im + bu_ref[blk, rows, 0:BLOCK_STATE]
                n_im = a_re * s_im + a_im * s_re + bu_ref[blk, rows, BLOCK_STATE:]
                bu_ref[blk, rows, 0:BLOCK_STATE] = n_re
                bu_ref[blk, rows, BLOCK_STATE:] = n_im
                s_re, s_im = n_re, n_im
            sre_ref[tile, st] = s_re
            sim_ref[tile, st] = s_im
        y2 = (jnp.dot(bu_ref[blk, 0:half, :].astype(BF16), wc_ref[blk], preferred_element_type=F32)
              + jnp.dot(u2_ref[blk], wd_ref[blk], preferred_element_type=F32))
        ys_ref[:, 0:n_b, lanes] = y2[:, 0:LANES].reshape(n_pairs, n_b, LANES)
        ys_ref[:, n_b:, lanes] = y2[:, LANES:].reshape(n_pairs, n_b, LANES)

    if pos0 + 1 >= max(POOL_WINDOWS):
        pos = None
    else:
        t_loc = lax.broadcasted_iota(jnp.int32, (m, 1), 0) // n_b
        pos = pos0 + step * n_t + t_loc
    for gi, w in enumerate(POOL_WINDOWS):
        cols = slice(gi * POOL_GROUP_CH, (gi + 1) * POOL_GROUP_CH)
        s = ext_ref[(POOL_BUF + 1 - w) * n_b:hist_rows + m, cols]
        span = 1
        while span < w:
            keep = s.shape[0] - span * n_b
            s = s[:keep] + s[span * n_b:]
            span *= 2
        if pos is None:
            mean = s * (1.0 / w)
        else:
            mean = s * (1.0 / jnp.minimum(pos + 1, w).astype(F32))
        pooled = mean - ext_ref[hist_rows:hist_rows + m, cols]
        yp_ref[:, cols] = jnp.dot(pooled.astype(BF16), pmix_ref[gi],
                                  preferred_element_type=F32) * pscale_ref[:, cols]
    for k in range(POOL_BUF):
        ext_ref[k * n_b:(k + 1) * n_b, :] = ext_ref[(k + n_t) * n_b:(k + n_t + 1) * n_b, :]

    y = ys_ref[...].reshape(m, D_MODEL) + d_ref[...] * us_ref[...].reshape(m, D_MODEL)
    y = jax.nn.gelu(y)
    b = yp_ref[...] * zp_ref[...]
    acc_p = jnp.dot(b.astype(BF16), w_bp_ref[...], preferred_element_type=F32)
    glu = jnp.dot(y.astype(BF16), w_glu_ref[...], preferred_element_type=F32)
    if carry:
        prepare_norm(x_next_ref)
    a = y * jax.nn.sigmoid(glu + b_glu_ref[...]) * zs_ref[...]
    acc_s = jnp.dot(a.astype(BF16), w_bs_ref[...], preferred_element_type=F32)
    if carry:
        prepare_u()
    merged = gs_ref[...] * acc_s + gp_ref[...] * acc_p
    delta = jnp.dot(merged.astype(BF16), w_out_ref[...], preferred_element_type=F32)
    if carry:
        prepare_slabs()
    y_rows = _rmsnorm(rows_tb(x_ref) + delta, fgain_ref[...])
    y_ref[...] = jnp.swapaxes(y_rows.reshape(n_t, n_b, D_MODEL), 0, 1)

    def write_state():
        ore_ref[...] = sre_ref[...]
        oim_ref[...] = sim_ref[...]
        obuf_ref[...] = ext_ref[0:hist_rows, :]

    if carry:
        pl.when(step == pl.num_programs(0) - 1)(write_state)
    else:
        write_state()


def _layer_call(x, s0_re, s0_im, buf0, weights, *, n_t, n_b, carry, pos0, name):
    batch, time, _ = x.shape
    m = n_t * n_b
    hist_rows = POOL_BUF * n_b
    x_block = (n_b, n_t, D_MODEL)
    if carry:
        assert batch == n_b and time % n_t == 0
        n_steps = time // n_t
        x_specs = [pl.BlockSpec(x_block, lambda i: (0, i, 0)),
                   pl.BlockSpec(x_block, lambda i: (0, jnp.minimum(i + 1, n_steps - 1), 0))]
        y_map = lambda i: (0, i, 0)
        per_step = lambda i: (0, 0)
        xs = (x, x)
    else:
        assert time == n_t and batch % n_b == 0
        n_steps = batch // n_b
        x_specs = [pl.BlockSpec(x_block, lambda i: (i, 0, 0))]
        y_map = lambda i: (i, 0, 0)
        per_step = lambda i: (i, 0)
        xs = (x,)
    whole = pl.BlockSpec(memory_space=pltpu.VMEM)
    kernel = functools.partial(_layer_kernel, n_t=n_t, n_b=n_b, carry=carry, pos0=pos0)
    return pl.pallas_call(
        kernel,
        grid=(n_steps,),
        in_specs=x_specs + [
            pl.BlockSpec((n_b, STATE_W), per_step),
            pl.BlockSpec((n_b, STATE_W), per_step),
            pl.BlockSpec((hist_rows, D_MODEL), per_step),
        ] + [whole] * len(weights),
        out_specs=[
            pl.BlockSpec(x_block, y_map),
            pl.BlockSpec((n_b, STATE_W), per_step),
            pl.BlockSpec((n_b, STATE_W), per_step),
            pl.BlockSpec((hist_rows, D_MODEL), per_step),
        ],
        out_shape=[
            jax.ShapeDtypeStruct((batch, time, D_MODEL), F32),
            jax.ShapeDtypeStruct((batch, STATE_W), F32),
            jax.ShapeDtypeStruct((batch, STATE_W), F32),
            jax.ShapeDtypeStruct((batch // n_b * hist_rows, D_MODEL), F32),
        ],
        scratch_shapes=[
            pltpu.VMEM((n_b, STATE_W), F32),
            pltpu.VMEM((n_b, STATE_W), F32),
            pltpu.VMEM(((POOL_BUF + n_t) * n_b, D_MODEL), F32),
            pltpu.VMEM((N_BLOCKS, m // 2 + n_b, 2 * BLOCK_STATE), F32),
            pltpu.VMEM((N_BLOCKS, m // 2, 2 * LANES), BF16),
            pltpu.VMEM((m, D_MODEL), BF16),
            pltpu.VMEM((n_t // 2, 2 * n_b, D_MODEL), F32),
            pltpu.VMEM((m, D_MODEL), F32),
            pltpu.VMEM((m, D_MODEL), F32),
            pltpu.VMEM((m, D_MODEL), F32),
            pltpu.VMEM((m, D_MODEL), F32),
            pltpu.VMEM((n_t // 2, 2 * n_b, D_MODEL), F32),
            pltpu.VMEM((m, D_MODEL), F32),
        ],
        compiler_params=pltpu.CompilerParams(
            dimension_semantics=("arbitrary",), vmem_limit_bytes=VMEM_LIMIT_BYTES),
        name=name,
    )(*xs, s0_re, s0_im, buf0, *weights)


def _discretise(a_re, a_im, log_dt, b_re, b_im):
    dt = jnp.exp(log_dt)[:, None]
    mag = jnp.exp(dt * a_re)
    ang = dt * a_im
    abar_re = mag * jnp.cos(ang)
    abar_im = mag * jnp.sin(ang)
    den = a_re * a_re + a_im * a_im
    nr = abar_re - 1.0
    ni = abar_im
    q_re = (nr * a_re + ni * a_im) / den
    q_im = (ni * a_re - nr * a_im) / den
    bbar_re = q_re[..., None] * b_re - q_im[..., None] * b_im
    bbar_im = q_re[..., None] * b_im + q_im[..., None] * b_re
    return abar_re, abar_im, bbar_re, bbar_im


def _pair_weights(abar_re, abar_im, bbar_re, bbar_im, c_re, c_im):
    gpb = GROUPS_PER_BLOCK
    eye = jnp.eye(gpb, dtype=F32).reshape(1, gpb, 1, gpb, 1)
    exact = functools.partial(jnp.einsum, precision=lax.Precision.HIGHEST)

    def block_diag(mat, rows, cols):
        mat = jnp.transpose(mat, (0, 2, 1)).reshape(N_BLOCKS, gpb, rows, 1, cols)
        return (mat * eye).reshape(N_BLOCKS, gpb * rows, gpb * cols)

    expand = functools.partial(block_diag, rows=SSM_GROUP_CH, cols=SSM_STATE)
    contract = functools.partial(block_diag, rows=SSM_STATE, cols=SSM_GROUP_CH)
    direct = functools.partial(block_diag, rows=SSM_GROUP_CH, cols=SSM_GROUP_CH)

    a2_re = abar_re * abar_re - abar_im * abar_im
    a2_im = 2.0 * abar_re * abar_im
    ab_re = abar_re[..., None] * bbar_re - abar_im[..., None] * bbar_im
    ab_im = abar_re[..., None] * bbar_im + abar_im[..., None] * bbar_re
    ca_re = c_re * abar_re[:, None, :] - c_im * abar_im[:, None, :]
    ca_im = c_re * abar_im[:, None, :] + c_im * abar_re[:, None, :]
    ca2_re = c_re * a2_re[:, None, :] - c_im * a2_im[:, None, :]
    ca2_im = c_re * a2_im[:, None, :] + c_im * a2_re[:, None, :]
    cb = exact('gop,gpi->goi', c_re, bbar_re) - exact('gop,gpi->goi', c_im, bbar_im)
    cab = exact('gop,gpi->goi', ca_re, bbar_re) - exact('gop,gpi->goi', ca_im, bbar_im)

    wb = jnp.concatenate([
        jnp.concatenate([expand(ab_re), expand(ab_im)], axis=-1),
        jnp.concatenate([expand(bbar_re), expand(bbar_im)], axis=-1)], axis=1)
    wc = jnp.concatenate([
        jnp.concatenate([contract(ca_re), contract(ca2_re)], axis=-1),
        jnp.concatenate([contract(-ca_im), contract(-ca2_im)], axis=-1)], axis=1)
    wd = jnp.concatenate([
        jnp.concatenate([direct(cb), direct(cab)], axis=-1),
        jnp.concatenate([jnp.zeros_like(direct(cb)), direct(cb)], axis=-1)], axis=1)
    return a2_re, a2_im, wb.astype(BF16), wc.astype(BF16), wd.astype(BF16)


def _rows_kb(buf, n_b):
    batch = buf.shape[0]
    buf = jnp.transpose(buf.reshape(batch // n_b, n_b, POOL_BUF, D_MODEL), (0, 2, 1, 3))
    return buf.reshape(batch * POOL_BUF, D_MODEL)


def _rows_bk(rows, n_b):
    batch = rows.shape[0] // POOL_BUF
    rows = jnp.transpose(rows.reshape(batch // n_b, POOL_BUF, n_b, D_MODEL), (0, 2, 1, 3))
    return rows.reshape(batch, POOL_BUF, D_MODEL)


PROMPT_T = 32
SAMPLE_B = 32


def kernel(x_prompt, x_sample, state_ssm_re, state_ssm_im, state_pool, meta_tokens, norm_gain, w_in, b_gate, ssm_a_re, ssm_a_im, ssm_log_dt, ssm_b_re, ssm_b_im, ssm_c_re, ssm_c_im, ssm_d, w_glu, b_glu, pool_mix, pool_scale, w_branch_ssm, w_branch_pool, w_out, final_norm_gain):
    batch, seq, _ = x_prompt.shape
    dec_batch, dec_seq, _ = x_sample.shape
    depth = norm_gain.shape[0]
    assert depth == 1
    l = 0
    abar_re, abar_im, bbar_re, bbar_im = _discretise(
        ssm_a_re[l], ssm_a_im[l], ssm_log_dt[l], ssm_b_re[l], ssm_b_im[l])
    a2_re, a2_im, wb, wc, wd = _pair_weights(
        abar_re, abar_im, bbar_re, bbar_im, ssm_c_re[l].astype(F32), ssm_c_im[l].astype(F32))
    row = lambda v: v.reshape(1, -1).astype(F32)
    weights = (
        row(norm_gain[l]), w_in[l].astype(BF16), row(b_gate[l]),
        row(a2_re), row(a2_im), wb, wc, wd, row(ssm_d[l]),
        w_glu[l].astype(BF16), row(b_glu[l]), pool_mix[l].astype(BF16), row(pool_scale[l]),
        w_branch_ssm[l].astype(BF16), w_branch_pool[l].astype(BF16), w_out[l].astype(BF16),
        row(final_norm_gain),
    )

    assert batch == SUBLANES
    meta = jnp.broadcast_to(meta_tokens[None].astype(x_prompt.dtype), (batch, N_META, D_MODEL))
    zero_state = jnp.zeros((batch, STATE_W), F32)
    _, m_re, m_im, m_buf = _layer_call(
        meta, zero_state, zero_state, jnp.zeros((POOL_BUF * batch, D_MODEL), F32), weights,
        n_t=N_META, n_b=batch, carry=False, pos0=0, name="meta_layer")
    y_prompt, p_re, p_im, p_buf = _layer_call(
        x_prompt, m_re, m_im, m_buf, weights,
        n_t=PROMPT_T, n_b=batch, carry=True, pos0=N_META, name="prompt_layer")
    new_pool_prompt = _rows_bk(p_buf, batch)[None]

    y_sample, s_re, s_im, s_buf = _layer_call(
        x_sample, state_ssm_re[l].reshape(dec_batch, STATE_W),
        state_ssm_im[l].reshape(dec_batch, STATE_W), _rows_kb(state_pool[l], SAMPLE_B), weights,
        n_t=dec_seq, n_b=SAMPLE_B, carry=False, pos0=PAST_LEN, name="sample_layer")
    new_pool_sample = _rows_bk(s_buf, SAMPLE_B)[None]

    state_shape = (1, -1, SSM_GROUPS, SSM_STATE)
    return (y_prompt, y_sample,
            p_re.reshape(state_shape), p_im.reshape(state_shape), new_pool_prompt,
            s_re.reshape(state_shape), s_im.reshape(state_shape), new_pool_sample)
```

```python
import functools

import jax
import jax.numpy as jnp
from jax import lax
from jax.experimental import pallas as pl
from jax.experimental.pallas import tpu as pltpu

D_MODEL = 1024
N_META = 16
SSM_GROUP_CH = 16
SSM_GROUPS = D_MODEL // SSM_GROUP_CH
SSM_STATE = 64
POOL_WINDOWS = (2, 4, 8, 16)
POOL_GROUP_CH = D_MODEL // len(POOL_WINDOWS)
POOL_BUF = max(POOL_WINDOWS) - 1
EPS = 1e-6
PAST_LEN = 16384

SUBLANES = 8
LANES = 128
GROUPS_PER_BLOCK = LANES // SSM_GROUP_CH
N_BLOCKS = SSM_GROUPS // GROUPS_PER_BLOCK
BLOCK_STATE = GROUPS_PER_BLOCK * SSM_STATE
STATE_W = SSM_GROUPS * SSM_STATE
VMEM_LIMIT_BYTES = 56 * 1024 * 1024

F32 = jnp.float32
BF16 = jnp.bfloat16


def _rmsnorm(x, gain):
    ms = jnp.mean(x * x, axis=-1, keepdims=True)
    return (x * lax.rsqrt(ms + EPS)) * gain


_sigmoid = jax.nn.sigmoid


def _silu(x):
    return x * _sigmoid(x)


def _layer_kernel(*refs, n_t, n_b, carry, pos0):
    if carry:
        x_ref, x_next_ref, *refs = refs
    else:
        x_ref, *refs = refs
    (s0re_ref, s0im_ref, buf0_ref,
     gain_ref, w_in_ref, b_gate_ref, a2_re_ref, a2_im_ref, wb_ref, wc_ref, wd_ref, d_ref,
     w_glu_ref, b_glu_ref, pmix_ref, pscale_ref, w_bs_ref, w_bp_ref, w_out_ref, fgain_ref,
     y_ref, ore_ref, oim_ref, obuf_ref,
     sre_ref, sim_ref, ext_ref, bu_ref, u2_ref, xn_ref, us_ref, zs_ref, gs_ref, zp_ref,
     gp_ref, ys_ref, yp_ref) = refs
    m = n_t * n_b
    n_pairs = n_t // 2
    half = n_pairs * n_b
    step = pl.program_id(0)
    hist_rows = POOL_BUF * n_b

    def load_state():
        sre_ref[...] = s0re_ref[...]
        sim_ref[...] = s0im_ref[...]
        ext_ref[0:hist_rows, :] = buf0_ref[...]

    def rows_tb(src_ref):
        return jnp.swapaxes(src_ref[...], 0, 1).reshape(m, D_MODEL)

    def proj(k):
        return jnp.dot(xn_ref[...], w_in_ref[:, k * D_MODEL:(k + 1) * D_MODEL],
                       preferred_element_type=F32)

    def prepare_norm(src_ref):
        xn_ref[...] = _rmsnorm(rows_tb(src_ref), gain_ref[...]).astype(BF16)

    def prepare_u():
        us_ref[...] = proj(0).reshape(n_pairs, 2 * n_b, D_MODEL)

    def prepare_slabs():
        for blk in range(N_BLOCKS):
            lanes = slice(blk * LANES, (blk + 1) * LANES)
            st = slice(blk * BLOCK_STATE, (blk + 1) * BLOCK_STATE)
            u_t0 = us_ref[:, 0:n_b, lanes].reshape(half, LANES)
            u_t1 = us_ref[:, n_b:, lanes].reshape(half, LANES)
            u2_ref[blk] = jnp.concatenate([u_t0, u_t1], axis=1).astype(BF16)
            bu_ref[blk, 0:n_b, 0:BLOCK_STATE] = sre_ref[:, st]
            bu_ref[blk, 0:n_b, BLOCK_STATE:] = sim_ref[:, st]
            bu_ref[blk, n_b:, :] = jnp.dot(u2_ref[blk], wb_ref[blk], preferred_element_type=F32)

    def prepare(src_ref):
        prepare_norm(src_ref)
        prepare_u()
        prepare_slabs()

    if carry:
        @pl.when(step == 0)
        def _():
            load_state()
            prepare(x_ref)
    else:
        load_state()
        prepare(x_ref)

    piece = 2 * LANES

    def proj_piece(k, p):
        cols = slice(p * piece, (p + 1) * piece)
        v = jnp.dot(xn_ref[...], w_in_ref[:, k * D_MODEL + p * piece:k * D_MODEL + (p + 1) * piece],
                    preferred_element_type=F32)
        if k == 1:
            zs_ref[:, cols] = _silu(v)
        elif k == 4:
            gs_ref[:, cols] = _sigmoid(v + b_gate_ref[:, cols])
        elif k == 2:
            ext_ref[hist_rows:hist_rows + m, cols] = v
        elif k == 3:
            zp_ref[:, cols] = _silu(v)
        else:
            gp_ref[:, cols] = _sigmoid(
                v + b_gate_ref[:, D_MODEL + p * piece:D_MODEL + (p + 1) * piece])

    pieces = [(k, p) for k in (1, 4, 2, 3, 5) for p in range(D_MODEL // piece)]

    for blk in range(N_BLOCKS):
        lanes = slice(blk * LANES, (blk + 1) * LANES)
        st = slice(blk * BLOCK_STATE, (blk + 1) * BLOCK_STATE)
        for k, p in pieces[blk * len(pieces) // N_BLOCKS:(blk + 1) * len(pieces) // N_BLOCKS]:
            proj_piece(k, p)
        a_re = jnp.broadcast_to(a2_re_ref[:, st], (SUBLANES, BLOCK_STATE))
        a_im = jnp.broadcast_to(a2_im_ref[:, st], (SUBLANES, BLOCK_STATE))
        for j in range(n_b // SUBLANES):
            tile = slice(j * SUBLANES, (j + 1) * SUBLANES)
            s_re = sre_ref[tile, st]
            s_im = sim_ref[tile, st]
            for k in range(n_pairs):
                rows = slice((k + 1) * n_b + j * SUBLANES, (k + 1) * n_b + (j + 1) * SUBLANES)
                n_re = a_re * s_re - a_im * s_im + bu_ref[blk, rows, 0:BLOCK_STATE]
                n_im = a_re * s_im + a_im * s_re + bu_ref[blk, rows, BLOCK_STATE:]
                bu_ref[blk, rows, 0:BLOCK_STATE] = n_re
                bu_ref[blk, rows, BLOCK_STATE:] = n_im
                s_re, s_im = n_re, n_im
            sre_ref[tile, st] = s_re
            sim_ref[tile, st] = s_im
        y2 = (jnp.dot(bu_ref[blk, 0:half, :].astype(BF16), wc_ref[blk], preferred_element_type=F32)
              + jnp.dot(u2_ref[blk], wd_ref[blk], preferred_element_type=F32))
        ys_ref[:, 0:n_b, lanes] = y2[:, 0:LANES].reshape(n_pairs, n_b, LANES)
        ys_ref[:, n_b:, lanes] = y2[:, LANES:].reshape(n_pairs, n_b, LANES)

    if pos0 + 1 >= max(POOL_WINDOWS):
        pos = None
    else:
        t_loc = lax.broadcasted_iota(jnp.int32, (m, 1), 0) // n_b
        pos = pos0 + step * n_t + t_loc
    for gi, w in enumerate(POOL_WINDOWS):
        cols = slice(gi * POOL_GROUP_CH, (gi + 1) * POOL_GROUP_CH)
        s = ext_ref[(POOL_BUF + 1 - w) * n_b:hist_rows + m, cols]
        span = 1
        while span < w:
            keep = s.shape[0] - span * n_b
            s = s[:keep] + s[span * n_b:]
            span *= 2
        if pos is None:
            mean = s * (1.0 / w)
        else:
            mean = s * (1.0 / jnp.minimum(pos + 1, w).astype(F32))
        pooled = mean - ext_ref[hist_rows:hist_rows + m, cols]
        yp_ref[:, cols] = jnp.dot(pooled.astype(BF16), pmix_ref[gi],
                                  preferred_element_type=F32) * pscale_ref[:, cols]
    for k in range(POOL_BUF):
        ext_ref[k * n_b:(k + 1) * n_b, :] = ext_ref[(k + n_t) * n_b:(k + n_t + 1) * n_b, :]

    y = ys_ref[...].reshape(m, D_MODEL) + d_ref[...] * us_ref[...].reshape(m, D_MODEL)
    y = jax.nn.gelu(y)
    b = yp_ref[...] * zp_ref[...]
    acc_p = jnp.dot(b.astype(BF16), w_bp_ref[...], preferred_element_type=F32)
    glu = jnp.dot(y.astype(BF16), w_glu_ref[...], preferred_element_type=F32)
    if carry:
        prepare_norm(x_next_ref)
    a = y * _sigmoid(glu + b_glu_ref[...]) * zs_ref[...]
    acc_s = jnp.dot(a.astype(BF16), w_bs_ref[...], preferred_element_type=F32)
    if carry:
        prepare_u()
    merged = gs_ref[...] * acc_s + gp_ref[...] * acc_p
    delta = jnp.dot(merged.astype(BF16), w_out_ref[...], preferred_element_type=F32)
    if carry:
        prepare_slabs()
    y_rows = _rmsnorm(rows_tb(x_ref) + delta, fgain_ref[...])
    y_ref[...] = jnp.swapaxes(y_rows.reshape(n_t, n_b, D_MODEL), 0, 1)

    def write_state():
        ore_ref[...] = sre_ref[...]
        oim_ref[...] = sim_ref[...]
        obuf_ref[...] = ext_ref[0:hist_rows, :]

    if carry:
        pl.when(step == pl.num_programs(0) - 1)(write_state)
    else:
        write_state()


def _layer_call(x, s0_re, s0_im, buf0, weights, *, n_t, n_b, carry, pos0, name):
    batch, time, _ = x.shape
    m = n_t * n_b
    hist_rows = POOL_BUF * n_b
    x_block = (n_b, n_t, D_MODEL)
    if carry:
        assert batch == n_b and time % n_t == 0
        n_steps = time // n_t
        x_specs = [pl.BlockSpec(x_block, lambda i: (0, i, 0)),
                   pl.BlockSpec(x_block, lambda i: (0, jnp.minimum(i + 1, n_steps - 1), 0))]
        y_map = lambda i: (0, i, 0)
        per_step = lambda i: (0, 0)
        xs = (x, x)
    else:
        assert time == n_t and batch % n_b == 0
        n_steps = batch // n_b
        x_specs = [pl.BlockSpec(x_block, lambda i: (i, 0, 0))]
        y_map = lambda i: (i, 0, 0)
        per_step = lambda i: (i, 0)
        xs = (x,)
    whole = pl.BlockSpec(memory_space=pltpu.VMEM)
    kernel = functools.partial(_layer_kernel, n_t=n_t, n_b=n_b, carry=carry, pos0=pos0)
    return pl.pallas_call(
        kernel,
        grid=(n_steps,),
        in_specs=x_specs + [
            pl.BlockSpec((n_b, STATE_W), per_step),
            pl.BlockSpec((n_b, STATE_W), per_step),
            pl.BlockSpec((hist_rows, D_MODEL), per_step),
        ] + [whole] * len(weights),
        out_specs=[
            pl.BlockSpec(x_block, y_map),
            pl.BlockSpec((n_b, STATE_W), per_step),
            pl.BlockSpec((n_b, STATE_W), per_step),
            pl.BlockSpec((hist_rows, D_MODEL), per_step),
        ],
        out_shape=[
            jax.ShapeDtypeStruct((batch, time, D_MODEL), F32),
            jax.ShapeDtypeStruct((batch, STATE_W), F32),
            jax.ShapeDtypeStruct((batch, STATE_W), F32),
            jax.ShapeDtypeStruct((batch // n_b * hist_rows, D_MODEL), F32),
        ],
        scratch_shapes=[
            pltpu.VMEM((n_b, STATE_W), F32),
            pltpu.VMEM((n_b, STATE_W), F32),
            pltpu.VMEM(((POOL_BUF + n_t) * n_b, D_MODEL), F32),
            pltpu.VMEM((N_BLOCKS, m // 2 + n_b, 2 * BLOCK_STATE), F32),
            pltpu.VMEM((N_BLOCKS, m // 2, 2 * LANES), BF16),
            pltpu.VMEM((m, D_MODEL), BF16),
            pltpu.VMEM((n_t // 2, 2 * n_b, D_MODEL), F32),
            pltpu.VMEM((m, D_MODEL), F32),
            pltpu.VMEM((m, D_MODEL), F32),
            pltpu.VMEM((m, D_MODEL), F32),
            pltpu.VMEM((m, D_MODEL), F32),
            pltpu.VMEM((n_t // 2, 2 * n_b, D_MODEL), F32),
            pltpu.VMEM((m, D_MODEL), F32),
        ],
        compiler_params=pltpu.CompilerParams(
            dimension_semantics=("arbitrary",), vmem_limit_bytes=VMEM_LIMIT_BYTES),
        name=name,
    )(*xs, s0_re, s0_im, buf0, *weights)


def _block_diag(parts, rows, cols):
    gpb = GROUPS_PER_BLOCK
    flat = [p.reshape(N_BLOCKS, gpb * rows, cols) for p in parts]
    tiled = jnp.concatenate([f for f in flat for _ in range(gpb)], axis=-1)
    r = lax.broadcasted_iota(jnp.int32, tiled.shape[1:], 0) // rows
    c = (lax.broadcasted_iota(jnp.int32, tiled.shape[1:], 1) // cols) % gpb
    return jnp.where(r == c, tiled, 0.0)


def _ssm_weights(a_re, a_im, log_dt, b_re, b_im, c_re, c_im):
    n_c, n_p = SSM_GROUP_CH, SSM_STATE
    dt = jnp.exp(log_dt)[:, None]
    mag = jnp.exp(dt * a_re)
    ang = dt * a_im
    abar_re = mag * jnp.cos(ang)
    abar_im = mag * jnp.sin(ang)
    den = a_re * a_re + a_im * a_im
    nr = abar_re - 1.0
    ni = abar_im
    q_re = ((nr * a_re + ni * a_im) / den)[:, None, :]
    q_im = ((ni * a_re - nr * a_im) / den)[:, None, :]
    a2_re = abar_re * abar_re - abar_im * abar_im
    a2_im = 2.0 * abar_re * abar_im

    bt_re = jnp.swapaxes(b_re, 1, 2)
    bt_im = jnp.swapaxes(b_im, 1, 2)
    bbar_re = q_re * bt_re - q_im * bt_im
    bbar_im = q_re * bt_im + q_im * bt_re
    ar, ai = abar_re[:, None, :], abar_im[:, None, :]
    ab_re = ar * bbar_re - ai * bbar_im
    ab_im = ar * bbar_im + ai * bbar_re

    ct_re = jnp.swapaxes(c_re, 1, 2)
    ct_im = jnp.swapaxes(c_im, 1, 2)
    ar, ai = abar_re[:, :, None], abar_im[:, :, None]
    ca_re = ct_re * ar - ct_im * ai
    ca_im = ct_re * ai + ct_im * ar
    ar2, ai2 = a2_re[:, :, None], a2_im[:, :, None]
    ca2_re = ct_re * ar2 - ct_im * ai2
    ca2_im = ct_re * ai2 + ct_im * ar2

    lhs = jnp.concatenate([bbar_re, bbar_im], axis=-1)
    rhs = jnp.concatenate([jnp.concatenate([ct_re, ca_re], axis=-1),
                           jnp.concatenate([-ct_im, -ca_im], axis=-1)], axis=1)
    direct = jnp.einsum('gip,gpo->gio', lhs, rhs, precision=lax.Precision.HIGHEST)
    cb, cab = direct[..., :n_c], direct[..., n_c:]

    wb = jnp.concatenate([_block_diag([ab_re, ab_im], n_c, n_p),
                          _block_diag([bbar_re, bbar_im], n_c, n_p)], axis=1)
    wc = jnp.concatenate([_block_diag([ca_re, ca2_re], n_p, n_c),
                          _block_diag([-ca_im, -ca2_im], n_p, n_c)], axis=1)
    wd = jnp.concatenate([_block_diag([cb, cab], n_c, n_c),
                          _block_diag([jnp.zeros_like(cb), cb], n_c, n_c)], axis=1)
    return a2_re, a2_im, wb.astype(BF16), wc.astype(BF16), wd.astype(BF16)


def _rows_kb(buf, n_b):
    batch = buf.shape[0]
    buf = jnp.transpose(buf.reshape(batch // n_b, n_b, POOL_BUF, D_MODEL), (0, 2, 1, 3))
    return buf.reshape(batch * POOL_BUF, D_MODEL)


def _rows_bk(rows, n_b):
    batch = rows.shape[0] // POOL_BUF
    rows = jnp.transpose(rows.reshape(batch // n_b, POOL_BUF, n_b, D_MODEL), (0, 2, 1, 3))
    return rows.reshape(batch, POOL_BUF, D_MODEL)


PROMPT_T = 32
SAMPLE_B = 32


def kernel(x_prompt, x_sample, state_ssm_re, state_ssm_im, state_pool, meta_tokens, norm_gain, w_in, b_gate, ssm_a_re, ssm_a_im, ssm_log_dt, ssm_b_re, ssm_b_im, ssm_c_re, ssm_c_im, ssm_d, w_glu, b_glu, pool_mix, pool_scale, w_branch_ssm, w_branch_pool, w_out, final_norm_gain):
    batch, seq, _ = x_prompt.shape
    dec_batch, dec_seq, _ = x_sample.shape
    depth = norm_gain.shape[0]
    assert depth == 1
    l = 0
    a2_re, a2_im, wb, wc, wd = _ssm_weights(
        ssm_a_re[l], ssm_a_im[l], ssm_log_dt[l], ssm_b_re[l], ssm_b_im[l],
        ssm_c_re[l], ssm_c_im[l])
    row = lambda v: v.reshape(1, -1).astype(F32)
    weights = (
        row(norm_gain[l]), w_in[l].astype(BF16), row(b_gate[l]),
        row(a2_re), row(a2_im), wb, wc, wd, row(ssm_d[l]),
        w_glu[l].astype(BF16), row(b_glu[l]), pool_mix[l].astype(BF16), row(pool_scale[l]),
        w_branch_ssm[l].astype(BF16), w_branch_pool[l].astype(BF16), w_out[l].astype(BF16),
        row(final_norm_gain),
    )

    assert batch == SUBLANES
    meta = jnp.broadcast_to(meta_tokens[None].astype(x_prompt.dtype), (batch, N_META, D_MODEL))
    zero_state = jnp.zeros((batch, STATE_W), F32)
    _, m_re, m_im, m_buf = _layer_call(
        meta, zero_state, zero_state, jnp.zeros((POOL_BUF * batch, D_MODEL), F32), weights,
        n_t=N_META, n_b=batch, carry=False, pos0=0, name="meta_layer")
    y_prompt, p_re, p_im, p_buf = _layer_call(
        x_prompt, m_re, m_im, m_buf, weights,
        n_t=PROMPT_T, n_b=batch, carry=True, pos0=N_META, name="prompt_layer")
    new_pool_prompt = _rows_bk(p_buf, batch)[None]

    y_sample, s_re, s_im, s_buf = _layer_call(
        x_sample, state_ssm_re[l].reshape(dec_batch, STATE_W),
        state_ssm_im[l].reshape(dec_batch, STATE_W), _rows_kb(state_pool[l], SAMPLE_B), weights,
        n_t=dec_seq, n_b=SAMPLE_B, carry=False, pos0=PAST_LEN, name="sample_layer")
    new_pool_sample = _rows_bk(s_buf, SAMPLE_B)[None]

    state_shape = (1, -1, SSM_GROUPS, SSM_STATE)
    return (y_prompt, y_sample,
            p_re.reshape(state_shape), p_im.reshape(state_shape), new_pool_prompt,
            s_re.reshape(state_shape), s_im.reshape(state_shape), new_pool_sample)
```

```python
import functools

import jax
import jax.numpy as jnp
import numpy as np
from jax import lax
from jax.experimental import pallas as pl
from jax.experimental.pallas import tpu as pltpu

D_MODEL = 1024
N_META = 16
SSM_GROUP_CH = 16
SSM_GROUPS = D_MODEL // SSM_GROUP_CH
SSM_STATE = 64
POOL_WINDOWS = (2, 4, 8, 16)
POOL_GROUP_CH = D_MODEL // len(POOL_WINDOWS)
POOL_BUF = max(POOL_WINDOWS) - 1
EPS = 1e-6
PAST_LEN = 16384

SUBLANES = 8
LANES = 128
GROUPS_PER_BLOCK = LANES // SSM_GROUP_CH
N_BLOCKS = SSM_GROUPS // GROUPS_PER_BLOCK
BLOCK_STATE = GROUPS_PER_BLOCK * SSM_STATE
STATE_W = SSM_GROUPS * SSM_STATE
VMEM_LIMIT_BYTES = 56 * 1024 * 1024

F32 = jnp.float32
BF16 = jnp.bfloat16


def _rmsnorm(x, gain):
    ms = jnp.mean(x * x, axis=-1, keepdims=True)
    return (x * lax.rsqrt(ms + EPS)) * gain


_sigmoid = jax.nn.sigmoid


def _silu(x):
    return x * _sigmoid(x)


def _layer_kernel(*refs, n_t, n_b, carry, pos0, zero_state):
    if carry:
        x_ref, x_next_ref, *refs = refs
    else:
        x_ref, *refs = refs
    if not zero_state:
        s0re_ref, s0im_ref, buf0_ref, *refs = refs
    (gain_ref, w_in_ref, b_gate_ref, a2_re_ref, a2_im_ref, wb_ref, wc_ref, wd_ref, d_ref,
     w_glu_ref, b_glu_ref, pmix_ref, pscale_ref, w_bs_ref, w_bp_ref, w_out_ref, fgain_ref,
     y_ref, ore_ref, oim_ref, obuf_ref,
     sre_ref, sim_ref, ext_ref, bu_ref, u2_ref, xn_ref, us_ref, zs_ref, gs_ref, zp_ref,
     gp_ref, ys_ref, yp_ref) = refs
    m = n_t * n_b
    n_pairs = n_t // 2
    half = n_pairs * n_b
    step = pl.program_id(0)
    hist_rows = POOL_BUF * n_b

    def load_state():
        if zero_state:
            sre_ref[...] = jnp.zeros(sre_ref.shape, F32)
            sim_ref[...] = jnp.zeros(sim_ref.shape, F32)
            ext_ref[0:hist_rows, :] = jnp.zeros((hist_rows, D_MODEL), F32)
        else:
            sre_ref[...] = s0re_ref[...]
            sim_ref[...] = s0im_ref[...]
            ext_ref[0:hist_rows, :] = buf0_ref[...]

    def rows_tb(src_ref):
        return jnp.swapaxes(src_ref[...], 0, 1).reshape(m, D_MODEL)

    def proj(k):
        return jnp.dot(xn_ref[...], w_in_ref[:, k * D_MODEL:(k + 1) * D_MODEL],
                       preferred_element_type=F32)

    def prepare_norm(src_ref):
        xn_ref[...] = _rmsnorm(rows_tb(src_ref), gain_ref[...]).astype(BF16)

    def prepare_u():
        us_ref[...] = proj(0).reshape(n_pairs, 2 * n_b, D_MODEL)

    def prepare_slabs():
        for blk in range(N_BLOCKS):
            lanes = slice(blk * LANES, (blk + 1) * LANES)
            st = slice(blk * BLOCK_STATE, (blk + 1) * BLOCK_STATE)
            u_t0 = us_ref[:, 0:n_b, lanes].reshape(half, LANES)
            u_t1 = us_ref[:, n_b:, lanes].reshape(half, LANES)
            u2_ref[blk] = jnp.concatenate([u_t0, u_t1], axis=1).astype(BF16)
            bu_ref[blk, 0:n_b, 0:BLOCK_STATE] = sre_ref[:, st]
            bu_ref[blk, 0:n_b, BLOCK_STATE:] = sim_ref[:, st]
            bu_ref[blk, n_b:, :] = jnp.dot(u2_ref[blk], wb_ref[blk], preferred_element_type=F32)

    def prepare(src_ref):
        prepare_norm(src_ref)
        prepare_u()
        prepare_slabs()

    if carry:
        @pl.when(step == 0)
        def _():
            load_state()
            prepare(x_ref)
    else:
        load_state()
        prepare(x_ref)

    piece = 2 * LANES

    def proj_piece(k, p):
        cols = slice(p * piece, (p + 1) * piece)
        v = jnp.dot(xn_ref[...], w_in_ref[:, k * D_MODEL + p * piece:k * D_MODEL + (p + 1) * piece],
                    preferred_element_type=F32)
        if k == 1:
            zs_ref[:, cols] = _silu(v)
        elif k == 4:
            gs_ref[:, cols] = _sigmoid(v + b_gate_ref[:, cols])
        elif k == 2:
            ext_ref[hist_rows:hist_rows + m, cols] = v
        elif k == 3:
            zp_ref[:, cols] = _silu(v)
        else:
            gp_ref[:, cols] = _sigmoid(
                v + b_gate_ref[:, D_MODEL + p * piece:D_MODEL + (p + 1) * piece])

    pieces = [(k, p) for k in (1, 4, 2, 3, 5) for p in range(D_MODEL // piece)]

    for blk in range(N_BLOCKS):
        lanes = slice(blk * LANES, (blk + 1) * LANES)
        st = slice(blk * BLOCK_STATE, (blk + 1) * BLOCK_STATE)
        for k, p in pieces[blk * len(pieces) // N_BLOCKS:(blk + 1) * len(pieces) // N_BLOCKS]:
            proj_piece(k, p)
        a_re = jnp.broadcast_to(a2_re_ref[:, st], (SUBLANES, BLOCK_STATE))
        a_im = jnp.broadcast_to(a2_im_ref[:, st], (SUBLANES, BLOCK_STATE))
        for j in range(n_b // SUBLANES):
            tile = slice(j * SUBLANES, (j + 1) * SUBLANES)
            s_re = sre_ref[tile, st]
            s_im = sim_ref[tile, st]
            for k in range(n_pairs):
                rows = slice((k + 1) * n_b + j * SUBLANES, (k + 1) * n_b + (j + 1) * SUBLANES)
                n_re = a_re * s_re - a_im * s_im + bu_ref[blk, rows, 0:BLOCK_STATE]
                n_im = a_re * s_im + a_im * s_re + bu_ref[blk, rows, BLOCK_STATE:]
                bu_ref[blk, rows, 0:BLOCK_STATE] = n_re
                bu_ref[blk, rows, BLOCK_STATE:] = n_im
                s_re, s_im = n_re, n_im
            sre_ref[tile, st] = s_re
            sim_ref[tile, st] = s_im
        y2 = (jnp.dot(bu_ref[blk, 0:half, :].astype(BF16), wc_ref[blk], preferred_element_type=F32)
              + jnp.dot(u2_ref[blk], wd_ref[blk], preferred_element_type=F32))
        ys_ref[:, 0:n_b, lanes] = y2[:, 0:LANES].reshape(n_pairs, n_b, LANES)
        ys_ref[:, n_b:, lanes] = y2[:, LANES:].reshape(n_pairs, n_b, LANES)

    if pos0 + 1 >= max(POOL_WINDOWS):
        pos = None
    else:
        t_loc = lax.broadcasted_iota(jnp.int32, (m, 1), 0) // n_b
        pos = pos0 + step * n_t + t_loc
    for gi, w in enumerate(POOL_WINDOWS):
        cols = slice(gi * POOL_GROUP_CH, (gi + 1) * POOL_GROUP_CH)
        s = ext_ref[(POOL_BUF + 1 - w) * n_b:hist_rows + m, cols]
        span = 1
        while span < w:
            keep = s.shape[0] - span * n_b
            s = s[:keep] + s[span * n_b:]
            span *= 2
        if pos is None:
            mean = s * (1.0 / w)
        else:
            mean = s * (1.0 / jnp.minimum(pos + 1, w).astype(F32))
        pooled = mean - ext_ref[hist_rows:hist_rows + m, cols]
        yp_ref[:, cols] = jnp.dot(pooled.astype(BF16), pmix_ref[gi],
                                  preferred_element_type=F32) * pscale_ref[:, cols]
    for k in range(POOL_BUF):
        ext_ref[k * n_b:(k + 1) * n_b, :] = ext_ref[(k + n_t) * n_b:(k + n_t + 1) * n_b, :]

    y = ys_ref[...].reshape(m, D_MODEL) + d_ref[...] * us_ref[...].reshape(m, D_MODEL)
    y = jax.nn.gelu(y)
    b = yp_ref[...] * zp_ref[...]
    acc_p = jnp.dot(b.astype(BF16), w_bp_ref[...], preferred_element_type=F32)
    glu = jnp.dot(y.astype(BF16), w_glu_ref[...], preferred_element_type=F32)
    if carry:
        prepare_norm(x_next_ref)
    a = y * _sigmoid(glu + b_glu_ref[...]) * zs_ref[...]
    acc_s = jnp.dot(a.astype(BF16), w_bs_ref[...], preferred_element_type=F32)
    if carry:
        prepare_u()
    merged = gs_ref[...] * acc_s + gp_ref[...] * acc_p
    delta = jnp.dot(merged.astype(BF16), w_out_ref[...], preferred_element_type=F32)
    if carry:
        prepare_slabs()
    y_rows = _rmsnorm(rows_tb(x_ref) + delta, fgain_ref[...])
    y_ref[...] = jnp.swapaxes(y_rows.reshape(n_t, n_b, D_MODEL), 0, 1)

    def write_state():
        ore_ref[...] = sre_ref[...]
        oim_ref[...] = sim_ref[...]
        obuf_ref[...] = ext_ref[0:hist_rows, :]

    if carry:
        pl.when(step == pl.num_programs(0) - 1)(write_state)
    else:
        write_state()


def _layer_call(x, state, weights, *, n_t, n_b, carry, pos0, name):
    batch, time, _ = x.shape
    m = n_t * n_b
    hist_rows = POOL_BUF * n_b
    x_block = (n_b, n_t, D_MODEL)
    if carry:
        assert batch == n_b and time % n_t == 0
        n_steps = time // n_t
        x_specs = [pl.BlockSpec(x_block, lambda i: (0, i, 0)),
                   pl.BlockSpec(x_block, lambda i: (0, jnp.minimum(i + 1, n_steps - 1), 0))]
        y_map = lambda i: (0, i, 0)
        per_step = lambda i: (0, 0)
        xs = (x, x)
    else:
        assert time == n_t and batch % n_b == 0
        n_steps = batch // n_b
        x_specs = [pl.BlockSpec(x_block, lambda i: (i, 0, 0))]
        y_map = lambda i: (i, 0, 0)
        per_step = lambda i: (i, 0)
        xs = (x,)
    whole = pl.BlockSpec(memory_space=pltpu.VMEM)
    state_specs = [
        pl.BlockSpec((n_b, STATE_W), per_step),
        pl.BlockSpec((n_b, STATE_W), per_step),
        pl.BlockSpec((hist_rows, D_MODEL), per_step),
    ]
    kernel = functools.partial(_layer_kernel, n_t=n_t, n_b=n_b, carry=carry, pos0=pos0,
                               zero_state=state is None)
    return pl.pallas_call(
        kernel,
        grid=(n_steps,),
        in_specs=x_specs + ([] if state is None else state_specs) + [whole] * len(weights),
        out_specs=[
            pl.BlockSpec(x_block, y_map),
            pl.BlockSpec((n_b, STATE_W), per_step),
            pl.BlockSpec((n_b, STATE_W), per_step),
            pl.BlockSpec((hist_rows, D_MODEL), per_step),
        ],
        out_shape=[
            jax.ShapeDtypeStruct((batch, time, D_MODEL), F32),
            jax.ShapeDtypeStruct((batch, STATE_W), F32),
            jax.ShapeDtypeStruct((batch, STATE_W), F32),
            jax.ShapeDtypeStruct((batch // n_b * hist_rows, D_MODEL), F32),
        ],
        scratch_shapes=[
            pltpu.VMEM((n_b, STATE_W), F32),
            pltpu.VMEM((n_b, STATE_W), F32),
            pltpu.VMEM(((POOL_BUF + n_t) * n_b, D_MODEL), F32),
            pltpu.VMEM((N_BLOCKS, m // 2 + n_b, 2 * BLOCK_STATE), F32),
            pltpu.VMEM((N_BLOCKS, m // 2, 2 * LANES), BF16),
            pltpu.VMEM((m, D_MODEL), BF16),
            pltpu.VMEM((n_t // 2, 2 * n_b, D_MODEL), F32),
            pltpu.VMEM((m, D_MODEL), F32),
            pltpu.VMEM((m, D_MODEL), F32),
            pltpu.VMEM((m, D_MODEL), F32),
            pltpu.VMEM((m, D_MODEL), F32),
            pltpu.VMEM((n_t // 2, 2 * n_b, D_MODEL), F32),
            pltpu.VMEM((m, D_MODEL), F32),
        ],
        compiler_params=pltpu.CompilerParams(
            dimension_semantics=("arbitrary",), vmem_limit_bytes=VMEM_LIMIT_BYTES),
        name=name,
    )(*xs, *(() if state is None else state), *weights)


def _block_diag(grid, rows, cols):
    gpb = GROUPS_PER_BLOCK
    width = len(grid[0]) * gpb * cols
    lane = np.arange(cols)
    row_blocks = []
    for row_parts in grid:
        acc = None
        for ci, part in enumerate(row_parts):
            if part is None:
                continue
            sel = np.zeros((cols, width), np.float32)
            for j in range(gpb):
                sel[lane, (ci * gpb + j) * cols + lane] = 1.0
            term = jnp.einsum('brk,kn->brn', part.reshape(N_BLOCKS, gpb * rows, cols), sel,
                              precision=lax.Precision.HIGHEST)
            acc = term if acc is None else acc + term
        row_blocks.append(acc)
    tiled = jnp.concatenate(row_blocks, axis=1)
    r = (lax.broadcasted_iota(jnp.int32, tiled.shape[1:], 0) // rows) % gpb
    c = (lax.broadcasted_iota(jnp.int32, tiled.shape[1:], 1) // cols) % gpb
    return jnp.where(r == c, tiled, 0.0)


def _ssm_weights(a_re, a_im, log_dt, b_re, b_im, c_re, c_im):
    n_c, n_p = SSM_GROUP_CH, SSM_STATE
    dt = jnp.exp(log_dt)[:, None]
    mag = jnp.exp(dt * a_re)
    ang = dt * a_im
    abar_re = mag * jnp.cos(ang)
    abar_im = mag * jnp.sin(ang)
    den = a_re * a_re + a_im * a_im
    nr = abar_re - 1.0
    ni = abar_im
    q_re = ((nr * a_re + ni * a_im) / den)[:, None, :]
    q_im = ((ni * a_re - nr * a_im) / den)[:, None, :]
    a2_re = abar_re * abar_re - abar_im * abar_im
    a2_im = 2.0 * abar_re * abar_im

    bt_re = jnp.swapaxes(b_re, 1, 2)
    bt_im = jnp.swapaxes(b_im, 1, 2)
    bbar_re = q_re * bt_re - q_im * bt_im
    bbar_im = q_re * bt_im + q_im * bt_re
    ar, ai = abar_re[:, None, :], abar_im[:, None, :]
    ab_re = ar * bbar_re - ai * bbar_im
    ab_im = ar * bbar_im + ai * bbar_re

    ct_re = jnp.swapaxes(c_re, 1, 2)
    ct_im = jnp.swapaxes(c_im, 1, 2)
    ar, ai = abar_re[:, :, None], abar_im[:, :, None]
    ca_re = ct_re * ar - ct_im * ai
    ca_im = ct_re * ai + ct_im * ar
    ar2, ai2 = a2_re[:, :, None], a2_im[:, :, None]
    ca2_re = ct_re * ar2 - ct_im * ai2
    ca2_im = ct_re * ai2 + ct_im * ar2

    lhs = jnp.concatenate([bbar_re, bbar_im], axis=-1)
    rhs = jnp.concatenate([jnp.concatenate([ct_re, ca_re], axis=-1),
                           jnp.concatenate([-ct_im, -ca_im], axis=-1)], axis=1)
    direct = jnp.einsum('gip,gpo->gio', lhs, rhs, precision=lax.Precision.HIGHEST)
    cb, cab = direct[..., :n_c], direct[..., n_c:]

    wb = _block_diag([[ab_re, ab_im], [bbar_re, bbar_im]], n_c, n_p)
    wc = _block_diag([[ca_re, ca2_re], [-ca_im, -ca2_im]], n_p, n_c)
    wd = _block_diag([[cb, cab], [None, cb]], n_c, n_c)
    return a2_re, a2_im, wb.astype(BF16), wc.astype(BF16), wd.astype(BF16)


def _rows_kb(buf, n_b):
    batch = buf.shape[0]
    buf = jnp.transpose(buf.reshape(batch // n_b, n_b, POOL_BUF, D_MODEL), (0, 2, 1, 3))
    return buf.reshape(batch * POOL_BUF, D_MODEL)


def _rows_bk(rows, n_b):
    batch = rows.shape[0] // POOL_BUF
    rows = jnp.transpose(rows.reshape(batch // n_b, POOL_BUF, n_b, D_MODEL), (0, 2, 1, 3))
    return rows.reshape(batch, POOL_BUF, D_MODEL)


PROMPT_T = 32
SAMPLE_B = 32


def kernel(x_prompt, x_sample, state_ssm_re, state_ssm_im, state_pool, meta_tokens, norm_gain, w_in, b_gate, ssm_a_re, ssm_a_im, ssm_log_dt, ssm_b_re, ssm_b_im, ssm_c_re, ssm_c_im, ssm_d, w_glu, b_glu, pool_mix, pool_scale, w_branch_ssm, w_branch_pool, w_out, final_norm_gain):
    batch, seq, _ = x_prompt.shape
    dec_batch, dec_seq, _ = x_sample.shape
    depth = norm_gain.shape[0]
    assert depth == 1
    l = 0
    a2_re, a2_im, wb, wc, wd = _ssm_weights(
        ssm_a_re[l], ssm_a_im[l], ssm_log_dt[l], ssm_b_re[l], ssm_b_im[l],
        ssm_c_re[l], ssm_c_im[l])
    row = lambda v: v.reshape(1, -1).astype(F32)
    weights = (
        row(norm_gain[l]), w_in[l].astype(BF16), row(b_gate[l]),
        row(a2_re), row(a2_im), wb, wc, wd, row(ssm_d[l]),
        w_glu[l].astype(BF16), row(b_glu[l]), pool_mix[l].astype(BF16), row(pool_scale[l]),
        w_branch_ssm[l].astype(BF16), w_branch_pool[l].astype(BF16), w_out[l].astype(BF16),
        row(final_norm_gain),
    )

    assert batch == SUBLANES
    meta = jnp.broadcast_to(meta_tokens[None].astype(x_prompt.dtype), (batch, N_META, D_MODEL))
    _, *meta_state = _layer_call(
        meta, None, weights,
        n_t=N_META, n_b=batch, carry=False, pos0=0, name="meta_layer")
    y_prompt, p_re, p_im, p_buf = _layer_call(
        x_prompt, meta_state, weights,
        n_t=PROMPT_T, n_b=batch, carry=True, pos0=N_META, name="prompt_layer")
    new_pool_prompt = _rows_bk(p_buf, batch)[None]

    sample_state = (state_ssm_re[l].reshape(dec_batch, STATE_W),
                    state_ssm_im[l].reshape(dec_batch, STATE_W),
                    _rows_kb(state_pool[l], SAMPLE_B))
    y_sample, s_re, s_im, s_buf = _layer_call(
        x_sample, sample_state, weights,
        n_t=dec_seq, n_b=SAMPLE_B, carry=False, pos0=PAST_LEN, name="sample_layer")
    new_pool_sample = _rows_bk(s_buf, SAMPLE_B)[None]

    state_shape = (1, -1, SSM_GROUPS, SSM_STATE)
    return (y_prompt, y_sample,
            p_re.reshape(state_shape), p_im.reshape(state_shape), new_pool_prompt,
            s_re.reshape(state_shape), s_im.reshape(state_shape), new_pool_sample)
```

```python
import functools

import jax
import jax.numpy as jnp
from jax import lax
from jax.experimental import pallas as pl
from jax.experimental.pallas import tpu as pltpu

D_MODEL = 1024
N_META = 16
SSM_GROUP_CH = 16
SSM_GROUPS = D_MODEL // SSM_GROUP_CH
SSM_STATE = 64
POOL_WINDOWS = (2, 4, 8, 16)
POOL_GROUP_CH = D_MODEL // len(POOL_WINDOWS)
POOL_BUF = max(POOL_WINDOWS) - 1
EPS = 1e-6
PAST_LEN = 16384

SUBLANES = 8
LANES = 128
GROUPS_PER_BLOCK = LANES // SSM_GROUP_CH
N_BLOCKS = SSM_GROUPS // GROUPS_PER_BLOCK
BLOCK_STATE = GROUPS_PER_BLOCK * SSM_STATE
STATE_W = SSM_GROUPS * SSM_STATE
VMEM_LIMIT_BYTES = 56 * 1024 * 1024

F32 = jnp.float32
BF16 = jnp.bfloat16


def _rmsnorm(x, gain):
    ms = jnp.mean(x * x, axis=-1, keepdims=True)
    return (x * lax.rsqrt(ms + EPS)) * gain


_sigmoid = jax.nn.sigmoid


def _silu(x):
    return x * _sigmoid(x)


def _layer_kernel(*refs, n_t, n_b, carry, pos0, zero_state):
    if carry:
        x_ref, x_next_ref, *refs = refs
    else:
        x_ref, *refs = refs
    if not zero_state:
        s0re_ref, s0im_ref, buf0_ref, *refs = refs
    (gain_ref, w_in_ref, b_gate_ref, a2_re_ref, a2_im_ref, wb_ref, wc_ref, wd_ref, d_ref,
     w_glu_ref, b_glu_ref, pmix_ref, pscale_ref, w_bs_ref, w_bp_ref, w_out_ref, fgain_ref,
     y_ref, ore_ref, oim_ref, obuf_ref,
     sre_ref, sim_ref, ext_ref, bu_ref, u2_ref, xn_ref, us_ref, zs_ref, gs_ref, zp_ref,
     gp_ref, ys_ref, yp_ref) = refs
    m = n_t * n_b
    n_pairs = n_t // 2
    half = n_pairs * n_b
    step = pl.program_id(0)
    hist_rows = POOL_BUF * n_b

    def load_state():
        if zero_state:
            sre_ref[...] = jnp.zeros(sre_ref.shape, F32)
            sim_ref[...] = jnp.zeros(sim_ref.shape, F32)
            ext_ref[0:hist_rows, :] = jnp.zeros((hist_rows, D_MODEL), F32)
        else:
            sre_ref[...] = s0re_ref[...]
            sim_ref[...] = s0im_ref[...]
            buf = jnp.concatenate([buf0_ref[...], jnp.zeros((n_b, 1, D_MODEL), F32)], axis=1)
            buf = jnp.swapaxes(buf, 0, 1).reshape((POOL_BUF + 1) * n_b, D_MODEL)
            ext_ref[0:hist_rows, :] = buf[0:hist_rows]

    def rows_tb(src_ref):
        return jnp.swapaxes(src_ref[...], 0, 1).reshape(m, D_MODEL)

    def proj(k):
        return jnp.dot(xn_ref[...], w_in_ref[:, k * D_MODEL:(k + 1) * D_MODEL],
                       preferred_element_type=F32)

    def prepare_norm(src_ref):
        xn_ref[...] = _rmsnorm(rows_tb(src_ref), gain_ref[...]).astype(BF16)

    def prepare_u():
        us_ref[...] = proj(0).reshape(n_pairs, 2 * n_b, D_MODEL)

    def prepare_slabs():
        for blk in range(N_BLOCKS):
            lanes = slice(blk * LANES, (blk + 1) * LANES)
            st = slice(blk * BLOCK_STATE, (blk + 1) * BLOCK_STATE)
            u_t0 = us_ref[:, 0:n_b, lanes].reshape(half, LANES)
            u_t1 = us_ref[:, n_b:, lanes].reshape(half, LANES)
            u2_ref[blk] = jnp.concatenate([u_t0, u_t1], axis=1).astype(BF16)
            bu_ref[blk, 0:n_b, 0:BLOCK_STATE] = sre_ref[:, st]
            bu_ref[blk, 0:n_b, BLOCK_STATE:] = sim_ref[:, st]
            bu_ref[blk, n_b:, :] = jnp.dot(u2_ref[blk], wb_ref[blk], preferred_element_type=F32)

    def prepare(src_ref):
        prepare_norm(src_ref)
        prepare_u()
        prepare_slabs()

    if carry:
        @pl.when(step == 0)
        def _():
            load_state()
            prepare(x_ref)
    else:
        load_state()
        prepare(x_ref)

    piece = 2 * LANES

    def proj_piece(k, p):
        cols = slice(p * piece, (p + 1) * piece)
        v = jnp.dot(xn_ref[...], w_in_ref[:, k * D_MODEL + p * piece:k * D_MODEL + (p + 1) * piece],
                    preferred_element_type=F32)
        if k == 1:
            zs_ref[:, cols] = _silu(v)
        elif k == 4:
            gs_ref[:, cols] = _sigmoid(v + b_gate_ref[:, cols])
        elif k == 2:
            ext_ref[hist_rows:hist_rows + m, cols] = v
        elif k == 3:
            zp_ref[:, cols] = _silu(v)
        else:
            gp_ref[:, cols] = _sigmoid(
                v + b_gate_ref[:, D_MODEL + p * piece:D_MODEL + (p + 1) * piece])

    pieces = [(k, p) for k in (1, 4, 2, 3, 5) for p in range(D_MODEL // piece)]

    for blk in range(N_BLOCKS):
        lanes = slice(blk * LANES, (blk + 1) * LANES)
        st = slice(blk * BLOCK_STATE, (blk + 1) * BLOCK_STATE)
        for k, p in pieces[blk * len(pieces) // N_BLOCKS:(blk + 1) * len(pieces) // N_BLOCKS]:
            proj_piece(k, p)
        a_re = jnp.broadcast_to(a2_re_ref[:, st], (SUBLANES, BLOCK_STATE))
        a_im = jnp.broadcast_to(a2_im_ref[:, st], (SUBLANES, BLOCK_STATE))
        for j in range(n_b // SUBLANES):
            tile = slice(j * SUBLANES, (j + 1) * SUBLANES)
            s_re = sre_ref[tile, st]
            s_im = sim_ref[tile, st]
            for k in range(n_pairs):
                rows = slice((k + 1) * n_b + j * SUBLANES, (k + 1) * n_b + (j + 1) * SUBLANES)
                n_re = a_re * s_re - a_im * s_im + bu_ref[blk, rows, 0:BLOCK_STATE]
                n_im = a_re * s_im + a_im * s_re + bu_ref[blk, rows, BLOCK_STATE:]
                bu_ref[blk, rows, 0:BLOCK_STATE] = n_re
                bu_ref[blk, rows, BLOCK_STATE:] = n_im
                s_re, s_im = n_re, n_im
            sre_ref[tile, st] = s_re
            sim_ref[tile, st] = s_im
        y2 = (jnp.dot(bu_ref[blk, 0:half, :].astype(BF16), wc_ref[blk], preferred_element_type=F32)
              + jnp.dot(u2_ref[blk], wd_ref[blk], preferred_element_type=F32))
        ys_ref[:, 0:n_b, lanes] = y2[:, 0:LANES].reshape(n_pairs, n_b, LANES)
        ys_ref[:, n_b:, lanes] = y2[:, LANES:].reshape(n_pairs, n_b, LANES)

    if pos0 + 1 >= max(POOL_WINDOWS):
        pos = None
    else:
        t_loc = lax.broadcasted_iota(jnp.int32, (m, 1), 0) // n_b
        pos = pos0 + step * n_t + t_loc
    for gi, w in enumerate(POOL_WINDOWS):
        cols = slice(gi * POOL_GROUP_CH, (gi + 1) * POOL_GROUP_CH)
        s = ext_ref[(POOL_BUF + 1 - w) * n_b:hist_rows + m, cols]
        span = 1
        while span < w:
            keep = s.shape[0] - span * n_b
            s = s[:keep] + s[span * n_b:]
            span *= 2
        if pos is None:
            mean = s * (1.0 / w)
        else:
            mean = s * (1.0 / jnp.minimum(pos + 1, w).astype(F32))
        pooled = mean - ext_ref[hist_rows:hist_rows + m, cols]
        yp_ref[:, cols] = jnp.dot(pooled.astype(BF16), pmix_ref[gi],
                                  preferred_element_type=F32) * pscale_ref[:, cols]
    for k in range(POOL_BUF):
        ext_ref[k * n_b:(k + 1) * n_b, :] = ext_ref[(k + n_t) * n_b:(k + n_t + 1) * n_b, :]

    y = ys_ref[...].reshape(m, D_MODEL) + d_ref[...] * us_ref[...].reshape(m, D_MODEL)
    y = jax.nn.gelu(y)
    b = yp_ref[...] * zp_ref[...]
    acc_p = jnp.dot(b.astype(BF16), w_bp_ref[...], preferred_element_type=F32)
    glu = jnp.dot(y.astype(BF16), w_glu_ref[...], preferred_element_type=F32)
    if carry:
        prepare_norm(x_next_ref)
    a = y * _sigmoid(glu + b_glu_ref[...]) * zs_ref[...]
    acc_s = jnp.dot(a.astype(BF16), w_bs_ref[...], preferred_element_type=F32)
    if carry:
        prepare_u()
    merged = gs_ref[...] * acc_s + gp_ref[...] * acc_p
    delta = jnp.dot(merged.astype(BF16), w_out_ref[...], preferred_element_type=F32)
    if carry:
        prepare_slabs()
    y_rows = _rmsnorm(rows_tb(x_ref) + delta, fgain_ref[...])
    y_ref[...] = jnp.swapaxes(y_rows.reshape(n_t, n_b, D_MODEL), 0, 1)

    def write_state():
        ore_ref[...] = sre_ref[...]
        oim_ref[...] = sim_ref[...]
        buf = jnp.concatenate([ext_ref[0:hist_rows, :], jnp.zeros((n_b, D_MODEL), F32)], axis=0)
        buf = jnp.swapaxes(buf.reshape(POOL_BUF + 1, n_b, D_MODEL), 0, 1)
        obuf_ref[...] = buf[:, 0:POOL_BUF, :]

    if carry:
        pl.when(step == pl.num_programs(0) - 1)(write_state)
    else:
        write_state()


def _layer_call(x, state, weights, *, n_t, n_b, carry, pos0, name):
    batch, time, _ = x.shape
    m = n_t * n_b
    x_block = (n_b, n_t, D_MODEL)
    buf_block = (n_b, POOL_BUF, D_MODEL)
    if carry:
        assert batch == n_b and time % n_t == 0
        n_steps = time // n_t
        x_specs = [pl.BlockSpec(x_block, lambda i: (0, i, 0)),
                   pl.BlockSpec(x_block, lambda i: (0, jnp.minimum(i + 1, n_steps - 1), 0))]
        y_map = lambda i: (0, i, 0)
        per_step = lambda i: (0, 0)
        per_step3 = lambda i: (0, 0, 0)
        xs = (x, x)
    else:
        assert time == n_t and batch % n_b == 0
        n_steps = batch // n_b
        x_specs = [pl.BlockSpec(x_block, lambda i: (i, 0, 0))]
        y_map = lambda i: (i, 0, 0)
        per_step = lambda i: (i, 0)
        per_step3 = lambda i: (i, 0, 0)
        xs = (x,)
    whole = pl.BlockSpec(memory_space=pltpu.VMEM)
    state_specs = [
        pl.BlockSpec((n_b, STATE_W), per_step),
        pl.BlockSpec((n_b, STATE_W), per_step),
        pl.BlockSpec(buf_block, per_step3),
    ]
    kernel = functools.partial(_layer_kernel, n_t=n_t, n_b=n_b, carry=carry, pos0=pos0,
                               zero_state=state is None)
    return pl.pallas_call(
        kernel,
        grid=(n_steps,),
        in_specs=x_specs + ([] if state is None else state_specs) + [whole] * len(weights),
        out_specs=[pl.BlockSpec(x_block, y_map)] + state_specs,
        out_shape=[
            jax.ShapeDtypeStruct((batch, time, D_MODEL), F32),
            jax.ShapeDtypeStruct((batch, STATE_W), F32),
            jax.ShapeDtypeStruct((batch, STATE_W), F32),
            jax.ShapeDtypeStruct((batch, POOL_BUF, D_MODEL), F32),
        ],
        scratch_shapes=[
            pltpu.VMEM((n_b, STATE_W), F32),
            pltpu.VMEM((n_b, STATE_W), F32),
            pltpu.VMEM(((POOL_BUF + n_t) * n_b, D_MODEL), F32),
            pltpu.VMEM((N_BLOCKS, m // 2 + n_b, 2 * BLOCK_STATE), F32),
            pltpu.VMEM((N_BLOCKS, m // 2, 2 * LANES), BF16),
            pltpu.VMEM((m, D_MODEL), BF16),
            pltpu.VMEM((n_t // 2, 2 * n_b, D_MODEL), F32),
            pltpu.VMEM((m, D_MODEL), F32),
            pltpu.VMEM((m, D_MODEL), F32),
            pltpu.VMEM((m, D_MODEL), F32),
            pltpu.VMEM((m, D_MODEL), F32),
            pltpu.VMEM((n_t // 2, 2 * n_b, D_MODEL), F32),
            pltpu.VMEM((m, D_MODEL), F32),
        ],
        compiler_params=pltpu.CompilerParams(
            dimension_semantics=("arbitrary",), vmem_limit_bytes=VMEM_LIMIT_BYTES),
        name=name,
    )(*xs, *(() if state is None else state), *weights)


def _block_diag_kernel(*refs, layouts):
    gpb = GROUPS_PER_BLOCK
    in_refs, out_refs = refs[:len(layouts)], refs[len(layouts):]
    for src_ref, dst_ref, (grid, rows, cols) in zip(in_refs, out_refs, layouts):
        height, width = gpb * rows, gpb * cols
        assert rows & (rows - 1) == 0 and cols & (cols - 1) == 0
        k = lax.broadcasted_iota(jnp.int32, (cols, width), 0)
        n = lax.broadcasted_iota(jnp.int32, (cols, width), 1)
        sel = jnp.where(jnp.bitwise_and(n, cols - 1) == k, 1.0, 0.0).astype(BF16)
        r = jnp.right_shift(lax.broadcasted_iota(jnp.int32, (height, width), 0),
                            rows.bit_length() - 1)
        c = jnp.right_shift(lax.broadcasted_iota(jnp.int32, (height, width), 1),
                            cols.bit_length() - 1)
        for ri, row_parts in enumerate(grid):
            for ci, part in enumerate(row_parts):
                if part is None:
                    cell = jnp.zeros((height, width), F32)
                else:
                    tiled = jnp.dot(src_ref[part, 0].astype(BF16), sel, preferred_element_type=F32)
                    cell = jnp.where(r == c, tiled, 0.0)
                dst_ref[0, ri * height:(ri + 1) * height, ci * width:(ci + 1) * width] = (
                    cell.astype(BF16))


def _block_diag_call(stacks, layouts):
    gpb = GROUPS_PER_BLOCK
    ins, in_specs, out_specs, out_shapes = [], [], [], []
    for stack, (grid, rows, cols) in zip(stacks, layouts):
        n_parts = stack.shape[0]
        ins.append(stack.reshape(n_parts, N_BLOCKS, gpb * rows, cols))
        in_specs.append(pl.BlockSpec((n_parts, 1, gpb * rows, cols), lambda i: (0, i, 0, 0)))
        shape = (len(grid) * gpb * rows, len(grid[0]) * gpb * cols)
        out_specs.append(pl.BlockSpec((1,) + shape, lambda i: (i, 0, 0)))
        out_shapes.append(jax.ShapeDtypeStruct((N_BLOCKS,) + shape, BF16))
    return pl.pallas_call(
        functools.partial(_block_diag_kernel, layouts=layouts),
        grid=(N_BLOCKS,), in_specs=in_specs, out_specs=out_specs, out_shape=out_shapes,
        name="s5_block_diag_weights",
    )(*ins)


def _ssm_weights(a_re, a_im, log_dt, b_re, b_im, c_re, c_im):
    n_c, n_p = SSM_GROUP_CH, SSM_STATE
    dt = jnp.exp(log_dt)[:, None]
    mag = jnp.exp(dt * a_re)
    ang = dt * a_im
    abar_re = mag * jnp.cos(ang)
    abar_im = mag * jnp.sin(ang)
    den = a_re * a_re + a_im * a_im
    nr = abar_re - 1.0
    ni = abar_im
    q_re = ((nr * a_re + ni * a_im) / den)[:, None, :]
    q_im = ((ni * a_re - nr * a_im) / den)[:, None, :]
    a2_re = abar_re * abar_re - abar_im * abar_im
    a2_im = 2.0 * abar_re * abar_im

    bt_re = jnp.swapaxes(b_re, 1, 2)
    bt_im = jnp.swapaxes(b_im, 1, 2)
    bbar_re = q_re * bt_re - q_im * bt_im
    bbar_im = q_re * bt_im + q_im * bt_re
    ar, ai = abar_re[:, None, :], abar_im[:, None, :]
    ab_re = ar * bbar_re - ai * bbar_im
    ab_im = ar * bbar_im + ai * bbar_re

    ct_re = jnp.swapaxes(c_re, 1, 2)
    ct_im = jnp.swapaxes(c_im, 1, 2)
    ar, ai = abar_re[:, :, None], abar_im[:, :, None]
    ca_re = ct_re * ar - ct_im * ai
    ca_im = ct_re * ai + ct_im * ar
    ar2, ai2 = a2_re[:, :, None], a2_im[:, :, None]
    ca2_re = ct_re * ar2 - ct_im * ai2
    ca2_im = ct_re * ai2 + ct_im * ar2

    lhs = jnp.concatenate([bbar_re, bbar_im], axis=-1)
    rhs = jnp.concatenate([jnp.concatenate([ct_re, ca_re], axis=-1),
                           jnp.concatenate([-ct_im, -ca_im], axis=-1)], axis=1)
    direct = jnp.einsum('gip,gpo->gio', lhs, rhs, precision=lax.Precision.HIGHEST)
    cb, cab = direct[..., :n_c], direct[..., n_c:]

    wb, wc, wd = _block_diag_call(
        (jnp.stack([ab_re, ab_im, bbar_re, bbar_im]),
         jnp.stack([ca_re, ca2_re, -ca_im, -ca2_im]),
         jnp.stack([cb, cab])),
        (([[0, 1], [2, 3]], n_c, n_p),
         ([[0, 1], [2, 3]], n_p, n_c),
         ([[0, 1], [None, 0]], n_c, n_c)))
    return a2_re, a2_im, wb, wc, wd


PROMPT_T = 32
SAMPLE_B = 32


def kernel(x_prompt, x_sample, state_ssm_re, state_ssm_im, state_pool, meta_tokens, norm_gain, w_in, b_gate, ssm_a_re, ssm_a_im, ssm_log_dt, ssm_b_re, ssm_b_im, ssm_c_re, ssm_c_im, ssm_d, w_glu, b_glu, pool_mix, pool_scale, w_branch_ssm, w_branch_pool, w_out, final_norm_gain):
    batch, seq, _ = x_prompt.shape
    dec_batch, dec_seq, _ = x_sample.shape
    depth = norm_gain.shape[0]
    assert depth == 1
    l = 0
    a2_re, a2_im, wb, wc, wd = _ssm_weights(
        ssm_a_re[l], ssm_a_im[l], ssm_log_dt[l], ssm_b_re[l], ssm_b_im[l],
        ssm_c_re[l], ssm_c_im[l])
    row = lambda v: v.reshape(1, -1).astype(F32)
    weights = (
        row(norm_gain[l]), w_in[l].astype(BF16), row(b_gate[l]),
        row(a2_re), row(a2_im), wb, wc, wd, row(ssm_d[l]),
        w_glu[l].astype(BF16), row(b_glu[l]), pool_mix[l].astype(BF16), row(pool_scale[l]),
        w_branch_ssm[l].astype(BF16), w_branch_pool[l].astype(BF16), w_out[l].astype(BF16),
        row(final_norm_gain),
    )

    assert batch == SUBLANES
    meta = jnp.broadcast_to(meta_tokens[None].astype(x_prompt.dtype), (batch, N_META, D_MODEL))
    _, *meta_state = _layer_call(
        meta, None, weights,
        n_t=N_META, n_b=batch, carry=False, pos0=0, name="meta_layer")
    y_prompt, p_re, p_im, p_buf = _layer_call(
        x_prompt, meta_state, weights,
        n_t=PROMPT_T, n_b=batch, carry=True, pos0=N_META, name="prompt_layer")

    sample_state = (state_ssm_re[l].reshape(dec_batch, STATE_W),
                    state_ssm_im[l].reshape(dec_batch, STATE_W),
                    state_pool[l])
    y_sample, s_re, s_im, s_buf = _layer_call(
        x_sample, sample_state, weights,
        n_t=dec_seq, n_b=SAMPLE_B, carry=False, pos0=PAST_LEN, name="sample_layer")

    state_shape = (1, -1, SSM_GROUPS, SSM_STATE)
    return (y_prompt, y_sample,
            p_re.reshape(state_shape), p_im.reshape(state_shape), p_buf[None],
            s_re.reshape(state_shape), s_im.reshape(state_shape), s_buf[None])
```

```python
import functools

import jax
import jax.numpy as jnp
from jax import lax
from jax.experimental import pallas as pl
from jax.experimental.pallas import tpu as pltpu

D_MODEL = 1024
N_META = 16
SSM_GROUP_CH = 16
SSM_GROUPS = D_MODEL // SSM_GROUP_CH
SSM_STATE = 64
POOL_WINDOWS = (2, 4, 8, 16)
POOL_GROUP_CH = D_MODEL // len(POOL_WINDOWS)
POOL_BUF = max(POOL_WINDOWS) - 1
EPS = 1e-6
PAST_LEN = 16384

SUBLANES = 8
LANES = 128
GROUPS_PER_BLOCK = LANES // SSM_GROUP_CH
N_BLOCKS = SSM_GROUPS // GROUPS_PER_BLOCK
BLOCK_STATE = GROUPS_PER_BLOCK * SSM_STATE
STATE_W = SSM_GROUPS * SSM_STATE
VMEM_LIMIT_BYTES = 56 * 1024 * 1024

F32 = jnp.float32
BF16 = jnp.bfloat16


def _rmsnorm(x, gain):
    ms = jnp.mean(x * x, axis=-1, keepdims=True)
    return (x * lax.rsqrt(ms + EPS)) * gain


_sigmoid = jax.nn.sigmoid


def _silu(x):
    return x * _sigmoid(x)


def _layer_kernel(*refs, n_t, n_b, carry, pos0, zero_state):
    if carry:
        x_ref, x_next_ref, *refs = refs
    else:
        x_ref, *refs = refs
    if not zero_state:
        s0re_ref, s0im_ref, buf0_ref, *refs = refs
    (gain_ref, w_in_ref, b_gate_ref, a2_re_ref, a2_im_ref, wb_ref, wc_ref, wd_ref, d_ref,
     w_glu_ref, b_glu_ref, pmix_ref, pscale_ref, w_bs_ref, w_bp_ref, w_out_ref, fgain_ref,
     y_ref, ore_ref, oim_ref, obuf_ref,
     sre_ref, sim_ref, ext_ref, bu_ref, u2_ref, xn_ref, us_ref, zs_ref, gs_ref, zp_ref,
     gp_ref, ys_ref, yp_ref) = refs
    m = n_t * n_b
    n_pairs = n_t // 2
    half = n_pairs * n_b
    step = pl.program_id(0)
    hist_rows = POOL_BUF * n_b

    def load_state():
        if zero_state:
            sre_ref[...] = jnp.zeros(sre_ref.shape, F32)
            sim_ref[...] = jnp.zeros(sim_ref.shape, F32)
            ext_ref[0:hist_rows, :] = jnp.zeros((hist_rows, D_MODEL), F32)
        else:
            sre_ref[...] = s0re_ref[...]
            sim_ref[...] = s0im_ref[...]
            buf = jnp.concatenate([buf0_ref[...], jnp.zeros((n_b, 1, D_MODEL), F32)], axis=1)
            buf = jnp.swapaxes(buf, 0, 1).reshape((POOL_BUF + 1) * n_b, D_MODEL)
            ext_ref[0:hist_rows, :] = buf[0:hist_rows]

    def rows_tb(src_ref):
        return jnp.swapaxes(src_ref[...], 0, 1).reshape(m, D_MODEL)

    def proj(k):
        return jnp.dot(xn_ref[...], w_in_ref[:, k * D_MODEL:(k + 1) * D_MODEL],
                       preferred_element_type=F32)

    def prepare_norm(src_ref):
        xn_ref[...] = _rmsnorm(rows_tb(src_ref), gain_ref[...]).astype(BF16)

    def prepare_u():
        us_ref[...] = proj(0).reshape(n_pairs, 2 * n_b, D_MODEL)

    def prepare_slabs():
        for blk in range(N_BLOCKS):
            lanes = slice(blk * LANES, (blk + 1) * LANES)
            st = slice(blk * BLOCK_STATE, (blk + 1) * BLOCK_STATE)
            u_t0 = us_ref[:, 0:n_b, lanes].reshape(half, LANES)
            u_t1 = us_ref[:, n_b:, lanes].reshape(half, LANES)
            u2_ref[blk] = jnp.concatenate([u_t0, u_t1], axis=1).astype(BF16)
            bu_ref[blk, 0:n_b, 0:BLOCK_STATE] = sre_ref[:, st]
            bu_ref[blk, 0:n_b, BLOCK_STATE:] = sim_ref[:, st]
            bu_ref[blk, n_b:, :] = jnp.dot(u2_ref[blk], wb_ref[blk], preferred_element_type=F32)

    def prepare(src_ref):
        prepare_norm(src_ref)
        prepare_u()
        prepare_slabs()

    if carry:
        @pl.when(step == 0)
        def _():
            load_state()
            prepare(x_ref)
    else:
        load_state()
        prepare(x_ref)

    piece = 2 * LANES

    def proj_piece(k, p):
        cols = slice(p * piece, (p + 1) * piece)
        v = jnp.dot(xn_ref[...], w_in_ref[:, k * D_MODEL + p * piece:k * D_MODEL + (p + 1) * piece],
                    preferred_element_type=F32)
        if k == 1:
            zs_ref[:, cols] = _silu(v)
        elif k == 4:
            gs_ref[:, cols] = _sigmoid(v + b_gate_ref[:, cols])
        elif k == 2:
            ext_ref[hist_rows:hist_rows + m, cols] = v
        elif k == 3:
            zp_ref[:, cols] = _silu(v)
        else:
            gp_ref[:, cols] = _sigmoid(
                v + b_gate_ref[:, D_MODEL + p * piece:D_MODEL + (p + 1) * piece])

    pieces = [(k, p) for k in (1, 4, 2, 3, 5) for p in range(D_MODEL // piece)]

    for blk in range(N_BLOCKS):
        lanes = slice(blk * LANES, (blk + 1) * LANES)
        st = slice(blk * BLOCK_STATE, (blk + 1) * BLOCK_STATE)
        for k, p in pieces[blk * len(pieces) // N_BLOCKS:(blk + 1) * len(pieces) // N_BLOCKS]:
            proj_piece(k, p)
        a_re = jnp.broadcast_to(a2_re_ref[:, st], (SUBLANES, BLOCK_STATE))
        a_im = jnp.broadcast_to(a2_im_ref[:, st], (SUBLANES, BLOCK_STATE))
        for j in range(n_b // SUBLANES):
            tile = slice(j * SUBLANES, (j + 1) * SUBLANES)
            s_re = sre_ref[tile, st]
            s_im = sim_ref[tile, st]
            for k in range(n_pairs):
                rows = slice((k + 1) * n_b + j * SUBLANES, (k + 1) * n_b + (j + 1) * SUBLANES)
                n_re = a_re * s_re - a_im * s_im + bu_ref[blk, rows, 0:BLOCK_STATE]
                n_im = a_re * s_im + a_im * s_re + bu_ref[blk, rows, BLOCK_STATE:]
                bu_ref[blk, rows, 0:BLOCK_STATE] = n_re
                bu_ref[blk, rows, BLOCK_STATE:] = n_im
                s_re, s_im = n_re, n_im
            sre_ref[tile, st] = s_re
            sim_ref[tile, st] = s_im
        y2 = (jnp.dot(bu_ref[blk, 0:half, :].astype(BF16), wc_ref[blk], preferred_element_type=F32)
              + jnp.dot(u2_ref[blk], wd_ref[blk], preferred_element_type=F32))
        ys_ref[:, 0:n_b, lanes] = y2[:, 0:LANES].reshape(n_pairs, n_b, LANES)
        ys_ref[:, n_b:, lanes] = y2[:, LANES:].reshape(n_pairs, n_b, LANES)

    if pos0 + 1 >= max(POOL_WINDOWS):
        pos = None
    else:
        t_loc = lax.broadcasted_iota(jnp.int32, (m, 1), 0) // n_b
        pos = pos0 + step * n_t + t_loc
    for gi, w in enumerate(POOL_WINDOWS):
        cols = slice(gi * POOL_GROUP_CH, (gi + 1) * POOL_GROUP_CH)
        s = ext_ref[(POOL_BUF + 1 - w) * n_b:hist_rows + m, cols]
        span = 1
        while span < w:
            keep = s.shape[0] - span * n_b
            s = s[:keep] + s[span * n_b:]
            span *= 2
        if pos is None:
            mean = s * (1.0 / w)
        else:
            mean = s * (1.0 / jnp.minimum(pos + 1, w).astype(F32))
        pooled = mean - ext_ref[hist_rows:hist_rows + m, cols]
        yp_ref[:, cols] = jnp.dot(pooled.astype(BF16), pmix_ref[gi],
                                  preferred_element_type=F32) * pscale_ref[:, cols]
    for k in range(POOL_BUF):
        ext_ref[k * n_b:(k + 1) * n_b, :] = ext_ref[(k + n_t) * n_b:(k + n_t + 1) * n_b, :]

    y = ys_ref[...].reshape(m, D_MODEL) + d_ref[...] * us_ref[...].reshape(m, D_MODEL)
    y = jax.nn.gelu(y)
    b = yp_ref[...] * zp_ref[...]
    acc_p = jnp.dot(b.astype(BF16), w_bp_ref[...], preferred_element_type=F32)
    glu = jnp.dot(y.astype(BF16), w_glu_ref[...], preferred_element_type=F32)
    if carry:
        prepare_norm(x_next_ref)
    a = y * _sigmoid(glu + b_glu_ref[...]) * zs_ref[...]
    acc_s = jnp.dot(a.astype(BF16), w_bs_ref[...], preferred_element_type=F32)
    if carry:
        prepare_u()
    merged = gs_ref[...] * acc_s + gp_ref[...] * acc_p
    delta = jnp.dot(merged.astype(BF16), w_out_ref[...], preferred_element_type=F32)
    if carry:
        prepare_slabs()
    y_rows = _rmsnorm(rows_tb(x_ref) + delta, fgain_ref[...])
    y_ref[...] = jnp.swapaxes(y_rows.reshape(n_t, n_b, D_MODEL), 0, 1)

    def write_state():
        ore_ref[...] = sre_ref[...]
        oim_ref[...] = sim_ref[...]
        buf = jnp.concatenate([ext_ref[0:hist_rows, :], jnp.zeros((n_b, D_MODEL), F32)], axis=0)
        buf = jnp.swapaxes(buf.reshape(POOL_BUF + 1, n_b, D_MODEL), 0, 1)
        obuf_ref[...] = buf[:, 0:POOL_BUF, :]

    if carry:
        pl.when(step == pl.num_programs(0) - 1)(write_state)
    else:
        write_state()


def _layer_call(x, state, weights, *, n_t, n_b, carry, pos0, name):
    batch, time, _ = x.shape
    m = n_t * n_b
    x_block = (n_b, n_t, D_MODEL)
    buf_block = (n_b, POOL_BUF, D_MODEL)
    if carry:
        assert batch == n_b and time % n_t == 0
        n_steps = time // n_t
        x_specs = [pl.BlockSpec(x_block, lambda i: (0, i, 0)),
                   pl.BlockSpec(x_block, lambda i: (0, jnp.minimum(i + 1, n_steps - 1), 0))]
        y_map = lambda i: (0, i, 0)
        per_step = lambda i: (0, 0)
        per_step3 = lambda i: (0, 0, 0)
        xs = (x, x)
    else:
        assert time == n_t and batch % n_b == 0
        n_steps = batch // n_b
        x_specs = [pl.BlockSpec(x_block, lambda i: (i, 0, 0))]
        y_map = lambda i: (i, 0, 0)
        per_step = lambda i: (i, 0)
        per_step3 = lambda i: (i, 0, 0)
        xs = (x,)
    whole = pl.BlockSpec(memory_space=pltpu.VMEM)
    state_specs = [
        pl.BlockSpec((n_b, STATE_W), per_step),
        pl.BlockSpec((n_b, STATE_W), per_step),
        pl.BlockSpec(buf_block, per_step3),
    ]
    kernel = functools.partial(_layer_kernel, n_t=n_t, n_b=n_b, carry=carry, pos0=pos0,
                               zero_state=state is None)
    return pl.pallas_call(
        kernel,
        grid=(n_steps,),
        in_specs=x_specs + ([] if state is None else state_specs) + [whole] * len(weights),
        out_specs=[pl.BlockSpec(x_block, y_map)] + state_specs,
        out_shape=[
            jax.ShapeDtypeStruct((batch, time, D_MODEL), F32),
            jax.ShapeDtypeStruct((batch, STATE_W), F32),
            jax.ShapeDtypeStruct((batch, STATE_W), F32),
            jax.ShapeDtypeStruct((batch, POOL_BUF, D_MODEL), F32),
        ],
        scratch_shapes=[
            pltpu.VMEM((n_b, STATE_W), F32),
            pltpu.VMEM((n_b, STATE_W), F32),
            pltpu.VMEM(((POOL_BUF + n_t) * n_b, D_MODEL), F32),
            pltpu.VMEM((N_BLOCKS, m // 2 + n_b, 2 * BLOCK_STATE), F32),
            pltpu.VMEM((N_BLOCKS, m // 2, 2 * LANES), BF16),
            pltpu.VMEM((m, D_MODEL), BF16),
            pltpu.VMEM((n_t // 2, 2 * n_b, D_MODEL), F32),
            pltpu.VMEM((m, D_MODEL), F32),
            pltpu.VMEM((m, D_MODEL), F32),
            pltpu.VMEM((m, D_MODEL), F32),
            pltpu.VMEM((m, D_MODEL), F32),
            pltpu.VMEM((n_t // 2, 2 * n_b, D_MODEL), F32),
            pltpu.VMEM((m, D_MODEL), F32),
        ],
        compiler_params=pltpu.CompilerParams(
            dimension_semantics=("arbitrary",), vmem_limit_bytes=VMEM_LIMIT_BYTES),
        name=name,
    )(*xs, *(() if state is None else state), *weights)


def _block_diag_kernel(*refs, layouts):
    gpb = GROUPS_PER_BLOCK
    refs = list(refs)
    out_refs = refs[len(refs) - len(layouts):]
    for dst_ref, (grid, rows, cols, transpose) in zip(out_refs, layouts):
        height, width = gpb * rows, gpb * cols
        assert rows & (rows - 1) == 0 and cols & (cols - 1) == 0
        k = lax.broadcasted_iota(jnp.int32, (cols, width), 0)
        n = lax.broadcasted_iota(jnp.int32, (cols, width), 1)
        sel = jnp.where(jnp.bitwise_and(n, cols - 1) == k, 1.0, 0.0).astype(BF16)
        r = jnp.right_shift(lax.broadcasted_iota(jnp.int32, (height, width), 0),
                            rows.bit_length() - 1)
        c = jnp.right_shift(lax.broadcasted_iota(jnp.int32, (height, width), 1),
                            cols.bit_length() - 1)
        for ri, row_parts in enumerate(grid):
            for ci, part in enumerate(row_parts):
                if part is None:
                    cell = jnp.zeros((height, width), F32)
                else:
                    src_ref = refs.pop(0)
                    tiled = jnp.dot(src_ref[0].astype(BF16), sel, preferred_element_type=F32)
                    cell = jnp.where(r == c, tiled, 0.0)
                if transpose:
                    cell = cell.T
                h, w = cell.shape
                dst_ref[0, ri * h:(ri + 1) * h, ci * w:(ci + 1) * w] = cell.astype(BF16)


def _block_diag_call(layouts):
    gpb = GROUPS_PER_BLOCK
    ins, in_specs, out_specs, out_shapes, kernel_layouts = [], [], [], [], []
    for grid, rows, cols, transpose in layouts:
        for part in (p for row_parts in grid for p in row_parts if p is not None):
            ins.append(part.reshape(N_BLOCKS, gpb * rows, cols))
            in_specs.append(pl.BlockSpec((1, gpb * rows, cols), lambda i: (i, 0, 0)))
        cell = (gpb * cols, gpb * rows) if transpose else (gpb * rows, gpb * cols)
        shape = (len(grid) * cell[0], len(grid[0]) * cell[1])
        out_specs.append(pl.BlockSpec((1,) + shape, lambda i: (i, 0, 0)))
        out_shapes.append(jax.ShapeDtypeStruct((N_BLOCKS,) + shape, BF16))
        kernel_layouts.append(([[p is not None or None for p in row] for row in grid],
                               rows, cols, transpose))
    return pl.pallas_call(
        functools.partial(_block_diag_kernel, layouts=kernel_layouts),
        grid=(N_BLOCKS,), in_specs=in_specs, out_specs=out_specs, out_shape=out_shapes,
        name="s5_block_diag_weights",
    )(*ins)


def _ssm_weights(a_re, a_im, log_dt, b_re, b_im, c_re, c_im):
    n_c, n_p = SSM_GROUP_CH, SSM_STATE
    dt = jnp.exp(log_dt)[:, None]
    mag = jnp.exp(dt * a_re)
    ang = dt * a_im
    abar_re = mag * jnp.cos(ang)
    abar_im = mag * jnp.sin(ang)
    den = a_re * a_re + a_im * a_im
    nr = abar_re - 1.0
    ni = abar_im
    q_re = ((nr * a_re + ni * a_im) / den)[:, None, :]
    q_im = ((ni * a_re - nr * a_im) / den)[:, None, :]
    a2_re = abar_re * abar_re - abar_im * abar_im
    a2_im = 2.0 * abar_re * abar_im

    bt_re = jnp.swapaxes(b_re, 1, 2)
    bt_im = jnp.swapaxes(b_im, 1, 2)
    bbar_re = q_re * bt_re - q_im * bt_im
    bbar_im = q_re * bt_im + q_im * bt_re
    ar, ai = abar_re[:, None, :], abar_im[:, None, :]
    ab_re = ar * bbar_re - ai * bbar_im
    ab_im = ar * bbar_im + ai * bbar_re

    ca_re = c_re * ar - c_im * ai
    ca_im = c_re * ai + c_im * ar
    ar2, ai2 = a2_re[:, None, :], a2_im[:, None, :]
    ca2_re = c_re * ar2 - c_im * ai2
    ca2_im = c_re * ai2 + c_im * ar2

    lhs = jnp.concatenate([bbar_re, bbar_im], axis=-1)
    rhs = jnp.concatenate([jnp.concatenate([c_re, ca_re], axis=1),
                           jnp.concatenate([-c_im, -ca_im], axis=1)], axis=-1)
    direct = jnp.einsum('gip,gop->gio', lhs, rhs, precision=lax.Precision.HIGHEST)
    cb, cab = direct[..., :n_c], direct[..., n_c:]

    wb, wc, wd = _block_diag_call((
        ([[ab_re, ab_im], [bbar_re, bbar_im]], n_c, n_p, False),
        ([[ca_re, ca2_re], [-ca_im, -ca2_im]], n_c, n_p, True),
        ([[cb, cab], [None, cb]], n_c, n_c, False)))
    return a2_re, a2_im, wb, wc, wd


PROMPT_T = 32
SAMPLE_B = 32


def kernel(x_prompt, x_sample, state_ssm_re, state_ssm_im, state_pool, meta_tokens, norm_gain, w_in, b_gate, ssm_a_re, ssm_a_im, ssm_log_dt, ssm_b_re, ssm_b_im, ssm_c_re, ssm_c_im, ssm_d, w_glu, b_glu, pool_mix, pool_scale, w_branch_ssm, w_branch_pool, w_out, final_norm_gain):
    batch, seq, _ = x_prompt.shape
    dec_batch, dec_seq, _ = x_sample.shape
    depth = norm_gain.shape[0]
    assert depth == 1
    l = 0
    a2_re, a2_im, wb, wc, wd = _ssm_weights(
        ssm_a_re[l], ssm_a_im[l], ssm_log_dt[l], ssm_b_re[l], ssm_b_im[l],
        ssm_c_re[l], ssm_c_im[l])
    row = lambda v: v.reshape(1, -1).astype(F32)
    weights = (
        row(norm_gain[l]), w_in[l].astype(BF16), row(b_gate[l]),
        row(a2_re), row(a2_im), wb, wc, wd, row(ssm_d[l]),
        w_glu[l].astype(BF16), row(b_glu[l]), pool_mix[l].astype(BF16), row(pool_scale[l]),
        w_branch_ssm[l].astype(BF16), w_branch_pool[l].astype(BF16), w_out[l].astype(BF16),
        row(final_norm_gain),
    )

    assert batch == SUBLANES
    meta = jnp.broadcast_to(meta_tokens[None].astype(x_prompt.dtype), (batch, N_META, D_MODEL))
    _, *meta_state = _layer_call(
        meta, None, weights,
        n_t=N_META, n_b=batch, carry=False, pos0=0, name="meta_layer")
    y_prompt, p_re, p_im, p_buf = _layer_call(
        x_prompt, meta_state, weights,
        n_t=PROMPT_T, n_b=batch, carry=True, pos0=N_META, name="prompt_layer")

    sample_state = (state_ssm_re[l].reshape(dec_batch, STATE_W),
                    state_ssm_im[l].reshape(dec_batch, STATE_W),
                    state_pool[l])
    y_sample, s_re, s_im, s_buf = _layer_call(
        x_sample, sample_state, weights,
        n_t=dec_seq, n_b=SAMPLE_B, carry=False, pos0=PAST_LEN, name="sample_layer")

    state_shape = (1, -1, SSM_GROUPS, SSM_STATE)
    return (y_prompt, y_sample,
            p_re.reshape(state_shape), p_im.reshape(state_shape), p_buf[None],
            s_re.reshape(state_shape), s_im.reshape(state_shape), s_buf[None])
```

```python
import functools

import jax
import jax.numpy as jnp
from jax import lax
from jax.experimental import pallas as pl
from jax.experimental.pallas import tpu as pltpu

D_MODEL = 1024
N_META = 16
SSM_GROUP_CH = 16
SSM_GROUPS = D_MODEL // SSM_GROUP_CH
SSM_STATE = 64
POOL_WINDOWS = (2, 4, 8, 16)
POOL_GROUP_CH = D_MODEL // len(POOL_WINDOWS)
POOL_BUF = max(POOL_WINDOWS) - 1
EPS = 1e-6
PAST_LEN = 16384

SUBLANES = 8
LANES = 128
GROUPS_PER_BLOCK = LANES // SSM_GROUP_CH
N_BLOCKS = SSM_GROUPS // GROUPS_PER_BLOCK
BLOCK_STATE = GROUPS_PER_BLOCK * SSM_STATE
STATE_W = SSM_GROUPS * SSM_STATE
VMEM_LIMIT_BYTES = 56 * 1024 * 1024

F32 = jnp.float32
BF16 = jnp.bfloat16


def _rmsnorm(x, gain):
    ms = jnp.mean(x * x, axis=-1, keepdims=True)
    return (x * lax.rsqrt(ms + EPS)) * gain


_sigmoid = jax.nn.sigmoid


def _silu(x):
    return x * _sigmoid(x)


def _layer_kernel(*refs, n_t, n_b, carry, pos0, zero_state):
    if carry:
        x_ref, x_next_ref, *refs = refs
    else:
        x_ref, *refs = refs
    if not zero_state:
        s0re_ref, s0im_ref, buf0_ref, *refs = refs
    (gain_ref, w_in_ref, b_gate_ref, a2_re_ref, a2_im_ref, wb_ref, wc_ref, wd_ref, d_ref,
     w_glu_ref, b_glu_ref, pmix_ref, pscale_ref, w_bs_ref, w_bp_ref, w_out_ref, fgain_ref,
     y_ref, ore_ref, oim_ref, obuf_ref,
     sre_ref, sim_ref, ext_ref, bu_ref, u2_ref, xn_ref, us_ref, zs_ref, gs_ref, zp_ref,
     gp_ref, ys_ref, yp_ref) = refs
    m = n_t * n_b
    n_pairs = n_t // 2
    half = n_pairs * n_b
    step = pl.program_id(0)
    hist_rows = POOL_BUF * n_b

    def load_state():
        if zero_state:
            sre_ref[...] = jnp.zeros(sre_ref.shape, F32)
            sim_ref[...] = jnp.zeros(sim_ref.shape, F32)
            ext_ref[0:hist_rows, :] = jnp.zeros((hist_rows, D_MODEL), F32)
        else:
            sre_ref[...] = s0re_ref[...]
            sim_ref[...] = s0im_ref[...]
            ext_ref[0:hist_rows, :] = buf0_ref[...].reshape(hist_rows, D_MODEL)

    def rows_tb(src_ref):
        return jnp.swapaxes(src_ref[...], 0, 1).reshape(m, D_MODEL)

    def proj(k):
        return jnp.dot(xn_ref[...], w_in_ref[:, k * D_MODEL:(k + 1) * D_MODEL],
                       preferred_element_type=F32)

    def prepare_norm(src_ref):
        xn_ref[...] = _rmsnorm(rows_tb(src_ref), gain_ref[...]).astype(BF16)

    def prepare_u():
        us_ref[...] = proj(0).reshape(n_pairs, 2 * n_b, D_MODEL)

    def prepare_slabs():
        for blk in range(N_BLOCKS):
            lanes = slice(blk * LANES, (blk + 1) * LANES)
            st = slice(blk * BLOCK_STATE, (blk + 1) * BLOCK_STATE)
            u_t0 = us_ref[:, 0:n_b, lanes].reshape(half, LANES)
            u_t1 = us_ref[:, n_b:, lanes].reshape(half, LANES)
            u2_ref[blk] = jnp.concatenate([u_t0, u_t1], axis=1).astype(BF16)
            bu_ref[blk, 0:n_b, 0:BLOCK_STATE] = sre_ref[:, st]
            bu_ref[blk, 0:n_b, BLOCK_STATE:] = sim_ref[:, st]
            bu_ref[blk, n_b:, :] = jnp.dot(u2_ref[blk], wb_ref[blk], preferred_element_type=F32)

    def prepare(src_ref):
        prepare_norm(src_ref)
        prepare_u()
        prepare_slabs()

    if carry:
        @pl.when(step == 0)
        def _():
            load_state()
            prepare(x_ref)
    else:
        load_state()
        prepare(x_ref)

    piece = 2 * LANES

    def proj_piece(k, p):
        cols = slice(p * piece, (p + 1) * piece)
        v = jnp.dot(xn_ref[...], w_in_ref[:, k * D_MODEL + p * piece:k * D_MODEL + (p + 1) * piece],
                    preferred_element_type=F32)
        if k == 1:
            zs_ref[:, cols] = _silu(v)
        elif k == 4:
            gs_ref[:, cols] = _sigmoid(v + b_gate_ref[:, cols])
        elif k == 2:
            ext_ref[hist_rows:hist_rows + m, cols] = v
        elif k == 3:
            zp_ref[:, cols] = _silu(v)
        else:
            gp_ref[:, cols] = _sigmoid(
                v + b_gate_ref[:, D_MODEL + p * piece:D_MODEL + (p + 1) * piece])

    pieces = [(k, p) for k in (1, 4, 2, 3, 5) for p in range(D_MODEL // piece)]

    for blk in range(N_BLOCKS):
        lanes = slice(blk * LANES, (blk + 1) * LANES)
        st = slice(blk * BLOCK_STATE, (blk + 1) * BLOCK_STATE)
        for k, p in pieces[blk * len(pieces) // N_BLOCKS:(blk + 1) * len(pieces) // N_BLOCKS]:
            proj_piece(k, p)
        a_re = jnp.broadcast_to(a2_re_ref[:, st], (SUBLANES, BLOCK_STATE))
        a_im = jnp.broadcast_to(a2_im_ref[:, st], (SUBLANES, BLOCK_STATE))
        for j in range(n_b // SUBLANES):
            tile = slice(j * SUBLANES, (j + 1) * SUBLANES)
            s_re = sre_ref[tile, st]
            s_im = sim_ref[tile, st]
            for k in range(n_pairs):
                rows = slice((k + 1) * n_b + j * SUBLANES, (k + 1) * n_b + (j + 1) * SUBLANES)
                n_re = a_re * s_re - a_im * s_im + bu_ref[blk, rows, 0:BLOCK_STATE]
                n_im = a_re * s_im + a_im * s_re + bu_ref[blk, rows, BLOCK_STATE:]
                bu_ref[blk, rows, 0:BLOCK_STATE] = n_re
                bu_ref[blk, rows, BLOCK_STATE:] = n_im
                s_re, s_im = n_re, n_im
            sre_ref[tile, st] = s_re
            sim_ref[tile, st] = s_im
        y2 = (jnp.dot(bu_ref[blk, 0:half, :].astype(BF16), wc_ref[blk], preferred_element_type=F32)
              + jnp.dot(u2_ref[blk], wd_ref[blk], preferred_element_type=F32))
        ys_ref[:, 0:n_b, lanes] = y2[:, 0:LANES].reshape(n_pairs, n_b, LANES)
        ys_ref[:, n_b:, lanes] = y2[:, LANES:].reshape(n_pairs, n_b, LANES)

    if pos0 + 1 >= max(POOL_WINDOWS):
        pos = None
    else:
        t_loc = lax.broadcasted_iota(jnp.int32, (m, 1), 0) // n_b
        pos = pos0 + step * n_t + t_loc
    for gi, w in enumerate(POOL_WINDOWS):
        cols = slice(gi * POOL_GROUP_CH, (gi + 1) * POOL_GROUP_CH)
        s = ext_ref[(POOL_BUF + 1 - w) * n_b:hist_rows + m, cols]
        span = 1
        while span < w:
            keep = s.shape[0] - span * n_b
            s = s[:keep] + s[span * n_b:]
            span *= 2
        if pos is None:
            mean = s * (1.0 / w)
        else:
            mean = s * (1.0 / jnp.minimum(pos + 1, w).astype(F32))
        pooled = mean - ext_ref[hist_rows:hist_rows + m, cols]
        yp_ref[:, cols] = jnp.dot(pooled.astype(BF16), pmix_ref[gi],
                                  preferred_element_type=F32) * pscale_ref[:, cols]
    for k in range(POOL_BUF):
        ext_ref[k * n_b:(k + 1) * n_b, :] = ext_ref[(k + n_t) * n_b:(k + n_t + 1) * n_b, :]

    y = ys_ref[...].reshape(m, D_MODEL) + d_ref[...] * us_ref[...].reshape(m, D_MODEL)
    y = jax.nn.gelu(y)
    b = yp_ref[...] * zp_ref[...]
    acc_p = jnp.dot(b.astype(BF16), w_bp_ref[...], preferred_element_type=F32)
    glu = jnp.dot(y.astype(BF16), w_glu_ref[...], preferred_element_type=F32)
    if carry:
        prepare_norm(x_next_ref)
    a = y * _sigmoid(glu + b_glu_ref[...]) * zs_ref[...]
    acc_s = jnp.dot(a.astype(BF16), w_bs_ref[...], preferred_element_type=F32)
    if carry:
        prepare_u()
    merged = gs_ref[...] * acc_s + gp_ref[...] * acc_p
    delta = jnp.dot(merged.astype(BF16), w_out_ref[...], preferred_element_type=F32)
    if carry:
        prepare_slabs()
    y_rows = _rmsnorm(rows_tb(x_ref) + delta, fgain_ref[...])
    y_ref[...] = jnp.swapaxes(y_rows.reshape(n_t, n_b, D_MODEL), 0, 1)

    def write_state():
        ore_ref[...] = sre_ref[...]
        oim_ref[...] = sim_ref[...]
        obuf_ref[...] = ext_ref[0:hist_rows, :].reshape(POOL_BUF, n_b, D_MODEL)

    if carry:
        pl.when(step == pl.num_programs(0) - 1)(write_state)
    else:
        write_state()


def _layer_call(x, state, weights, *, n_t, n_b, carry, pos0, name):
    batch, time, _ = x.shape
    m = n_t * n_b
    x_block = (n_b, n_t, D_MODEL)
    buf_block = (POOL_BUF, n_b, D_MODEL)
    if carry:
        assert batch == n_b and time % n_t == 0
        n_steps = time // n_t
        x_specs = [pl.BlockSpec(x_block, lambda i: (0, i, 0)),
                   pl.BlockSpec(x_block, lambda i: (0, jnp.minimum(i + 1, n_steps - 1), 0))]
        y_map = lambda i: (0, i, 0)
        per_step = lambda i: (0, 0)
        per_step3 = lambda i: (0, 0, 0)
        xs = (x, x)
    else:
        assert time == n_t and batch % n_b == 0
        n_steps = batch // n_b
        x_specs = [pl.BlockSpec(x_block, lambda i: (i, 0, 0))]
        y_map = lambda i: (i, 0, 0)
        per_step = lambda i: (i, 0)
        per_step3 = lambda i: (0, i, 0)
        xs = (x,)
    whole = pl.BlockSpec(memory_space=pltpu.VMEM)
    state_specs = [
        pl.BlockSpec((n_b, STATE_W), per_step),
        pl.BlockSpec((n_b, STATE_W), per_step),
        pl.BlockSpec(buf_block, per_step3),
    ]
    kernel = functools.partial(_layer_kernel, n_t=n_t, n_b=n_b, carry=carry, pos0=pos0,
                               zero_state=state is None)
    return pl.pallas_call(
        kernel,
        grid=(n_steps,),
        in_specs=x_specs + ([] if state is None else state_specs) + [whole] * len(weights),
        out_specs=[pl.BlockSpec(x_block, y_map)] + state_specs,
        out_shape=[
            jax.ShapeDtypeStruct((batch, time, D_MODEL), F32),
            jax.ShapeDtypeStruct((batch, STATE_W), F32),
            jax.ShapeDtypeStruct((batch, STATE_W), F32),
            jax.ShapeDtypeStruct((POOL_BUF, batch, D_MODEL), F32),
        ],
        scratch_shapes=[
            pltpu.VMEM((n_b, STATE_W), F32),
            pltpu.VMEM((n_b, STATE_W), F32),
            pltpu.VMEM(((POOL_BUF + n_t) * n_b, D_MODEL), F32),
            pltpu.VMEM((N_BLOCKS, m // 2 + n_b, 2 * BLOCK_STATE), F32),
            pltpu.VMEM((N_BLOCKS, m // 2, 2 * LANES), BF16),
            pltpu.VMEM((m, D_MODEL), BF16),
            pltpu.VMEM((n_t // 2, 2 * n_b, D_MODEL), F32),
            pltpu.VMEM((m, D_MODEL), F32),
            pltpu.VMEM((m, D_MODEL), F32),
            pltpu.VMEM((m, D_MODEL), F32),
            pltpu.VMEM((m, D_MODEL), F32),
            pltpu.VMEM((n_t // 2, 2 * n_b, D_MODEL), F32),
            pltpu.VMEM((m, D_MODEL), F32),
        ],
        compiler_params=pltpu.CompilerParams(
            dimension_semantics=("arbitrary",), vmem_limit_bytes=VMEM_LIMIT_BYTES),
        name=name,
    )(*xs, *(() if state is None else state), *weights)


PREP_BLOCKS_PER_STEP = 4


def _block_diag_kernel(*refs, layouts):
    gpb = GROUPS_PER_BLOCK
    refs = list(refs)
    out_refs = refs[len(refs) - len(layouts):]
    for dst_ref, (grid, rows, cols, transpose) in zip(out_refs, layouts):
        height, width = gpb * rows, gpb * cols
        assert rows & (rows - 1) == 0 and cols & (cols - 1) == 0
        k = lax.broadcasted_iota(jnp.int32, (cols, width), 0)
        n = lax.broadcasted_iota(jnp.int32, (cols, width), 1)
        sel = jnp.where(jnp.bitwise_and(n, cols - 1) == k, 1.0, 0.0).astype(BF16)
        r = jnp.right_shift(lax.broadcasted_iota(jnp.int32, (height, width), 0),
                            rows.bit_length() - 1)
        c = jnp.right_shift(lax.broadcasted_iota(jnp.int32, (height, width), 1),
                            cols.bit_length() - 1)
        for ri, row_parts in enumerate(grid):
            for ci, part in enumerate(row_parts):
                src_ref = None if part is None else refs.pop(0)
                for blk in range(PREP_BLOCKS_PER_STEP):
                    if src_ref is None:
                        cell = jnp.zeros((height, width), F32)
                    else:
                        tiled = jnp.dot(src_ref[blk].astype(BF16), sel,
                                        preferred_element_type=F32)
                        cell = jnp.where(r == c, tiled, 0.0)
                    if transpose:
                        cell = cell.T
                    h, w = cell.shape
                    dst_ref[blk, ri * h:(ri + 1) * h, ci * w:(ci + 1) * w] = cell.astype(BF16)


def _block_diag_call(layouts):
    gpb = GROUPS_PER_BLOCK
    per_step = PREP_BLOCKS_PER_STEP
    ins, in_specs, out_specs, out_shapes, kernel_layouts = [], [], [], [], []
    for grid, rows, cols, transpose in layouts:
        for part in (p for row_parts in grid for p in row_parts if p is not None):
            ins.append(part.reshape(N_BLOCKS, gpb * rows, cols))
            in_specs.append(pl.BlockSpec((per_step, gpb * rows, cols), lambda i: (i, 0, 0)))
        cell = (gpb * cols, gpb * rows) if transpose else (gpb * rows, gpb * cols)
        shape = (len(grid) * cell[0], len(grid[0]) * cell[1])
        out_specs.append(pl.BlockSpec((per_step,) + shape, lambda i: (i, 0, 0)))
        out_shapes.append(jax.ShapeDtypeStruct((N_BLOCKS,) + shape, BF16))
        kernel_layouts.append(([[p is not None or None for p in row] for row in grid],
                               rows, cols, transpose))
    return pl.pallas_call(
        functools.partial(_block_diag_kernel, layouts=kernel_layouts),
        grid=(N_BLOCKS // per_step,), in_specs=in_specs, out_specs=out_specs,
        out_shape=out_shapes,
        name="s5_block_diag_weights",
    )(*ins)


def _ssm_weights(a_re, a_im, log_dt, b_re, b_im, c_re, c_im):
    n_c, n_p = SSM_GROUP_CH, SSM_STATE
    dt = jnp.exp(log_dt)[:, None]
    mag = jnp.exp(dt * a_re)
    ang = dt * a_im
    abar_re = mag * jnp.cos(ang)
    abar_im = mag * jnp.sin(ang)
    den = a_re * a_re + a_im * a_im
    nr = abar_re - 1.0
    ni = abar_im
    q_re = ((nr * a_re + ni * a_im) / den)[:, None, :]
    q_im = ((ni * a_re - nr * a_im) / den)[:, None, :]
    a2_re = abar_re * abar_re - abar_im * abar_im
    a2_im = 2.0 * abar_re * abar_im

    bt_re = jnp.swapaxes(b_re, 1, 2)
    bt_im = jnp.swapaxes(b_im, 1, 2)
    bbar_re = q_re * bt_re - q_im * bt_im
    bbar_im = q_re * bt_im + q_im * bt_re
    ar, ai = abar_re[:, None, :], abar_im[:, None, :]
    ab_re = ar * bbar_re - ai * bbar_im
    ab_im = ar * bbar_im + ai * bbar_re

    ca_re = c_re * ar - c_im * ai
    ca_im = c_re * ai + c_im * ar
    ar2, ai2 = a2_re[:, None, :], a2_im[:, None, :]
    ca2_re = c_re * ar2 - c_im * ai2
    ca2_im = c_re * ai2 + c_im * ar2

    lhs = jnp.concatenate([bbar_re, bbar_im], axis=-1)
    rhs = jnp.concatenate([jnp.concatenate([c_re, ca_re], axis=1),
                           jnp.concatenate([-c_im, -ca_im], axis=1)], axis=-1)
    direct = jnp.einsum('gip,gop->gio', lhs, rhs, precision=lax.Precision.HIGHEST)
    cb, cab = direct[..., :n_c], direct[..., n_c:]

    wb, wc, wd = _block_diag_call((
        ([[ab_re, ab_im], [bbar_re, bbar_im]], n_c, n_p, False),
        ([[ca_re, ca2_re], [-ca_im, -ca2_im]], n_c, n_p, True),
        ([[cb, cab], [None, cb]], n_c, n_c, False)))
    return a2_re, a2_im, wb, wc, wd


PROMPT_T = 32
SAMPLE_B = 32


def kernel(x_prompt, x_sample, state_ssm_re, state_ssm_im, state_pool, meta_tokens, norm_gain, w_in, b_gate, ssm_a_re, ssm_a_im, ssm_log_dt, ssm_b_re, ssm_b_im, ssm_c_re, ssm_c_im, ssm_d, w_glu, b_glu, pool_mix, pool_scale, w_branch_ssm, w_branch_pool, w_out, final_norm_gain):
    batch, seq, _ = x_prompt.shape
    dec_batch, dec_seq, _ = x_sample.shape
    depth = norm_gain.shape[0]
    assert depth == 1
    l = 0
    a2_re, a2_im, wb, wc, wd = _ssm_weights(
        ssm_a_re[l], ssm_a_im[l], ssm_log_dt[l], ssm_b_re[l], ssm_b_im[l],
        ssm_c_re[l], ssm_c_im[l])
    row = lambda v: v.reshape(1, -1).astype(F32)
    weights = (
        row(norm_gain[l]), w_in[l].astype(BF16), row(b_gate[l]),
        row(a2_re), row(a2_im), wb, wc, wd, row(ssm_d[l]),
        w_glu[l].astype(BF16), row(b_glu[l]), pool_mix[l].astype(BF16), row(pool_scale[l]),
        w_branch_ssm[l].astype(BF16), w_branch_pool[l].astype(BF16), w_out[l].astype(BF16),
        row(final_norm_gain),
    )

    assert batch == SUBLANES
    meta = jnp.broadcast_to(meta_tokens[None].astype(x_prompt.dtype), (batch, N_META, D_MODEL))
    _, *meta_state = _layer_call(
        meta, None, weights,
        n_t=N_META, n_b=batch, carry=False, pos0=0, name="meta_layer")
    y_prompt, p_re, p_im, p_buf = _layer_call(
        x_prompt, meta_state, weights,
        n_t=PROMPT_T, n_b=batch, carry=True, pos0=N_META, name="prompt_layer")

    sample_state = (state_ssm_re[l].reshape(dec_batch, STATE_W),
                    state_ssm_im[l].reshape(dec_batch, STATE_W),
                    jnp.swapaxes(state_pool[l], 0, 1))
    y_sample, s_re, s_im, s_buf = _layer_call(
        x_sample, sample_state, weights,
        n_t=dec_seq, n_b=SAMPLE_B, carry=False, pos0=PAST_LEN, name="sample_layer")

    state_shape = (1, -1, SSM_GROUPS, SSM_STATE)
    return (y_prompt, y_sample,
            p_re.reshape(state_shape), p_im.reshape(state_shape), jnp.swapaxes(p_buf, 0, 1)[None],
            s_re.reshape(state_shape), s_im.reshape(state_shape), jnp.swapaxes(s_buf, 0, 1)[None])
```

```python
import functools

import jax
import jax.numpy as jnp
from jax import lax
from jax.experimental import pallas as pl
from jax.experimental.pallas import tpu as pltpu

D_MODEL = 1024
N_META = 16
SSM_GROUP_CH = 16
SSM_GROUPS = D_MODEL // SSM_GROUP_CH
SSM_STATE = 64
POOL_WINDOWS = (2, 4, 8, 16)
POOL_GROUP_CH = D_MODEL // len(POOL_WINDOWS)
POOL_BUF = max(POOL_WINDOWS) - 1
EPS = 1e-6
PAST_LEN = 16384

SUBLANES = 8
LANES = 128
GROUPS_PER_BLOCK = LANES // SSM_GROUP_CH
N_BLOCKS = SSM_GROUPS // GROUPS_PER_BLOCK
BLOCK_STATE = GROUPS_PER_BLOCK * SSM_STATE
STATE_W = SSM_GROUPS * SSM_STATE
VMEM_LIMIT_BYTES = 56 * 1024 * 1024

F32 = jnp.float32
BF16 = jnp.bfloat16


def _rmsnorm(x, gain):
    ms = jnp.mean(x * x, axis=-1, keepdims=True)
    return (x * lax.rsqrt(ms + EPS)) * gain


_sigmoid = jax.nn.sigmoid


def _silu(x):
    return x * _sigmoid(x)


def _layer_kernel(*refs, n_t, n_b, carry, pos0, zero_state):
    if carry:
        x_first_ref, x_next_ref, *refs = refs
    else:
        x_ref, *refs = refs
    if not zero_state:
        s0re_ref, s0im_ref, buf0_ref, *refs = refs
    (gain_ref, w_in_ref, b_gate_ref, a2_re_ref, a2_im_ref, wb_ref, wc_ref, wd_ref, d_ref,
     w_glu_ref, b_glu_ref, pmix_ref, pscale_ref, w_bs_ref, w_bp_ref, w_out_ref, fgain_ref,
     y_ref, ore_ref, oim_ref, obuf_ref,
     sre_ref, sim_ref, ext_ref, bu_ref, u2_ref, xn_ref, us_ref, zs_ref, gs_ref, zp_ref,
     gp_ref, ys_ref, yp_ref, *tail) = refs
    if carry:
        (x_ref,) = tail
    m = n_t * n_b
    n_pairs = n_t // 2
    half = n_pairs * n_b
    step = pl.program_id(0)
    hist_rows = POOL_BUF * n_b

    def load_state():
        if zero_state:
            sre_ref[...] = jnp.zeros(sre_ref.shape, F32)
            sim_ref[...] = jnp.zeros(sim_ref.shape, F32)
            ext_ref[0:hist_rows, :] = jnp.zeros((hist_rows, D_MODEL), F32)
        else:
            sre_ref[...] = s0re_ref[...]
            sim_ref[...] = s0im_ref[...]
            ext_ref[0:hist_rows, :] = buf0_ref[...].reshape(hist_rows, D_MODEL)

    def rows_tb(src_ref):
        return jnp.swapaxes(src_ref[...], 0, 1).reshape(m, D_MODEL)

    def proj(k):
        return jnp.dot(xn_ref[...], w_in_ref[:, k * D_MODEL:(k + 1) * D_MODEL],
                       preferred_element_type=F32)

    def prepare_norm(src_ref):
        xn_ref[...] = _rmsnorm(rows_tb(src_ref), gain_ref[...]).astype(BF16)

    def prepare_u():
        us_ref[...] = proj(0).reshape(n_pairs, 2 * n_b, D_MODEL)

    def prepare_pairs():
        for blk in range(N_BLOCKS):
            lanes = slice(blk * LANES, (blk + 1) * LANES)
            st = slice(blk * BLOCK_STATE, (blk + 1) * BLOCK_STATE)
            u_t0 = us_ref[:, 0:n_b, lanes].reshape(half, LANES)
            u_t1 = us_ref[:, n_b:, lanes].reshape(half, LANES)
            u2_ref[blk] = jnp.concatenate([u_t0, u_t1], axis=1).astype(BF16)
            bu_ref[blk, 0:n_b, 0:BLOCK_STATE] = sre_ref[:, st]
            bu_ref[blk, 0:n_b, BLOCK_STATE:] = sim_ref[:, st]

    def prepare_slabs():
        for blk in range(N_BLOCKS):
            bu_ref[blk, n_b:, :] = jnp.dot(u2_ref[blk], wb_ref[blk], preferred_element_type=F32)

    def prepare(src_ref):
        prepare_norm(src_ref)
        prepare_u()
        prepare_pairs()
        prepare_slabs()

    if carry:
        @pl.when(step == 0)
        def _():
            load_state()
            x_ref[...] = x_first_ref[...]
            prepare(x_ref)
    else:
        load_state()
        prepare(x_ref)

    piece = 2 * LANES

    def proj_piece(k, p):
        cols = slice(p * piece, (p + 1) * piece)
        v = jnp.dot(xn_ref[...], w_in_ref[:, k * D_MODEL + p * piece:k * D_MODEL + (p + 1) * piece],
                    preferred_element_type=F32)
        if k == 1:
            zs_ref[:, cols] = _silu(v)
        elif k == 4:
            gs_ref[:, cols] = _sigmoid(v + b_gate_ref[:, cols])
        elif k == 2:
            ext_ref[hist_rows:hist_rows + m, cols] = v
        elif k == 3:
            zp_ref[:, cols] = _silu(v)
        else:
            gp_ref[:, cols] = _sigmoid(
                v + b_gate_ref[:, D_MODEL + p * piece:D_MODEL + (p + 1) * piece])

    pieces = [(k, p) for k in (1, 4, 2, 3, 5) for p in range(D_MODEL // piece)]

    for blk in range(N_BLOCKS):
        lanes = slice(blk * LANES, (blk + 1) * LANES)
        st = slice(blk * BLOCK_STATE, (blk + 1) * BLOCK_STATE)
        for k, p in pieces[blk * len(pieces) // N_BLOCKS:(blk + 1) * len(pieces) // N_BLOCKS]:
            proj_piece(k, p)
        a_re = jnp.broadcast_to(a2_re_ref[:, st], (SUBLANES, BLOCK_STATE))
        a_im = jnp.broadcast_to(a2_im_ref[:, st], (SUBLANES, BLOCK_STATE))
        for j in range(n_b // SUBLANES):
            tile = slice(j * SUBLANES, (j + 1) * SUBLANES)
            s_re = sre_ref[tile, st]
            s_im = sim_ref[tile, st]
            for k in range(n_pairs):
                rows = slice((k + 1) * n_b + j * SUBLANES, (k + 1) * n_b + (j + 1) * SUBLANES)
                n_re = a_re * s_re - a_im * s_im + bu_ref[blk, rows, 0:BLOCK_STATE]
                n_im = a_re * s_im + a_im * s_re + bu_ref[blk, rows, BLOCK_STATE:]
                bu_ref[blk, rows, 0:BLOCK_STATE] = n_re
                bu_ref[blk, rows, BLOCK_STATE:] = n_im
                s_re, s_im = n_re, n_im
            sre_ref[tile, st] = s_re
            sim_ref[tile, st] = s_im
        y2 = (jnp.dot(bu_ref[blk, 0:half, :].astype(BF16), wc_ref[blk], preferred_element_type=F32)
              + jnp.dot(u2_ref[blk], wd_ref[blk], preferred_element_type=F32))
        ys_ref[:, 0:n_b, lanes] = y2[:, 0:LANES].reshape(n_pairs, n_b, LANES)
        ys_ref[:, n_b:, lanes] = y2[:, LANES:].reshape(n_pairs, n_b, LANES)

    if pos0 + 1 >= max(POOL_WINDOWS):
        pos = None
    else:
        t_loc = lax.broadcasted_iota(jnp.int32, (m, 1), 0) // n_b
        pos = pos0 + step * n_t + t_loc
    for gi, w in enumerate(POOL_WINDOWS):
        cols = slice(gi * POOL_GROUP_CH, (gi + 1) * POOL_GROUP_CH)
        s = ext_ref[(POOL_BUF + 1 - w) * n_b:hist_rows + m, cols]
        span = 1
        while span < w:
            keep = s.shape[0] - span * n_b
            s = s[:keep] + s[span * n_b:]
            span *= 2
        if pos is None:
            mean = s * (1.0 / w)
        else:
            mean = s * (1.0 / jnp.minimum(pos + 1, w).astype(F32))
        pooled = mean - ext_ref[hist_rows:hist_rows + m, cols]
        yp_ref[:, cols] = jnp.dot(pooled.astype(BF16), pmix_ref[gi],
                                  preferred_element_type=F32) * pscale_ref[:, cols]
    for k in range(POOL_BUF):
        ext_ref[k * n_b:(k + 1) * n_b, :] = ext_ref[(k + n_t) * n_b:(k + n_t + 1) * n_b, :]

    y = ys_ref[...].reshape(m, D_MODEL) + d_ref[...] * us_ref[...].reshape(m, D_MODEL)
    y = jax.nn.gelu(y)
    b = yp_ref[...] * zp_ref[...]
    acc_p = jnp.dot(b.astype(BF16), w_bp_ref[...], preferred_element_type=F32)
    glu = jnp.dot(y.astype(BF16), w_glu_ref[...], preferred_element_type=F32)
    if carry:
        prepare_norm(x_next_ref)
    a = y * _sigmoid(glu + b_glu_ref[...]) * zs_ref[...]
    acc_s = jnp.dot(a.astype(BF16), w_bs_ref[...], preferred_element_type=F32)
    if carry:
        prepare_u()
        prepare_pairs()
    merged = gs_ref[...] * acc_s + gp_ref[...] * acc_p
    delta = jnp.dot(merged.astype(BF16), w_out_ref[...], preferred_element_type=F32)
    if carry:
        prepare_slabs()
    y_rows = _rmsnorm(rows_tb(x_ref) + delta, fgain_ref[...])
    y_ref[...] = jnp.swapaxes(y_rows.reshape(n_t, n_b, D_MODEL), 0, 1)
    if carry:
        x_ref[...] = x_next_ref[...]

    def write_state():
        ore_ref[...] = sre_ref[...]
        oim_ref[...] = sim_ref[...]
        obuf_ref[...] = ext_ref[0:hist_rows, :].reshape(POOL_BUF, n_b, D_MODEL)

    if carry:
        pl.when(step == pl.num_programs(0) - 1)(write_state)
    else:
        write_state()


def _layer_call(x, state, weights, *, n_t, n_b, carry, pos0, name):
    batch, time, _ = x.shape
    m = n_t * n_b
    x_block = (n_b, n_t, D_MODEL)
    buf_block = (POOL_BUF, n_b, D_MODEL)
    if carry:
        assert batch == n_b and time % n_t == 0
        n_steps = time // n_t
        x_specs = [pl.BlockSpec(x_block, lambda i: (0, 0, 0)),
                   pl.BlockSpec(x_block, lambda i: (0, jnp.minimum(i + 1, n_steps - 1), 0))]
        y_map = lambda i: (0, i, 0)
        per_step = lambda i: (0, 0)
        per_step3 = lambda i: (0, 0, 0)
        xs = (x, x)
        extra_scratch = [pltpu.VMEM(x_block, F32)]
    else:
        assert time == n_t and batch % n_b == 0
        n_steps = batch // n_b
        x_specs = [pl.BlockSpec(x_block, lambda i: (i, 0, 0))]
        y_map = lambda i: (i, 0, 0)
        per_step = lambda i: (i, 0)
        per_step3 = lambda i: (0, i, 0)
        xs = (x,)
        extra_scratch = []
    whole = pl.BlockSpec(memory_space=pltpu.VMEM)
    state_specs = [
        pl.BlockSpec((n_b, STATE_W), per_step),
        pl.BlockSpec((n_b, STATE_W), per_step),
        pl.BlockSpec(buf_block, per_step3),
    ]
    kernel = functools.partial(_layer_kernel, n_t=n_t, n_b=n_b, carry=carry, pos0=pos0,
                               zero_state=state is None)
    return pl.pallas_call(
        kernel,
        grid=(n_steps,),
        in_specs=x_specs + ([] if state is None else state_specs) + [whole] * len(weights),
        out_specs=[pl.BlockSpec(x_block, y_map)] + state_specs,
        out_shape=[
            jax.ShapeDtypeStruct((batch, time, D_MODEL), F32),
            jax.ShapeDtypeStruct((batch, STATE_W), F32),
            jax.ShapeDtypeStruct((batch, STATE_W), F32),
            jax.ShapeDtypeStruct((POOL_BUF, batch, D_MODEL), F32),
        ],
        scratch_shapes=[
            pltpu.VMEM((n_b, STATE_W), F32),
            pltpu.VMEM((n_b, STATE_W), F32),
            pltpu.VMEM(((POOL_BUF + n_t) * n_b, D_MODEL), F32),
            pltpu.VMEM((N_BLOCKS, m // 2 + n_b, 2 * BLOCK_STATE), F32),
            pltpu.VMEM((N_BLOCKS, m // 2, 2 * LANES), BF16),
            pltpu.VMEM((m, D_MODEL), BF16),
            pltpu.VMEM((n_t // 2, 2 * n_b, D_MODEL), F32),
            pltpu.VMEM((m, D_MODEL), F32),
            pltpu.VMEM((m, D_MODEL), F32),
            pltpu.VMEM((m, D_MODEL), F32),
            pltpu.VMEM((m, D_MODEL), F32),
            pltpu.VMEM((n_t // 2, 2 * n_b, D_MODEL), F32),
            pltpu.VMEM((m, D_MODEL), F32),
        ] + extra_scratch,
        compiler_params=pltpu.CompilerParams(
            dimension_semantics=("arbitrary",), vmem_limit_bytes=VMEM_LIMIT_BYTES),
        name=name,
    )(*xs, *(() if state is None else state), *weights)


PREP_BLOCKS_PER_STEP = 4


def _block_diag_kernel(*refs, layouts):
    gpb = GROUPS_PER_BLOCK
    refs = list(refs)
    out_refs = refs[len(refs) - len(layouts):]
    for dst_ref, (grid, rows, cols, transpose) in zip(out_refs, layouts):
        height, width = gpb * rows, gpb * cols
        assert rows & (rows - 1) == 0 and cols & (cols - 1) == 0
        k = lax.broadcasted_iota(jnp.int32, (cols, width), 0)
        n = lax.broadcasted_iota(jnp.int32, (cols, width), 1)
        sel = jnp.where(jnp.bitwise_and(n, cols - 1) == k, 1.0, 0.0).astype(BF16)
        r = jnp.right_shift(lax.broadcasted_iota(jnp.int32, (height, width), 0),
                            rows.bit_length() - 1)
        c = jnp.right_shift(lax.broadcasted_iota(jnp.int32, (height, width), 1),
                            cols.bit_length() - 1)
        for ri, row_parts in enumerate(grid):
            for ci, part in enumerate(row_parts):
                src_ref = None if part is None else refs.pop(0)
                for blk in range(PREP_BLOCKS_PER_STEP):
                    if src_ref is None:
                        cell = jnp.zeros((height, width), F32)
                    else:
                        tiled = jnp.dot(src_ref[blk].astype(BF16), sel,
                                        preferred_element_type=F32)
                        cell = jnp.where(r == c, tiled, 0.0)
                    if transpose:
                        cell = cell.T
                    h, w = cell.shape
                    dst_ref[blk, ri * h:(ri + 1) * h, ci * w:(ci + 1) * w] = cell.astype(BF16)


def _block_diag_call(layouts):
    gpb = GROUPS_PER_BLOCK
    per_step = PREP_BLOCKS_PER_STEP
    ins, in_specs, out_specs, out_shapes, kernel_layouts = [], [], [], [], []
    for grid, rows, cols, transpose in layouts:
        for part in (p for row_parts in grid for p in row_parts if p is not None):
            ins.append(part.reshape(N_BLOCKS, gpb * rows, cols))
            in_specs.append(pl.BlockSpec((per_step, gpb * rows, cols), lambda i: (i, 0, 0)))
        cell = (gpb * cols, gpb * rows) if transpose else (gpb * rows, gpb * cols)
        shape = (len(grid) * cell[0], len(grid[0]) * cell[1])
        out_specs.append(pl.BlockSpec((per_step,) + shape, lambda i: (i, 0, 0)))
        out_shapes.append(jax.ShapeDtypeStruct((N_BLOCKS,) + shape, BF16))
        kernel_layouts.append(([[p is not None or None for p in row] for row in grid],
                               rows, cols, transpose))
    return pl.pallas_call(
        functools.partial(_block_diag_kernel, layouts=kernel_layouts),
        grid=(N_BLOCKS // per_step,), in_specs=in_specs, out_specs=out_specs,
        out_shape=out_shapes,
        name="s5_block_diag_weights",
    )(*ins)


def _ssm_weights(a_re, a_im, log_dt, b_re, b_im, c_re, c_im):
    n_c, n_p = SSM_GROUP_CH, SSM_STATE
    dt = jnp.exp(log_dt)[:, None]
    mag = jnp.exp(dt * a_re)
    ang = dt * a_im
    abar_re = mag * jnp.cos(ang)
    abar_im = mag * jnp.sin(ang)
    den = a_re * a_re + a_im * a_im
    nr = abar_re - 1.0
    ni = abar_im
    q_re = ((nr * a_re + ni * a_im) / den)[:, None, :]
    q_im = ((ni * a_re - nr * a_im) / den)[:, None, :]
    a2_re = abar_re * abar_re - abar_im * abar_im
    a2_im = 2.0 * abar_re * abar_im

    bt_re = jnp.swapaxes(b_re, 1, 2)
    bt_im = jnp.swapaxes(b_im, 1, 2)
    bbar_re = q_re * bt_re - q_im * bt_im
    bbar_im = q_re * bt_im + q_im * bt_re
    ar, ai = abar_re[:, None, :], abar_im[:, None, :]
    ab_re = ar * bbar_re - ai * bbar_im
    ab_im = ar * bbar_im + ai * bbar_re

    ca_re = c_re * ar - c_im * ai
    ca_im = c_re * ai + c_im * ar
    ar2, ai2 = a2_re[:, None, :], a2_im[:, None, :]
    ca2_re = c_re * ar2 - c_im * ai2
    ca2_im = c_re * ai2 + c_im * ar2

    lhs = jnp.concatenate([bbar_re, bbar_im], axis=-1)
    rhs = jnp.concatenate([jnp.concatenate([c_re, ca_re], axis=1),
                           jnp.concatenate([-c_im, -ca_im], axis=1)], axis=-1)
    direct = jnp.einsum('gip,gop->gio', lhs, rhs, precision=lax.Precision.HIGHEST)
    cb, cab = direct[..., :n_c], direct[..., n_c:]

    wb, wc, wd = _block_diag_call((
        ([[ab_re, ab_im], [bbar_re, bbar_im]], n_c, n_p, False),
        ([[ca_re, ca2_re], [-ca_im, -ca2_im]], n_c, n_p, True),
        ([[cb, cab], [None, cb]], n_c, n_c, False)))
    return a2_re, a2_im, wb, wc, wd


PROMPT_T = 32
SAMPLE_B = 32


def kernel(x_prompt, x_sample, state_ssm_re, state_ssm_im, state_pool, meta_tokens, norm_gain, w_in, b_gate, ssm_a_re, ssm_a_im, ssm_log_dt, ssm_b_re, ssm_b_im, ssm_c_re, ssm_c_im, ssm_d, w_glu, b_glu, pool_mix, pool_scale, w_branch_ssm, w_branch_pool, w_out, final_norm_gain):
    batch, seq, _ = x_prompt.shape
    dec_batch, dec_seq, _ = x_sample.shape
    depth = norm_gain.shape[0]
    assert depth == 1
    l = 0
    a2_re, a2_im, wb, wc, wd = _ssm_weights(
        ssm_a_re[l], ssm_a_im[l], ssm_log_dt[l], ssm_b_re[l], ssm_b_im[l],
        ssm_c_re[l], ssm_c_im[l])
    row = lambda v: v.reshape(1, -1).astype(F32)
    weights = (
        row(norm_gain[l]), w_in[l].astype(BF16), row(b_gate[l]),
        row(a2_re), row(a2_im), wb, wc, wd, row(ssm_d[l]),
        w_glu[l].astype(BF16), row(b_glu[l]), pool_mix[l].astype(BF16), row(pool_scale[l]),
        w_branch_ssm[l].astype(BF16), w_branch_pool[l].astype(BF16), w_out[l].astype(BF16),
        row(final_norm_gain),
    )

    assert batch == SUBLANES
    meta = jnp.broadcast_to(meta_tokens[None].astype(x_prompt.dtype), (batch, N_META, D_MODEL))
    _, *meta_state = _layer_call(
        meta, None, weights,
        n_t=N_META, n_b=batch, carry=False, pos0=0, name="meta_layer")
    y_prompt, p_re, p_im, p_buf = _layer_call(
        x_prompt, meta_state, weights,
        n_t=PROMPT_T, n_b=batch, carry=True, pos0=N_META, name="prompt_layer")

    sample_state = (state_ssm_re[l].reshape(dec_batch, STATE_W),
                    state_ssm_im[l].reshape(dec_batch, STATE_W),
                    jnp.swapaxes(state_pool[l], 0, 1))
    y_sample, s_re, s_im, s_buf = _layer_call(
        x_sample, sample_state, weights,
        n_t=dec_seq, n_b=SAMPLE_B, carry=False, pos0=PAST_LEN, name="sample_layer")

    state_shape = (1, -1, SSM_GROUPS, SSM_STATE)
    return (y_prompt, y_sample,
            p_re.reshape(state_shape), p_im.reshape(state_shape), jnp.swapaxes(p_buf, 0, 1)[None],
            s_re.reshape(state_shape), s_im.reshape(state_shape), jnp.swapaxes(s_buf, 0, 1)[None])
```

```python
import functools

import jax
import jax.numpy as jnp
from jax import lax
from jax.experimental import pallas as pl
from jax.experimental.pallas import tpu as pltpu

D_MODEL = 1024
N_META = 16
SSM_GROUP_CH = 16
SSM_GROUPS = D_MODEL // SSM_GROUP_CH
SSM_STATE = 64
POOL_WINDOWS = (2, 4, 8, 16)
POOL_GROUP_CH = D_MODEL // len(POOL_WINDOWS)
POOL_BUF = max(POOL_WINDOWS) - 1
EPS = 1e-6
PAST_LEN = 16384

SUBLANES = 8
LANES = 128
GROUPS_PER_BLOCK = LANES // SSM_GROUP_CH
N_BLOCKS = SSM_GROUPS // GROUPS_PER_BLOCK
BLOCK_STATE = GROUPS_PER_BLOCK * SSM_STATE
STATE_W = SSM_GROUPS * SSM_STATE
VMEM_LIMIT_BYTES = 56 * 1024 * 1024

F32 = jnp.float32
BF16 = jnp.bfloat16


def _rmsnorm(x, gain):
    ms = jnp.mean(x * x, axis=-1, keepdims=True)
    return (x * lax.rsqrt(ms + EPS)) * gain


_sigmoid = jax.nn.sigmoid


def _silu(x):
    return x * _sigmoid(x)


def _layer_kernel(*refs, n_t, n_b, carry, pos0, zero_state):
    if carry:
        prefix_ref, x_ref, x_next_ref, *refs = refs
    else:
        x_ref, *refs = refs
    if not zero_state:
        s0re_ref, s0im_ref, buf0_ref, *refs = refs
    (gain_ref, w_in_ref, b_gate_ref, a2_re_ref, a2_im_ref, wb_ref, wc_ref, wd_ref, d_ref,
     w_glu_ref, b_glu_ref, pmix_ref, pscale_ref, w_bs_ref, w_bp_ref, w_out_ref, fgain_ref,
     y_ref, ore_ref, oim_ref, obuf_ref,
     sre_ref, sim_ref, ext_ref, bu_ref, u2_ref, xn_ref, us_ref, zs_ref, gs_ref, zp_ref,
     gp_ref, ys_ref, yp_ref) = refs
    m = n_t * n_b
    n_pairs = n_t // 2
    half = n_pairs * n_b
    step = pl.program_id(0)
    hist_rows = POOL_BUF * n_b

    def load_state():
        if zero_state:
            sre_ref[...] = jnp.zeros(sre_ref.shape, F32)
            sim_ref[...] = jnp.zeros(sim_ref.shape, F32)
            ext_ref[0:hist_rows, :] = jnp.zeros((hist_rows, D_MODEL), F32)
        else:
            sre_ref[...] = s0re_ref[...]
            sim_ref[...] = s0im_ref[...]
            ext_ref[0:hist_rows, :] = buf0_ref[...].reshape(hist_rows, D_MODEL)

    def rows_tb(src_ref):
        return jnp.swapaxes(src_ref[...], 0, 1).reshape(m, D_MODEL)

    def proj(k):
        return jnp.dot(xn_ref[...], w_in_ref[:, k * D_MODEL:(k + 1) * D_MODEL],
                       preferred_element_type=F32)

    def prepare_norm(src_ref):
        xn_ref[...] = _rmsnorm(rows_tb(src_ref), gain_ref[...]).astype(BF16)

    def prepare_u():
        us_ref[...] = proj(0).reshape(n_pairs, 2 * n_b, D_MODEL)

    def prepare_pairs():
        for blk in range(N_BLOCKS):
            lanes = slice(blk * LANES, (blk + 1) * LANES)
            st = slice(blk * BLOCK_STATE, (blk + 1) * BLOCK_STATE)
            u_t0 = us_ref[:, 0:n_b, lanes].reshape(half, LANES)
            u_t1 = us_ref[:, n_b:, lanes].reshape(half, LANES)
            u2_ref[blk] = jnp.concatenate([u_t0, u_t1], axis=1).astype(BF16)
            bu_ref[blk, 0:n_b, 0:BLOCK_STATE] = sre_ref[:, st]
            bu_ref[blk, 0:n_b, BLOCK_STATE:] = sim_ref[:, st]

    def prepare_slabs():
        for blk in range(N_BLOCKS):
            bu_ref[blk, n_b:, :] = jnp.dot(u2_ref[blk], wb_ref[blk], preferred_element_type=F32)

    def prepare(src_ref):
        prepare_norm(src_ref)
        prepare_u()
        prepare_pairs()
        prepare_slabs()

    if carry:
        @pl.when(step == 0)
        def _():
            load_state()
            prepare(prefix_ref)
    else:
        load_state()
        prepare(x_ref)

    piece = 2 * LANES

    def proj_piece(k, p):
        cols = slice(p * piece, (p + 1) * piece)
        v = jnp.dot(xn_ref[...], w_in_ref[:, k * D_MODEL + p * piece:k * D_MODEL + (p + 1) * piece],
                    preferred_element_type=F32)
        if k == 1:
            zs_ref[:, cols] = _silu(v)
        elif k == 4:
            gs_ref[:, cols] = _sigmoid(v + b_gate_ref[:, cols])
        elif k == 2:
            ext_ref[hist_rows:hist_rows + m, cols] = v
        elif k == 3:
            zp_ref[:, cols] = _silu(v)
        else:
            gp_ref[:, cols] = _sigmoid(
                v + b_gate_ref[:, D_MODEL + p * piece:D_MODEL + (p + 1) * piece])

    pieces = [(k, p) for k in (1, 4, 2, 3, 5) for p in range(D_MODEL // piece)]

    for blk in range(N_BLOCKS):
        lanes = slice(blk * LANES, (blk + 1) * LANES)
        st = slice(blk * BLOCK_STATE, (blk + 1) * BLOCK_STATE)
        for k, p in pieces[blk * len(pieces) // N_BLOCKS:(blk + 1) * len(pieces) // N_BLOCKS]:
            proj_piece(k, p)
        a_re = jnp.broadcast_to(a2_re_ref[:, st], (SUBLANES, BLOCK_STATE))
        a_im = jnp.broadcast_to(a2_im_ref[:, st], (SUBLANES, BLOCK_STATE))
        for j in range(n_b // SUBLANES):
            tile = slice(j * SUBLANES, (j + 1) * SUBLANES)
            s_re = sre_ref[tile, st]
            s_im = sim_ref[tile, st]
            for k in range(n_pairs):
                rows = slice((k + 1) * n_b + j * SUBLANES, (k + 1) * n_b + (j + 1) * SUBLANES)
                n_re = a_re * s_re - a_im * s_im + bu_ref[blk, rows, 0:BLOCK_STATE]
                n_im = a_re * s_im + a_im * s_re + bu_ref[blk, rows, BLOCK_STATE:]
                bu_ref[blk, rows, 0:BLOCK_STATE] = n_re
                bu_ref[blk, rows, BLOCK_STATE:] = n_im
                s_re, s_im = n_re, n_im
            sre_ref[tile, st] = s_re
            sim_ref[tile, st] = s_im
        y2 = (jnp.dot(bu_ref[blk, 0:half, :].astype(BF16), wc_ref[blk], preferred_element_type=F32)
              + jnp.dot(u2_ref[blk], wd_ref[blk], preferred_element_type=F32))
        ys_ref[:, 0:n_b, lanes] = y2[:, 0:LANES].reshape(n_pairs, n_b, LANES)
        ys_ref[:, n_b:, lanes] = y2[:, LANES:].reshape(n_pairs, n_b, LANES)

    if pos0 + 1 >= max(POOL_WINDOWS):
        pos = None
    else:
        t_loc = lax.broadcasted_iota(jnp.int32, (m, 1), 0) // n_b
        pos = pos0 + step * n_t + t_loc
    for gi, w in enumerate(POOL_WINDOWS):
        cols = slice(gi * POOL_GROUP_CH, (gi + 1) * POOL_GROUP_CH)
        s = ext_ref[(POOL_BUF + 1 - w) * n_b:hist_rows + m, cols]
        span = 1
        while span < w:
            keep = s.shape[0] - span * n_b
            s = s[:keep] + s[span * n_b:]
            span *= 2
        if pos is None:
            mean = s * (1.0 / w)
        else:
            cnt = jnp.maximum(jnp.minimum(pos + 1, w), 1)
            mean = s * (1.0 / cnt.astype(F32))
        pooled = mean - ext_ref[hist_rows:hist_rows + m, cols]
        yp_ref[:, cols] = jnp.dot(pooled.astype(BF16), pmix_ref[gi],
                                  preferred_element_type=F32) * pscale_ref[:, cols]
    for k in range(POOL_BUF):
        ext_ref[k * n_b:(k + 1) * n_b, :] = ext_ref[(k + n_t) * n_b:(k + n_t + 1) * n_b, :]

    y = ys_ref[...].reshape(m, D_MODEL) + d_ref[...] * us_ref[...].reshape(m, D_MODEL)
    y = jax.nn.gelu(y)
    b = yp_ref[...] * zp_ref[...]
    acc_p = jnp.dot(b.astype(BF16), w_bp_ref[...], preferred_element_type=F32)
    glu = jnp.dot(y.astype(BF16), w_glu_ref[...], preferred_element_type=F32)
    if carry:
        prepare_norm(x_next_ref)
    a = y * _sigmoid(glu + b_glu_ref[...]) * zs_ref[...]
    acc_s = jnp.dot(a.astype(BF16), w_bs_ref[...], preferred_element_type=F32)
    if carry:
        prepare_u()
        prepare_pairs()
    merged = gs_ref[...] * acc_s + gp_ref[...] * acc_p
    delta = jnp.dot(merged.astype(BF16), w_out_ref[...], preferred_element_type=F32)
    if carry:
        prepare_slabs()
    y_rows = _rmsnorm(rows_tb(x_ref) + delta, fgain_ref[...])
    y_ref[...] = jnp.swapaxes(y_rows.reshape(n_t, n_b, D_MODEL), 0, 1)

    def write_state():
        ore_ref[...] = sre_ref[...]
        oim_ref[...] = sim_ref[...]
        obuf_ref[...] = ext_ref[0:hist_rows, :].reshape(POOL_BUF, n_b, D_MODEL)

    if carry:
        pl.when(step == pl.num_programs(0) - 1)(write_state)
    else:
        write_state()


def _layer_call(x, state, weights, *, n_t, n_b, carry, pos0, name, prefix=None):
    batch, time, _ = x.shape
    m = n_t * n_b
    x_block = (n_b, n_t, D_MODEL)
    buf_block = (POOL_BUF, n_b, D_MODEL)
    if carry:
        assert batch == n_b and time % n_t == 0 and prefix.shape == x_block
        n_blocks = time // n_t
        n_steps = n_blocks + 1
        x_specs = [pl.BlockSpec(x_block, lambda i: (0, 0, 0)),
                   pl.BlockSpec(x_block, lambda i: (0, jnp.maximum(i - 1, 0), 0)),
                   pl.BlockSpec(x_block, lambda i: (0, jnp.minimum(i, n_blocks - 1), 0))]
        y_map = lambda i: (0, jnp.maximum(i - 1, 0), 0)
        per_step = lambda i: (0, 0)
        per_step3 = lambda i: (0, 0, 0)
        xs = (prefix, x, x)
    else:
        assert time == n_t and batch % n_b == 0
        n_steps = batch // n_b
        x_specs = [pl.BlockSpec(x_block, lambda i: (i, 0, 0))]
        y_map = lambda i: (i, 0, 0)
        per_step = lambda i: (i, 0)
        per_step3 = lambda i: (0, i, 0)
        xs = (x,)
    whole = pl.BlockSpec(memory_space=pltpu.VMEM)
    state_specs = [
        pl.BlockSpec((n_b, STATE_W), per_step),
        pl.BlockSpec((n_b, STATE_W), per_step),
        pl.BlockSpec(buf_block, per_step3),
    ]
    kernel = functools.partial(_layer_kernel, n_t=n_t, n_b=n_b, carry=carry, pos0=pos0,
                               zero_state=state is None)
    return pl.pallas_call(
        kernel,
        grid=(n_steps,),
        in_specs=x_specs + ([] if state is None else state_specs) + [whole] * len(weights),
        out_specs=[pl.BlockSpec(x_block, y_map)] + state_specs,
        out_shape=[
            jax.ShapeDtypeStruct((batch, time, D_MODEL), F32),
            jax.ShapeDtypeStruct((batch, STATE_W), F32),
            jax.ShapeDtypeStruct((batch, STATE_W), F32),
            jax.ShapeDtypeStruct((POOL_BUF, batch, D_MODEL), F32),
        ],
        scratch_shapes=[
            pltpu.VMEM((n_b, STATE_W), F32),
            pltpu.VMEM((n_b, STATE_W), F32),
            pltpu.VMEM(((POOL_BUF + n_t) * n_b, D_MODEL), F32),
            pltpu.VMEM((N_BLOCKS, m // 2 + n_b, 2 * BLOCK_STATE), F32),
            pltpu.VMEM((N_BLOCKS, m // 2, 2 * LANES), BF16),
            pltpu.VMEM((m, D_MODEL), BF16),
            pltpu.VMEM((n_t // 2, 2 * n_b, D_MODEL), F32),
            pltpu.VMEM((m, D_MODEL), F32),
            pltpu.VMEM((m, D_MODEL), F32),
            pltpu.VMEM((m, D_MODEL), F32),
            pltpu.VMEM((m, D_MODEL), F32),
            pltpu.VMEM((n_t // 2, 2 * n_b, D_MODEL), F32),
            pltpu.VMEM((m, D_MODEL), F32),
        ],
        compiler_params=pltpu.CompilerParams(
            dimension_semantics=("arbitrary",), vmem_limit_bytes=VMEM_LIMIT_BYTES),
        name=name,
    )(*xs, *(() if state is None else state), *weights)


PREP_BLOCKS_PER_STEP = 4


def _block_diag_kernel(*refs, layouts):
    gpb = GROUPS_PER_BLOCK
    refs = list(refs)
    out_refs = refs[len(refs) - len(layouts):]
    for dst_ref, (grid, rows, cols, transpose) in zip(out_refs, layouts):
        height, width = gpb * rows, gpb * cols
        assert rows & (rows - 1) == 0 and cols & (cols - 1) == 0
        k = lax.broadcasted_iota(jnp.int32, (cols, width), 0)
        n = lax.broadcasted_iota(jnp.int32, (cols, width), 1)
        sel = jnp.where(jnp.bitwise_and(n, cols - 1) == k, 1.0, 0.0).astype(BF16)
        r = jnp.right_shift(lax.broadcasted_iota(jnp.int32, (height, width), 0),
                            rows.bit_length() - 1)
        c = jnp.right_shift(lax.broadcasted_iota(jnp.int32, (height, width), 1),
                            cols.bit_length() - 1)
        for ri, row_parts in enumerate(grid):
            for ci, part in enumerate(row_parts):
                src_ref = None if part is None else refs.pop(0)
                for blk in range(PREP_BLOCKS_PER_STEP):
                    if src_ref is None:
                        cell = jnp.zeros((height, width), F32)
                    else:
                        tiled = jnp.dot(src_ref[blk].astype(BF16), sel,
                                        preferred_element_type=F32)
                        cell = jnp.where(r == c, tiled, 0.0)
                    if transpose:
                        cell = cell.T
                    h, w = cell.shape
                    dst_ref[blk, ri * h:(ri + 1) * h, ci * w:(ci + 1) * w] = cell.astype(BF16)


def _block_diag_call(layouts):
    gpb = GROUPS_PER_BLOCK
    per_step = PREP_BLOCKS_PER_STEP
    ins, in_specs, out_specs, out_shapes, kernel_layouts = [], [], [], [], []
    for grid, rows, cols, transpose in layouts:
        for part in (p for row_parts in grid for p in row_parts if p is not None):
            ins.append(part.reshape(N_BLOCKS, gpb * rows, cols))
            in_specs.append(pl.BlockSpec((per_step, gpb * rows, cols), lambda i: (i, 0, 0)))
        cell = (gpb * cols, gpb * rows) if transpose else (gpb * rows, gpb * cols)
        shape = (len(grid) * cell[0], len(grid[0]) * cell[1])
        out_specs.append(pl.BlockSpec((per_step,) + shape, lambda i: (i, 0, 0)))
        out_shapes.append(jax.ShapeDtypeStruct((N_BLOCKS,) + shape, BF16))
        kernel_layouts.append(([[p is not None or None for p in row] for row in grid],
                               rows, cols, transpose))
    return pl.pallas_call(
        functools.partial(_block_diag_kernel, layouts=kernel_layouts),
        grid=(N_BLOCKS // per_step,), in_specs=in_specs, out_specs=out_specs,
        out_shape=out_shapes,
        name="s5_block_diag_weights",
    )(*ins)


def _ssm_weights(a_re, a_im, log_dt, b_re, b_im, c_re, c_im):
    n_c, n_p = SSM_GROUP_CH, SSM_STATE
    dt = jnp.exp(log_dt)[:, None]
    mag = jnp.exp(dt * a_re)
    ang = dt * a_im
    abar_re = mag * jnp.cos(ang)
    abar_im = mag * jnp.sin(ang)
    den = a_re * a_re + a_im * a_im
    nr = abar_re - 1.0
    ni = abar_im
    q_re = ((nr * a_re + ni * a_im) / den)[:, None, :]
    q_im = ((ni * a_re - nr * a_im) / den)[:, None, :]
    a2_re = abar_re * abar_re - abar_im * abar_im
    a2_im = 2.0 * abar_re * abar_im

    bt_re = jnp.swapaxes(b_re, 1, 2)
    bt_im = jnp.swapaxes(b_im, 1, 2)
    bbar_re = q_re * bt_re - q_im * bt_im
    bbar_im = q_re * bt_im + q_im * bt_re
    ar, ai = abar_re[:, None, :], abar_im[:, None, :]
    ab_re = ar * bbar_re - ai * bbar_im
    ab_im = ar * bbar_im + ai * bbar_re

    ca_re = c_re * ar - c_im * ai
    ca_im = c_re * ai + c_im * ar
    ar2, ai2 = a2_re[:, None, :], a2_im[:, None, :]
    ca2_re = c_re * ar2 - c_im * ai2
    ca2_im = c_re * ai2 + c_im * ar2

    lhs = jnp.concatenate([bbar_re, bbar_im], axis=-1)
    rhs = jnp.concatenate([jnp.concatenate([c_re, ca_re], axis=1),
                           jnp.concatenate([-c_im, -ca_im], axis=1)], axis=-1)
    direct = jnp.einsum('gip,gop->gio', lhs, rhs, precision=lax.Precision.HIGHEST)
    cb, cab = direct[..., :n_c], direct[..., n_c:]

    wb, wc, wd = _block_diag_call((
        ([[ab_re, ab_im], [bbar_re, bbar_im]], n_c, n_p, False),
        ([[ca_re, ca2_re], [-ca_im, -ca2_im]], n_c, n_p, True),
        ([[cb, cab], [None, cb]], n_c, n_c, False)))
    return a2_re, a2_im, wb, wc, wd


PROMPT_T = 32
SAMPLE_B = 32


def kernel(x_prompt, x_sample, state_ssm_re, state_ssm_im, state_pool, meta_tokens, norm_gain, w_in, b_gate, ssm_a_re, ssm_a_im, ssm_log_dt, ssm_b_re, ssm_b_im, ssm_c_re, ssm_c_im, ssm_d, w_glu, b_glu, pool_mix, pool_scale, w_branch_ssm, w_branch_pool, w_out, final_norm_gain):
    batch, seq, _ = x_prompt.shape
    dec_batch, dec_seq, _ = x_sample.shape
    depth = norm_gain.shape[0]
    assert depth == 1
    l = 0
    a2_re, a2_im, wb, wc, wd = _ssm_weights(
        ssm_a_re[l], ssm_a_im[l], ssm_log_dt[l], ssm_b_re[l], ssm_b_im[l],
        ssm_c_re[l], ssm_c_im[l])
    row = lambda v: v.reshape(1, -1).astype(F32)
    weights = (
        row(norm_gain[l]), w_in[l].astype(BF16), row(b_gate[l]),
        row(a2_re), row(a2_im), wb, wc, wd, row(ssm_d[l]),
        w_glu[l].astype(BF16), row(b_glu[l]), pool_mix[l].astype(BF16), row(pool_scale[l]),
        w_branch_ssm[l].astype(BF16), w_branch_pool[l].astype(BF16), w_out[l].astype(BF16),
        row(final_norm_gain),
    )

    assert batch == SUBLANES and N_META <= PROMPT_T
    meta = jnp.broadcast_to(meta_tokens[None].astype(x_prompt.dtype), (batch, N_META, D_MODEL))
    prefix = jnp.concatenate(
        [jnp.zeros((batch, PROMPT_T - N_META, D_MODEL), x_prompt.dtype), meta], axis=1)
    y_prompt, p_re, p_im, p_buf = _layer_call(
        x_prompt, None, weights, prefix=prefix,
        n_t=PROMPT_T, n_b=batch, carry=True, pos0=N_META - PROMPT_T, name="prompt_layer")

    sample_state = (state_ssm_re[l].reshape(dec_batch, STATE_W),
                    state_ssm_im[l].reshape(dec_batch, STATE_W),
                    jnp.swapaxes(state_pool[l], 0, 1))
    y_sample, s_re, s_im, s_buf = _layer_call(
        x_sample, sample_state, weights,
        n_t=dec_seq, n_b=SAMPLE_B, carry=False, pos0=PAST_LEN, name="sample_layer")

    state_shape = (1, -1, SSM_GROUPS, SSM_STATE)
    return (y_prompt, y_sample,
            p_re.reshape(state_shape), p_im.reshape(state_shape), jnp.swapaxes(p_buf, 0, 1)[None],
            s_re.reshape(state_shape), s_im.reshape(state_shape), jnp.swapaxes(s_buf, 0, 1)[None])
```

```python
import functools

import jax
import jax.numpy as jnp
from jax import lax
from jax.experimental import pallas as pl
from jax.experimental.pallas import tpu as pltpu

D_MODEL = 1024
N_META = 16
SSM_GROUP_CH = 16
SSM_GROUPS = D_MODEL // SSM_GROUP_CH
SSM_STATE = 64
POOL_WINDOWS = (2, 4, 8, 16)
POOL_GROUP_CH = D_MODEL // len(POOL_WINDOWS)
POOL_BUF = max(POOL_WINDOWS) - 1
EPS = 1e-6
PAST_LEN = 16384

SUBLANES = 8
LANES = 128
GROUPS_PER_BLOCK = LANES // SSM_GROUP_CH
N_BLOCKS = SSM_GROUPS // GROUPS_PER_BLOCK
BLOCK_STATE = GROUPS_PER_BLOCK * SSM_STATE
STATE_W = SSM_GROUPS * SSM_STATE
VMEM_LIMIT_BYTES = 56 * 1024 * 1024

F32 = jnp.float32
BF16 = jnp.bfloat16


def _rmsnorm(x, gain):
    ms = jnp.mean(x * x, axis=-1, keepdims=True)
    return (x * lax.rsqrt(ms + EPS)) * gain


_sigmoid = jax.nn.sigmoid


def _silu(x):
    return x * _sigmoid(x)


def _layer_kernel(*refs, n_t, n_b, carry, pos0, zero_state):
    if carry:
        prefix_ref, x_ref, x_next_ref, *refs = refs
    else:
        x_ref, *refs = refs
    if not zero_state:
        s0re_ref, s0im_ref, buf0_ref, *refs = refs
    (gain_ref, w_in_ref, b_gate_ref, a2_re_ref, a2_im_ref, wb_ref, wc_ref, wd_ref, d_ref,
     w_glu_ref, b_glu_ref, pmix_ref, pscale_ref, w_bs_ref, w_bp_ref, w_out_ref, fgain_ref,
     y_ref, ore_ref, oim_ref, obuf_ref,
     sre_ref, sim_ref, ext_ref, bu_ref, u2_ref, xn_ref, us_ref, zs_ref, gs_ref, zp_ref,
     gp_ref, ys_ref, yp_ref) = refs
    m = n_t * n_b
    n_pairs = n_t // 2
    half = n_pairs * n_b
    step = pl.program_id(0)
    hist_rows = POOL_BUF * n_b

    def load_state():
        if zero_state:
            sre_ref[...] = jnp.zeros(sre_ref.shape, F32)
            sim_ref[...] = jnp.zeros(sim_ref.shape, F32)
            ext_ref[0:hist_rows, :] = jnp.zeros((hist_rows, D_MODEL), F32)
        else:
            sre_ref[...] = s0re_ref[...]
            sim_ref[...] = s0im_ref[...]
            ext_ref[0:hist_rows, :] = buf0_ref[...].reshape(hist_rows, D_MODEL)

    def rows_tb(src_ref):
        return jnp.swapaxes(src_ref[...], 0, 1).reshape(m, D_MODEL)

    def proj(k):
        return jnp.dot(xn_ref[...], w_in_ref[:, k * D_MODEL:(k + 1) * D_MODEL],
                       preferred_element_type=F32)

    def prepare_norm(src_ref):
        xn_ref[...] = _rmsnorm(rows_tb(src_ref), gain_ref[...]).astype(BF16)

    def prepare_u():
        us_ref[...] = proj(0).reshape(n_pairs, 2 * n_b, D_MODEL)

    def prepare_pairs():
        for blk in range(N_BLOCKS):
            lanes = slice(blk * LANES, (blk + 1) * LANES)
            st = slice(blk * BLOCK_STATE, (blk + 1) * BLOCK_STATE)
            u_t0 = us_ref[:, 0:n_b, lanes].reshape(half, LANES)
            u_t1 = us_ref[:, n_b:, lanes].reshape(half, LANES)
            u2_ref[blk] = jnp.concatenate([u_t0, u_t1], axis=1).astype(BF16)
            bu_ref[blk, 0:n_b, 0:BLOCK_STATE] = sre_ref[:, st]
            bu_ref[blk, 0:n_b, BLOCK_STATE:] = sim_ref[:, st]

    def prepare_slabs():
        for blk in range(N_BLOCKS):
            bu_ref[blk, n_b:, :] = jnp.dot(u2_ref[blk], wb_ref[blk], preferred_element_type=F32)

    def prepare(src_ref):
        prepare_norm(src_ref)
        prepare_u()
        prepare_pairs()
        prepare_slabs()

    if carry:
        @pl.when(step == 0)
        def _():
            load_state()
            prepare(prefix_ref)
    else:
        load_state()
        prepare(x_ref)

    piece = 2 * LANES

    def proj_piece(k, p):
        cols = slice(p * piece, (p + 1) * piece)
        v = jnp.dot(xn_ref[...], w_in_ref[:, k * D_MODEL + p * piece:k * D_MODEL + (p + 1) * piece],
                    preferred_element_type=F32)
        if k == 1:
            zs_ref[:, cols] = _silu(v)
        elif k == 4:
            gs_ref[:, cols] = _sigmoid(v + b_gate_ref[:, cols])
        elif k == 2:
            ext_ref[hist_rows:hist_rows + m, cols] = v
        elif k == 3:
            zp_ref[:, cols] = _silu(v)
        else:
            gp_ref[:, cols] = _sigmoid(
                v + b_gate_ref[:, D_MODEL + p * piece:D_MODEL + (p + 1) * piece])

    pieces = [(k, p) for k in (1, 4, 2, 3, 5) for p in range(D_MODEL // piece)]

    for blk in range(N_BLOCKS):
        lanes = slice(blk * LANES, (blk + 1) * LANES)
        st = slice(blk * BLOCK_STATE, (blk + 1) * BLOCK_STATE)
        for k, p in pieces[blk * len(pieces) // N_BLOCKS:(blk + 1) * len(pieces) // N_BLOCKS]:
            proj_piece(k, p)
        a_re = jnp.broadcast_to(a2_re_ref[:, st], (SUBLANES, BLOCK_STATE))
        a_im = jnp.broadcast_to(a2_im_ref[:, st], (SUBLANES, BLOCK_STATE))
        for j in range(n_b // SUBLANES):
            tile = slice(j * SUBLANES, (j + 1) * SUBLANES)
            s_re = sre_ref[tile, st]
            s_im = sim_ref[tile, st]
            for k in range(n_pairs):
                rows = slice((k + 1) * n_b + j * SUBLANES, (k + 1) * n_b + (j + 1) * SUBLANES)
                n_re = a_re * s_re - a_im * s_im + bu_ref[blk, rows, 0:BLOCK_STATE]
                n_im = a_re * s_im + a_im * s_re + bu_ref[blk, rows, BLOCK_STATE:]
                bu_ref[blk, rows, 0:BLOCK_STATE] = n_re
                bu_ref[blk, rows, BLOCK_STATE:] = n_im
                s_re, s_im = n_re, n_im
            sre_ref[tile, st] = s_re
            sim_ref[tile, st] = s_im
        y2 = (jnp.dot(bu_ref[blk, 0:half, :].astype(BF16), wc_ref[blk], preferred_element_type=F32)
              + jnp.dot(u2_ref[blk], wd_ref[blk], preferred_element_type=F32))
        ys_ref[:, 0:n_b, lanes] = y2[:, 0:LANES].reshape(n_pairs, n_b, LANES)
        ys_ref[:, n_b:, lanes] = y2[:, LANES:].reshape(n_pairs, n_b, LANES)

    if pos0 + 1 >= max(POOL_WINDOWS):
        pos = None
    else:
        t_loc = lax.broadcasted_iota(jnp.int32, (m, 1), 0) // n_b
        pos = pos0 + step * n_t + t_loc
    for gi, w in enumerate(POOL_WINDOWS):
        cols = slice(gi * POOL_GROUP_CH, (gi + 1) * POOL_GROUP_CH)
        s = ext_ref[(POOL_BUF + 1 - w) * n_b:hist_rows + m, cols]
        span = 1
        while span < w:
            keep = s.shape[0] - span * n_b
            s = s[:keep] + s[span * n_b:]
            span *= 2
        if pos is None:
            mean = s * (1.0 / w)
        else:
            cnt = jnp.maximum(jnp.minimum(pos + 1, w), 1)
            mean = s * (1.0 / cnt.astype(F32))
        pooled = mean - ext_ref[hist_rows:hist_rows + m, cols]
        yp_ref[:, cols] = jnp.dot(pooled.astype(BF16), pmix_ref[gi],
                                  preferred_element_type=F32) * pscale_ref[:, cols]
    for k in range(POOL_BUF):
        ext_ref[k * n_b:(k + 1) * n_b, :] = ext_ref[(k + n_t) * n_b:(k + n_t + 1) * n_b, :]

    y = ys_ref[...].reshape(m, D_MODEL) + d_ref[...] * us_ref[...].reshape(m, D_MODEL)
    y = jax.nn.gelu(y)
    b = yp_ref[...] * zp_ref[...]
    acc_p = jnp.dot(b.astype(BF16), w_bp_ref[...], preferred_element_type=F32)
    glu = jnp.dot(y.astype(BF16), w_glu_ref[...], preferred_element_type=F32)
    if carry:
        prepare_norm(x_next_ref)
    a = y * _sigmoid(glu + b_glu_ref[...]) * zs_ref[...]
    acc_s = jnp.dot(a.astype(BF16), w_bs_ref[...], preferred_element_type=F32)
    if carry:
        prepare_u()
        prepare_pairs()
    merged = gs_ref[...] * acc_s + gp_ref[...] * acc_p
    delta = jnp.dot(merged.astype(BF16), w_out_ref[...], preferred_element_type=F32)
    if carry:
        prepare_slabs()
    delta_bt = jnp.swapaxes(delta.reshape(n_t, n_b, D_MODEL), 0, 1)
    y_ref[...] = _rmsnorm(x_ref[...] + delta_bt, fgain_ref[...])

    def write_state():
        ore_ref[...] = sre_ref[...]
        oim_ref[...] = sim_ref[...]
        obuf_ref[...] = ext_ref[0:hist_rows, :].reshape(POOL_BUF, n_b, D_MODEL)

    if carry:
        pl.when(step == pl.num_programs(0) - 1)(write_state)
    else:
        write_state()


def _layer_call(x, state, weights, *, n_t, n_b, carry, pos0, name, prefix=None):
    batch, time, _ = x.shape
    m = n_t * n_b
    x_block = (n_b, n_t, D_MODEL)
    buf_block = (POOL_BUF, n_b, D_MODEL)
    if carry:
        assert batch == n_b and time % n_t == 0 and prefix.shape == x_block
        n_blocks = time // n_t
        n_steps = n_blocks + 1
        x_specs = [pl.BlockSpec(x_block, lambda i: (0, 0, 0)),
                   pl.BlockSpec(x_block, lambda i: (0, jnp.maximum(i - 1, 0), 0)),
                   pl.BlockSpec(x_block, lambda i: (0, jnp.minimum(i, n_blocks - 1), 0))]
        y_map = lambda i: (0, jnp.maximum(i - 1, 0), 0)
        per_step = lambda i: (0, 0)
        per_step3 = lambda i: (0, 0, 0)
        xs = (prefix, x, x)
    else:
        assert time == n_t and batch % n_b == 0
        n_steps = batch // n_b
        x_specs = [pl.BlockSpec(x_block, lambda i: (i, 0, 0))]
        y_map = lambda i: (i, 0, 0)
        per_step = lambda i: (i, 0)
        per_step3 = lambda i: (0, i, 0)
        xs = (x,)
    whole = pl.BlockSpec(memory_space=pltpu.VMEM)
    state_specs = [
        pl.BlockSpec((n_b, STATE_W), per_step),
        pl.BlockSpec((n_b, STATE_W), per_step),
        pl.BlockSpec(buf_block, per_step3),
    ]
    kernel = functools.partial(_layer_kernel, n_t=n_t, n_b=n_b, carry=carry, pos0=pos0,
                               zero_state=state is None)
    return pl.pallas_call(
        kernel,
        grid=(n_steps,),
        in_specs=x_specs + ([] if state is None else state_specs) + [whole] * len(weights),
        out_specs=[pl.BlockSpec(x_block, y_map)] + state_specs,
        out_shape=[
            jax.ShapeDtypeStruct((batch, time, D_MODEL), F32),
            jax.ShapeDtypeStruct((batch, STATE_W), F32),
            jax.ShapeDtypeStruct((batch, STATE_W), F32),
            jax.ShapeDtypeStruct((POOL_BUF, batch, D_MODEL), F32),
        ],
        scratch_shapes=[
            pltpu.VMEM((n_b, STATE_W), F32),
            pltpu.VMEM((n_b, STATE_W), F32),
            pltpu.VMEM(((POOL_BUF + n_t) * n_b, D_MODEL), F32),
            pltpu.VMEM((N_BLOCKS, m // 2 + n_b, 2 * BLOCK_STATE), F32),
            pltpu.VMEM((N_BLOCKS, m // 2, 2 * LANES), BF16),
            pltpu.VMEM((m, D_MODEL), BF16),
            pltpu.VMEM((n_t // 2, 2 * n_b, D_MODEL), F32),
            pltpu.VMEM((m, D_MODEL), F32),
            pltpu.VMEM((m, D_MODEL), F32),
            pltpu.VMEM((m, D_MODEL), F32),
            pltpu.VMEM((m, D_MODEL), F32),
            pltpu.VMEM((n_t // 2, 2 * n_b, D_MODEL), F32),
            pltpu.VMEM((m, D_MODEL), F32),
        ],
        compiler_params=pltpu.CompilerParams(
            dimension_semantics=("arbitrary",), vmem_limit_bytes=VMEM_LIMIT_BYTES),
        name=name,
    )(*xs, *(() if state is None else state), *weights)


PREP_BLOCKS_PER_STEP = 4


def _block_diag_kernel(*refs, layouts):
    gpb = GROUPS_PER_BLOCK
    refs = list(refs)
    out_refs = refs[len(refs) - len(layouts):]
    for dst_ref, (grid, rows, cols, transpose) in zip(out_refs, layouts):
        height, width = gpb * rows, gpb * cols
        assert rows & (rows - 1) == 0 and cols & (cols - 1) == 0
        k = lax.broadcasted_iota(jnp.int32, (cols, width), 0)
        n = lax.broadcasted_iota(jnp.int32, (cols, width), 1)
        sel = jnp.where(jnp.bitwise_and(n, cols - 1) == k, 1.0, 0.0).astype(BF16)
        r = jnp.right_shift(lax.broadcasted_iota(jnp.int32, (height, width), 0),
                            rows.bit_length() - 1)
        c = jnp.right_shift(lax.broadcasted_iota(jnp.int32, (height, width), 1),
                            cols.bit_length() - 1)
        for ri, row_parts in enumerate(grid):
            for ci, part in enumerate(row_parts):
                src_ref = None if part is None else refs.pop(0)
                for blk in range(PREP_BLOCKS_PER_STEP):
                    if src_ref is None:
                        cell = jnp.zeros((height, width), F32)
                    else:
                        tiled = jnp.dot(src_ref[blk].astype(BF16), sel,
                                        preferred_element_type=F32)
                        cell = jnp.where(r == c, tiled, 0.0)
                    if transpose:
                        cell = cell.T
                    h, w = cell.shape
                    dst_ref[blk, ri * h:(ri + 1) * h, ci * w:(ci + 1) * w] = cell.astype(BF16)


def _block_diag_call(layouts):
    gpb = GROUPS_PER_BLOCK
    per_step = PREP_BLOCKS_PER_STEP
    ins, in_specs, out_specs, out_shapes, kernel_layouts = [], [], [], [], []
    for grid, rows, cols, transpose in layouts:
        for part in (p for row_parts in grid for p in row_parts if p is not None):
            ins.append(part.reshape(N_BLOCKS, gpb * rows, cols))
            in_specs.append(pl.BlockSpec((per_step, gpb * rows, cols), lambda i: (i, 0, 0)))
        cell = (gpb * cols, gpb * rows) if transpose else (gpb * rows, gpb * cols)
        shape = (len(grid) * cell[0], len(grid[0]) * cell[1])
        out_specs.append(pl.BlockSpec((per_step,) + shape, lambda i: (i, 0, 0)))
        out_shapes.append(jax.ShapeDtypeStruct((N_BLOCKS,) + shape, BF16))
        kernel_layouts.append(([[p is not None or None for p in row] for row in grid],
                               rows, cols, transpose))
    return pl.pallas_call(
        functools.partial(_block_diag_kernel, layouts=kernel_layouts),
        grid=(N_BLOCKS // per_step,), in_specs=in_specs, out_specs=out_specs,
        out_shape=out_shapes,
        name="s5_block_diag_weights",
    )(*ins)


def _ssm_weights(a_re, a_im, log_dt, b_re, b_im, c_re, c_im):
    n_c, n_p = SSM_GROUP_CH, SSM_STATE
    dt = jnp.exp(log_dt)[:, None]
    mag = jnp.exp(dt * a_re)
    ang = dt * a_im
    abar_re = mag * jnp.cos(ang)
    abar_im = mag * jnp.sin(ang)
    den = a_re * a_re + a_im * a_im
    nr = abar_re - 1.0
    ni = abar_im
    q_re = ((nr * a_re + ni * a_im) / den)[:, None, :]
    q_im = ((ni * a_re - nr * a_im) / den)[:, None, :]
    a2_re = abar_re * abar_re - abar_im * abar_im
    a2_im = 2.0 * abar_re * abar_im

    bt_re = jnp.swapaxes(b_re, 1, 2)
    bt_im = jnp.swapaxes(b_im, 1, 2)
    bbar_re = q_re * bt_re - q_im * bt_im
    bbar_im = q_re * bt_im + q_im * bt_re
    ar, ai = abar_re[:, None, :], abar_im[:, None, :]
    ab_re = ar * bbar_re - ai * bbar_im
    ab_im = ar * bbar_im + ai * bbar_re

    ca_re = c_re * ar - c_im * ai
    ca_im = c_re * ai + c_im * ar
    ar2, ai2 = a2_re[:, None, :], a2_im[:, None, :]
    ca2_re = c_re * ar2 - c_im * ai2
    ca2_im = c_re * ai2 + c_im * ar2

    lhs = jnp.concatenate([bbar_re, bbar_im], axis=-1)
    rhs = jnp.concatenate([jnp.concatenate([c_re, ca_re], axis=1),
                           jnp.concatenate([-c_im, -ca_im], axis=1)], axis=-1)
    direct = jnp.einsum('gip,gop->gio', lhs, rhs, precision=lax.Precision.HIGHEST)
    cb, cab = direct[..., :n_c], direct[..., n_c:]

    wb, wc, wd = _block_diag_call((
        ([[ab_re, ab_im], [bbar_re, bbar_im]], n_c, n_p, False),
        ([[ca_re, ca2_re], [-ca_im, -ca2_im]], n_c, n_p, True),
        ([[cb, cab], [None, cb]], n_c, n_c, False)))
    return a2_re, a2_im, wb, wc, wd


PROMPT_T = 32
SAMPLE_B = 32


def kernel(x_prompt, x_sample, state_ssm_re, state_ssm_im, state_pool, meta_tokens, norm_gain, w_in, b_gate, ssm_a_re, ssm_a_im, ssm_log_dt, ssm_b_re, ssm_b_im, ssm_c_re, ssm_c_im, ssm_d, w_glu, b_glu, pool_mix, pool_scale, w_branch_ssm, w_branch_pool, w_out, final_norm_gain):
    batch, seq, _ = x_prompt.shape
    dec_batch, dec_seq, _ = x_sample.shape
    depth = norm_gain.shape[0]
    assert depth == 1
    l = 0
    a2_re, a2_im, wb, wc, wd = _ssm_weights(
        ssm_a_re[l], ssm_a_im[l], ssm_log_dt[l], ssm_b_re[l], ssm_b_im[l],
        ssm_c_re[l], ssm_c_im[l])
    row = lambda v: v.reshape(1, -1).astype(F32)
    weights = (
        row(norm_gain[l]), w_in[l].astype(BF16), row(b_gate[l]),
        row(a2_re), row(a2_im), wb, wc, wd, row(ssm_d[l]),
        w_glu[l].astype(BF16), row(b_glu[l]), pool_mix[l].astype(BF16), row(pool_scale[l]),
        w_branch_ssm[l].astype(BF16), w_branch_pool[l].astype(BF16), w_out[l].astype(BF16),
        row(final_norm_gain),
    )

    assert batch == SUBLANES and N_META <= PROMPT_T
    meta = jnp.broadcast_to(meta_tokens[None].astype(x_prompt.dtype), (batch, N_META, D_MODEL))
    prefix = jnp.concatenate(
        [jnp.zeros((batch, PROMPT_T - N_META, D_MODEL), x_prompt.dtype), meta], axis=1)
    y_prompt, p_re, p_im, p_buf = _layer_call(
        x_prompt, None, weights, prefix=prefix,
        n_t=PROMPT_T, n_b=batch, carry=True, pos0=N_META - PROMPT_T, name="prompt_layer")

    sample_state = (state_ssm_re[l].reshape(dec_batch, STATE_W),
                    state_ssm_im[l].reshape(dec_batch, STATE_W),
                    jnp.swapaxes(state_pool[l], 0, 1))
    y_sample, s_re, s_im, s_buf = _layer_call(
        x_sample, sample_state, weights,
        n_t=dec_seq, n_b=SAMPLE_B, carry=False, pos0=PAST_LEN, name="sample_layer")

    state_shape = (1, -1, SSM_GROUPS, SSM_STATE)
    return (y_prompt, y_sample,
            p_re.reshape(state_shape), p_im.reshape(state_shape), jnp.swapaxes(p_buf, 0, 1)[None],
            s_re.reshape(state_shape), s_im.reshape(state_shape), jnp.swapaxes(s_buf, 0, 1)[None])
```

```python
import functools

import jax
import jax.numpy as jnp
from jax import lax
from jax.experimental import pallas as pl
from jax.experimental.pallas import tpu as pltpu

D_MODEL = 1024
N_META = 16
SSM_GROUP_CH = 16
SSM_GROUPS = D_MODEL // SSM_GROUP_CH
SSM_STATE = 64
POOL_WINDOWS = (2, 4, 8, 16)
POOL_GROUP_CH = D_MODEL // len(POOL_WINDOWS)
POOL_BUF = max(POOL_WINDOWS) - 1
EPS = 1e-6
PAST_LEN = 16384

SUBLANES = 8
LANES = 128
GROUPS_PER_BLOCK = LANES // SSM_GROUP_CH
N_BLOCKS = SSM_GROUPS // GROUPS_PER_BLOCK
BLOCK_STATE = GROUPS_PER_BLOCK * SSM_STATE
STATE_W = SSM_GROUPS * SSM_STATE
VMEM_LIMIT_BYTES = 56 * 1024 * 1024

F32 = jnp.float32
BF16 = jnp.bfloat16


def _rmsnorm(x, gain):
    ms = jnp.mean(x * x, axis=-1, keepdims=True)
    return (x * lax.rsqrt(ms + EPS)) * gain


_sigmoid = jax.nn.sigmoid


def _silu(x):
    return x * _sigmoid(x)


def _layer_kernel(*refs, n_t, n_b, carry, pos0, zero_state):
    if carry:
        prefix_ref, x_ref, x_next_ref, *refs = refs
    else:
        x_ref, *refs = refs
    if not zero_state:
        s0re_ref, s0im_ref, buf0_ref, *refs = refs
    (gain_ref, w_in_ref, b_gate_ref, a2_re_ref, a2_im_ref, wb_ref, wc_ref, wd_ref, d_ref,
     w_glu_ref, b_glu_ref, pmix_ref, pscale_ref, w_bs_ref, w_bp_ref, w_out_ref, fgain_ref,
     y_ref, ore_ref, oim_ref, obuf_ref,
     sre_ref, sim_ref, ext_ref, bu_ref, u2_ref, xn_ref, us_ref, zs_ref, gs_ref, zp_ref,
     gp_ref, ys_ref, yp_ref) = refs
    m = n_t * n_b
    n_pairs = n_t // 2
    half = n_pairs * n_b
    step = pl.program_id(0)
    hist_rows = POOL_BUF * n_b

    def load_state():
        if zero_state:
            sre_ref[...] = jnp.zeros(sre_ref.shape, F32)
            sim_ref[...] = jnp.zeros(sim_ref.shape, F32)
            ext_ref[0:hist_rows, :] = jnp.zeros((hist_rows, D_MODEL), F32)
        else:
            sre_ref[...] = s0re_ref[...]
            sim_ref[...] = s0im_ref[...]
            ext_ref[0:hist_rows, :] = buf0_ref[...].reshape(hist_rows, D_MODEL)

    def rows_tb(src_ref):
        return jnp.swapaxes(src_ref[...], 0, 1).reshape(m, D_MODEL)

    def proj(k):
        return jnp.dot(xn_ref[...], w_in_ref[:, k * D_MODEL:(k + 1) * D_MODEL],
                       preferred_element_type=F32)

    def prepare_norm(src_ref):
        xn_ref[...] = _rmsnorm(rows_tb(src_ref), gain_ref[...]).astype(BF16)

    def prepare_u():
        us_ref[...] = proj(0).reshape(n_pairs, 2 * n_b, D_MODEL)

    def prepare_pairs():
        for blk in range(N_BLOCKS):
            lanes = slice(blk * LANES, (blk + 1) * LANES)
            st = slice(blk * BLOCK_STATE, (blk + 1) * BLOCK_STATE)
            u_t0 = us_ref[:, 0:n_b, lanes].reshape(half, LANES)
            u_t1 = us_ref[:, n_b:, lanes].reshape(half, LANES)
            u2_ref[blk] = jnp.concatenate([u_t0, u_t1], axis=1).astype(BF16)
            bu_ref[blk, 0:n_b, 0:BLOCK_STATE] = sre_ref[:, st]
            bu_ref[blk, 0:n_b, BLOCK_STATE:] = sim_ref[:, st]

    def prepare_slabs():
        for blk in range(N_BLOCKS):
            bu_ref[blk, n_b:, :] = jnp.dot(u2_ref[blk], wb_ref[blk], preferred_element_type=F32)

    def prepare(src_ref):
        prepare_norm(src_ref)
        prepare_u()
        prepare_pairs()
        prepare_slabs()

    if carry:
        @pl.when(step == 0)
        def _():
            load_state()
            prepare(prefix_ref)
    else:
        load_state()
        prepare(x_ref)

    piece = 2 * LANES

    def proj_piece(k, p):
        cols = slice(p * piece, (p + 1) * piece)
        v = jnp.dot(xn_ref[...], w_in_ref[:, k * D_MODEL + p * piece:k * D_MODEL + (p + 1) * piece],
                    preferred_element_type=F32)
        if k == 1:
            zs_ref[:, cols] = _silu(v)
        elif k == 4:
            gs_ref[:, cols] = _sigmoid(v + b_gate_ref[:, cols])
        elif k == 2:
            ext_ref[hist_rows:hist_rows + m, cols] = v
        elif k == 3:
            zp_ref[:, cols] = _silu(v)
        else:
            gp_ref[:, cols] = _sigmoid(
                v + b_gate_ref[:, D_MODEL + p * piece:D_MODEL + (p + 1) * piece])

    pieces = [(k, p) for k in (1, 4, 2, 3, 5) for p in range(D_MODEL // piece)]

    for blk in range(N_BLOCKS):
        lanes = slice(blk * LANES, (blk + 1) * LANES)
        st = slice(blk * BLOCK_STATE, (blk + 1) * BLOCK_STATE)
        for k, p in pieces[blk * len(pieces) // N_BLOCKS:(blk + 1) * len(pieces) // N_BLOCKS]:
            proj_piece(k, p)
        a_re = jnp.broadcast_to(a2_re_ref[:, st], (SUBLANES, BLOCK_STATE))
        a_im = jnp.broadcast_to(a2_im_ref[:, st], (SUBLANES, BLOCK_STATE))
        for j in range(n_b // SUBLANES):
            tile = slice(j * SUBLANES, (j + 1) * SUBLANES)
            s_re = sre_ref[tile, st]
            s_im = sim_ref[tile, st]
            for k in range(n_pairs):
                rows = slice((k + 1) * n_b + j * SUBLANES, (k + 1) * n_b + (j + 1) * SUBLANES)
                n_re = a_re * s_re - a_im * s_im + bu_ref[blk, rows, 0:BLOCK_STATE]
                n_im = a_re * s_im + a_im * s_re + bu_ref[blk, rows, BLOCK_STATE:]
                bu_ref[blk, rows, 0:BLOCK_STATE] = n_re
                bu_ref[blk, rows, BLOCK_STATE:] = n_im
                s_re, s_im = n_re, n_im
            sre_ref[tile, st] = s_re
            sim_ref[tile, st] = s_im
        y2 = (jnp.dot(bu_ref[blk, 0:half, :].astype(BF16), wc_ref[blk], preferred_element_type=F32)
              + jnp.dot(u2_ref[blk], wd_ref[blk], preferred_element_type=F32))
        ys_ref[:, 0:n_b, lanes] = y2[:, 0:LANES].reshape(n_pairs, n_b, LANES)
        ys_ref[:, n_b:, lanes] = y2[:, LANES:].reshape(n_pairs, n_b, LANES)

    if pos0 + 1 >= max(POOL_WINDOWS):
        pos = None
    else:
        t_loc = lax.broadcasted_iota(jnp.int32, (m, 1), 0) // n_b
        pos = pos0 + step * n_t + t_loc
    for gi, w in enumerate(POOL_WINDOWS):
        cols = slice(gi * POOL_GROUP_CH, (gi + 1) * POOL_GROUP_CH)
        s = ext_ref[(POOL_BUF + 1 - w) * n_b:hist_rows + m, cols]
        span = 1
        while span < w:
            keep = s.shape[0] - span * n_b
            s = s[:keep] + s[span * n_b:]
            span *= 2
        if pos is None:
            mean = s * (1.0 / w)
        else:
            cnt = jnp.maximum(jnp.minimum(pos + 1, w), 1)
            mean = s * (1.0 / cnt.astype(F32))
        pooled = mean - ext_ref[hist_rows:hist_rows + m, cols]
        yp_ref[:, cols] = jnp.dot(pooled.astype(BF16), pmix_ref[gi],
                                  preferred_element_type=F32) * pscale_ref[:, cols]
    for k in range(POOL_BUF):
        ext_ref[k * n_b:(k + 1) * n_b, :] = ext_ref[(k + n_t) * n_b:(k + n_t + 1) * n_b, :]

    y = ys_ref[...].reshape(m, D_MODEL) + d_ref[...] * us_ref[...].reshape(m, D_MODEL)
    y = jax.nn.gelu(y)
    b = yp_ref[...] * zp_ref[...]
    acc_p = jnp.dot(b.astype(BF16), w_bp_ref[...], preferred_element_type=F32)
    glu = jnp.dot(y.astype(BF16), w_glu_ref[...], preferred_element_type=F32)
    if carry:
        prepare_norm(x_next_ref)
    a = y * _sigmoid(glu + b_glu_ref[...]) * zs_ref[...]
    acc_s = jnp.dot(a.astype(BF16), w_bs_ref[...], preferred_element_type=F32)
    if carry:
        prepare_u()
        prepare_pairs()
    merged = gs_ref[...] * acc_s + gp_ref[...] * acc_p
    delta = jnp.dot(merged.astype(BF16), w_out_ref[...], preferred_element_type=F32)
    if carry:
        prepare_slabs()
    delta_bt = jnp.swapaxes(delta.reshape(n_t, n_b, D_MODEL), 0, 1)
    y_ref[...] = _rmsnorm(x_ref[...] + delta_bt, fgain_ref[...])

    def write_state():
        ore_ref[...] = sre_ref[...]
        oim_ref[...] = sim_ref[...]
        obuf_ref[...] = ext_ref[0:hist_rows, :].reshape(POOL_BUF, n_b, D_MODEL)

    if carry:
        pl.when(step == pl.num_programs(0) - 1)(write_state)
    else:
        write_state()


def _layer_call(x, state, weights, *, n_t, n_b, carry, pos0, name, prefix=None):
    batch, time, _ = x.shape
    m = n_t * n_b
    x_block = (n_b, n_t, D_MODEL)
    buf_block = (POOL_BUF, n_b, D_MODEL)
    if carry:
        assert batch == n_b and time % n_t == 0 and prefix.shape == x_block
        n_blocks = time // n_t
        n_steps = n_blocks + 1
        x_specs = [pl.BlockSpec(x_block, lambda i: (0, 0, 0)),
                   pl.BlockSpec(x_block, lambda i: (0, jnp.maximum(i - 1, 0), 0)),
                   pl.BlockSpec(x_block, lambda i: (0, jnp.minimum(i, n_blocks - 1), 0))]
        y_map = lambda i: (0, jnp.maximum(i - 1, 0), 0)
        per_step = lambda i: (0, 0)
        per_step3 = lambda i: (0, 0, 0)
        xs = (prefix, x, x)
    else:
        assert time == n_t and batch % n_b == 0
        n_steps = batch // n_b
        x_specs = [pl.BlockSpec(x_block, lambda i: (i, 0, 0))]
        y_map = lambda i: (i, 0, 0)
        per_step = lambda i: (i, 0)
        per_step3 = lambda i: (0, i, 0)
        xs = (x,)
    whole = pl.BlockSpec(memory_space=pltpu.VMEM)
    state_specs = [
        pl.BlockSpec((n_b, STATE_W), per_step),
        pl.BlockSpec((n_b, STATE_W), per_step),
        pl.BlockSpec(buf_block, per_step3),
    ]
    kernel = functools.partial(_layer_kernel, n_t=n_t, n_b=n_b, carry=carry, pos0=pos0,
                               zero_state=state is None)
    return pl.pallas_call(
        kernel,
        grid=(n_steps,),
        in_specs=x_specs + ([] if state is None else state_specs) + [whole] * len(weights),
        out_specs=[pl.BlockSpec(x_block, y_map)] + state_specs,
        out_shape=[
            jax.ShapeDtypeStruct((batch, time, D_MODEL), F32),
            jax.ShapeDtypeStruct((batch, STATE_W), F32),
            jax.ShapeDtypeStruct((batch, STATE_W), F32),
            jax.ShapeDtypeStruct((POOL_BUF, batch, D_MODEL), F32),
        ],
        scratch_shapes=[
            pltpu.VMEM((n_b, STATE_W), F32),
            pltpu.VMEM((n_b, STATE_W), F32),
            pltpu.VMEM(((POOL_BUF + n_t) * n_b, D_MODEL), F32),
            pltpu.VMEM((N_BLOCKS, m // 2 + n_b, 2 * BLOCK_STATE), F32),
            pltpu.VMEM((N_BLOCKS, m // 2, 2 * LANES), BF16),
            pltpu.VMEM((m, D_MODEL), BF16),
            pltpu.VMEM((n_t // 2, 2 * n_b, D_MODEL), F32),
            pltpu.VMEM((m, D_MODEL), F32),
            pltpu.VMEM((m, D_MODEL), F32),
            pltpu.VMEM((m, D_MODEL), F32),
            pltpu.VMEM((m, D_MODEL), F32),
            pltpu.VMEM((n_t // 2, 2 * n_b, D_MODEL), F32),
            pltpu.VMEM((m, D_MODEL), F32),
        ],
        compiler_params=pltpu.CompilerParams(
            dimension_semantics=("arbitrary",), vmem_limit_bytes=VMEM_LIMIT_BYTES),
        name=name,
    )(*xs, *(() if state is None else state), *weights)


PREP_BLOCKS_PER_STEP = 4


def _s5_weights_kernel(abar_re_ref, abar_im_ref, a2_re_ref, a2_im_ref, q_re_ref, q_im_ref,
                       bt_re_ref, bt_im_ref, c_re_ref, c_im_ref, wb_ref, wc_ref, wd_ref):
    gpb, n_c, n_p = GROUPS_PER_BLOCK, SSM_GROUP_CH, SSM_STATE
    lane = lax.broadcasted_iota(jnp.int32, (n_p, BLOCK_STATE), 1)
    sel = jnp.where(jnp.bitwise_and(lane, n_p - 1)
                    == lax.broadcasted_iota(jnp.int32, (n_p, BLOCK_STATE), 0), 1.0, 0.0).astype(BF16)
    same_group = (
        jnp.right_shift(lax.broadcasted_iota(jnp.int32, (LANES, BLOCK_STATE), 0),
                        n_c.bit_length() - 1)
        == jnp.right_shift(lax.broadcasted_iota(jnp.int32, (LANES, BLOCK_STATE), 1),
                           n_p.bit_length() - 1))
    same_group_cc = (
        jnp.right_shift(lax.broadcasted_iota(jnp.int32, (LANES, LANES), 0), n_c.bit_length() - 1)
        == jnp.right_shift(lax.broadcasted_iota(jnp.int32, (LANES, LANES), 1),
                           n_c.bit_length() - 1))

    def per_row(ref, blk):
        v = ref[blk]
        return jnp.broadcast_to(v[:, None, :], (gpb, n_c, n_p)).reshape(LANES, n_p)

    def expand(mat):
        tiled = jnp.dot(mat.astype(BF16), sel, preferred_element_type=F32)
        return jnp.where(same_group, tiled, 0.0)

    def gram(x, y):
        p = lax.dot_general(x, y, (((1,), (1,)), ((), ())), precision=lax.Precision.HIGHEST,
                            preferred_element_type=F32)
        return jnp.where(same_group_cc, p, 0.0)

    for blk in range(PREP_BLOCKS_PER_STEP):
        ar, ai = per_row(abar_re_ref, blk), per_row(abar_im_ref, blk)
        ar2, ai2 = per_row(a2_re_ref, blk), per_row(a2_im_ref, blk)
        qr, qi = per_row(q_re_ref, blk), per_row(q_im_ref, blk)
        bt_re, bt_im = bt_re_ref[blk], bt_im_ref[blk]
        c_re, c_im = c_re_ref[blk], c_im_ref[blk]
        bbar_re = qr * bt_re - qi * bt_im
        bbar_im = qr * bt_im + qi * bt_re
        ab_re = ar * bbar_re - ai * bbar_im
        ab_im = ar * bbar_im + ai * bbar_re
        ca_re = c_re * ar - c_im * ai
        ca_im = c_re * ai + c_im * ar
        ca2_re = c_re * ar2 - c_im * ai2
        ca2_im = c_re * ai2 + c_im * ar2
        cb = gram(bbar_re, c_re) - gram(bbar_im, c_im)
        cab = gram(bbar_re, ca_re) - gram(bbar_im, ca_im)

        wb_ref[blk, 0:LANES, 0:BLOCK_STATE] = expand(ab_re).astype(BF16)
        wb_ref[blk, 0:LANES, BLOCK_STATE:] = expand(ab_im).astype(BF16)
        wb_ref[blk, LANES:, 0:BLOCK_STATE] = expand(bbar_re).astype(BF16)
        wb_ref[blk, LANES:, BLOCK_STATE:] = expand(bbar_im).astype(BF16)
        wc_ref[blk, 0:BLOCK_STATE, 0:LANES] = expand(ca_re).T.astype(BF16)
        wc_ref[blk, 0:BLOCK_STATE, LANES:] = expand(ca2_re).T.astype(BF16)
        wc_ref[blk, BLOCK_STATE:, 0:LANES] = expand(-ca_im).T.astype(BF16)
        wc_ref[blk, BLOCK_STATE:, LANES:] = expand(-ca2_im).T.astype(BF16)
        wd_ref[blk, 0:LANES, 0:LANES] = cb.astype(BF16)
        wd_ref[blk, 0:LANES, LANES:] = cab.astype(BF16)
        wd_ref[blk, LANES:, 0:LANES] = jnp.zeros((LANES, LANES), BF16)
        wd_ref[blk, LANES:, LANES:] = cb.astype(BF16)


def _s5_weights_call(per_state, per_channel):
    gpb, per_step = GROUPS_PER_BLOCK, PREP_BLOCKS_PER_STEP
    ins = ([v.reshape(N_BLOCKS, gpb, SSM_STATE) for v in per_state]
           + [v.reshape(N_BLOCKS, LANES, SSM_STATE) for v in per_channel])
    in_specs = ([pl.BlockSpec((per_step, gpb, SSM_STATE), lambda i: (i, 0, 0))] * len(per_state)
                + [pl.BlockSpec((per_step, LANES, SSM_STATE), lambda i: (i, 0, 0))]
                * len(per_channel))
    shapes = [(2 * LANES, 2 * BLOCK_STATE), (2 * BLOCK_STATE, 2 * LANES), (2 * LANES, 2 * LANES)]
    return pl.pallas_call(
        _s5_weights_kernel,
        grid=(N_BLOCKS // per_step,), in_specs=in_specs,
        out_specs=[pl.BlockSpec((per_step,) + s, lambda i: (i, 0, 0)) for s in shapes],
        out_shape=[jax.ShapeDtypeStruct((N_BLOCKS,) + s, BF16) for s in shapes],
        name="s5_block_diag_weights",
    )(*ins)


def _ssm_weights(a_re, a_im, log_dt, b_re, b_im, c_re, c_im):
    dt = jnp.exp(log_dt)[:, None]
    mag = jnp.exp(dt * a_re)
    ang = dt * a_im
    abar_re = mag * jnp.cos(ang)
    abar_im = mag * jnp.sin(ang)
    den = a_re * a_re + a_im * a_im
    nr = abar_re - 1.0
    ni = abar_im
    q_re = (nr * a_re + ni * a_im) / den
    q_im = (ni * a_re - nr * a_im) / den
    a2_re = abar_re * abar_re - abar_im * abar_im
    a2_im = 2.0 * abar_re * abar_im
    wb, wc, wd = _s5_weights_call(
        (abar_re, abar_im, a2_re, a2_im, q_re, q_im),
        (jnp.swapaxes(b_re, 1, 2), jnp.swapaxes(b_im, 1, 2), c_re, c_im))
    return a2_re, a2_im, wb, wc, wd


PROMPT_T = 32
SAMPLE_B = 32


def kernel(x_prompt, x_sample, state_ssm_re, state_ssm_im, state_pool, meta_tokens, norm_gain, w_in, b_gate, ssm_a_re, ssm_a_im, ssm_log_dt, ssm_b_re, ssm_b_im, ssm_c_re, ssm_c_im, ssm_d, w_glu, b_glu, pool_mix, pool_scale, w_branch_ssm, w_branch_pool, w_out, final_norm_gain):
    batch, seq, _ = x_prompt.shape
    dec_batch, dec_seq, _ = x_sample.shape
    depth = norm_gain.shape[0]
    assert depth == 1
    l = 0
    a2_re, a2_im, wb, wc, wd = _ssm_weights(
        ssm_a_re[l], ssm_a_im[l], ssm_log_dt[l], ssm_b_re[l], ssm_b_im[l],
        ssm_c_re[l], ssm_c_im[l])
    row = lambda v: v.reshape(1, -1).astype(F32)
    weights = (
        row(norm_gain[l]), w_in[l].astype(BF16), row(b_gate[l]),
        row(a2_re), row(a2_im), wb, wc, wd, row(ssm_d[l]),
        w_glu[l].astype(BF16), row(b_glu[l]), pool_mix[l].astype(BF16), row(pool_scale[l]),
        w_branch_ssm[l].astype(BF16), w_branch_pool[l].astype(BF16), w_out[l].astype(BF16),
        row(final_norm_gain),
    )

    assert batch == SUBLANES and N_META <= PROMPT_T
    meta = jnp.broadcast_to(meta_tokens[None].astype(x_prompt.dtype), (batch, N_META, D_MODEL))
    prefix = jnp.concatenate(
        [jnp.zeros((batch, PROMPT_T - N_META, D_MODEL), x_prompt.dtype), meta], axis=1)
    y_prompt, p_re, p_im, p_buf = _layer_call(
        x_prompt, None, weights, prefix=prefix,
        n_t=PROMPT_T, n_b=batch, carry=True, pos0=N_META - PROMPT_T, name="prompt_layer")

    sample_state = (state_ssm_re[l].reshape(dec_batch, STATE_W),
                    state_ssm_im[l].reshape(dec_batch, STATE_W),
                    jnp.swapaxes(state_pool[l], 0, 1))
    y_sample, s_re, s_im, s_buf = _layer_call(
        x_sample, sample_state, weights,
        n_t=dec_seq, n_b=SAMPLE_B, carry=False, pos0=PAST_LEN, name="sample_layer")

    state_shape = (1, -1, SSM_GROUPS, SSM_STATE)
    return (y_prompt, y_sample,
            p_re.reshape(state_shape), p_im.reshape(state_shape), jnp.swapaxes(p_buf, 0, 1)[None],
            s_re.reshape(state_shape), s_im.reshape(state_shape), jnp.swapaxes(s_buf, 0, 1)[None])
```

```python
import functools

import jax
import jax.numpy as jnp
from jax import lax
from jax.experimental import pallas as pl
from jax.experimental.pallas import tpu as pltpu

D_MODEL = 1024
N_META = 16
SSM_GROUP_CH = 16
SSM_GROUPS = D_MODEL // SSM_GROUP_CH
SSM_STATE = 64
POOL_WINDOWS = (2, 4, 8, 16)
POOL_GROUP_CH = D_MODEL // len(POOL_WINDOWS)
POOL_BUF = max(POOL_WINDOWS) - 1
EPS = 1e-6
PAST_LEN = 16384

SUBLANES = 8
LANES = 128
GROUPS_PER_BLOCK = LANES // SSM_GROUP_CH
N_BLOCKS = SSM_GROUPS // GROUPS_PER_BLOCK
BLOCK_STATE = GROUPS_PER_BLOCK * SSM_STATE
STATE_W = SSM_GROUPS * SSM_STATE
VMEM_LIMIT_BYTES = 56 * 1024 * 1024

F32 = jnp.float32
BF16 = jnp.bfloat16


def _rmsnorm(x, gain):
    ms = jnp.mean(x * x, axis=-1, keepdims=True)
    return (x * lax.rsqrt(ms + EPS)) * gain


_sigmoid = jax.nn.sigmoid


def _silu(x):
    return x * _sigmoid(x)


def _layer_kernel(*refs, n_t, n_b, carry, pos0, zero_state):
    if carry:
        prefix_ref, x_ref, x_next_ref, *refs = refs
    else:
        x_ref, *refs = refs
    if not zero_state:
        s0re_ref, s0im_ref, buf0_ref, *refs = refs
    (gain_ref, w_in_ref, b_gate_ref, a2_re_ref, a2_im_ref, wb_ref, wc_ref, wd_ref, d_ref,
     w_glu_ref, b_glu_ref, pmix_ref, pscale_ref, w_bs_ref, w_bp_ref, w_out_ref, fgain_ref,
     y_ref, ore_ref, oim_ref, obuf_ref,
     sre_ref, sim_ref, ext_ref, bu_ref, u2_ref, xn_ref, us_ref, zs_ref, gs_ref, zp_ref,
     gp_ref, ys_ref, yp_ref, *tail) = refs
    if carry:
        (xn_next_ref,) = tail
    m = n_t * n_b
    n_pairs = n_t // 2
    half = n_pairs * n_b
    step = pl.program_id(0)
    hist_rows = POOL_BUF * n_b

    def load_state():
        if zero_state:
            sre_ref[...] = jnp.zeros(sre_ref.shape, F32)
            sim_ref[...] = jnp.zeros(sim_ref.shape, F32)
            ext_ref[0:hist_rows, :] = jnp.zeros((hist_rows, D_MODEL), F32)
        else:
            sre_ref[...] = s0re_ref[...]
            sim_ref[...] = s0im_ref[...]
            ext_ref[0:hist_rows, :] = buf0_ref[...].reshape(hist_rows, D_MODEL)

    def rows_tb(src_ref):
        return jnp.swapaxes(src_ref[...], 0, 1).reshape(m, D_MODEL)

    def proj(k, src_ref=xn_ref):
        return jnp.dot(src_ref[...], w_in_ref[:, k * D_MODEL:(k + 1) * D_MODEL],
                       preferred_element_type=F32)

    def prepare_norm(src_ref, dst_ref=xn_ref):
        dst_ref[...] = _rmsnorm(rows_tb(src_ref), gain_ref[...]).astype(BF16)

    def prepare_u(src_ref=xn_ref):
        us_ref[...] = proj(0, src_ref).reshape(n_pairs, 2 * n_b, D_MODEL)

    def prepare_pairs():
        for blk in range(N_BLOCKS):
            lanes = slice(blk * LANES, (blk + 1) * LANES)
            st = slice(blk * BLOCK_STATE, (blk + 1) * BLOCK_STATE)
            u_t0 = us_ref[:, 0:n_b, lanes].reshape(half, LANES)
            u_t1 = us_ref[:, n_b:, lanes].reshape(half, LANES)
            u2_ref[blk] = jnp.concatenate([u_t0, u_t1], axis=1).astype(BF16)
            bu_ref[blk, 0:n_b, 0:BLOCK_STATE] = sre_ref[:, st]
            bu_ref[blk, 0:n_b, BLOCK_STATE:] = sim_ref[:, st]

    def prepare_slabs():
        for blk in range(N_BLOCKS):
            bu_ref[blk, n_b:, :] = jnp.dot(u2_ref[blk], wb_ref[blk], preferred_element_type=F32)

    def prepare(src_ref):
        prepare_norm(src_ref)
        prepare_u()
        prepare_pairs()
        prepare_slabs()

    if carry:
        @pl.when(step == 0)
        def _():
            load_state()
            prepare(prefix_ref)

        prepare_norm(x_next_ref, xn_next_ref)
    else:
        load_state()
        prepare(x_ref)

    piece = 2 * LANES

    def proj_piece(k, p):
        cols = slice(p * piece, (p + 1) * piece)
        v = jnp.dot(xn_ref[...], w_in_ref[:, k * D_MODEL + p * piece:k * D_MODEL + (p + 1) * piece],
                    preferred_element_type=F32)
        if k == 1:
            zs_ref[:, cols] = _silu(v)
        elif k == 4:
            gs_ref[:, cols] = _sigmoid(v + b_gate_ref[:, cols])
        elif k == 2:
            ext_ref[hist_rows:hist_rows + m, cols] = v
        elif k == 3:
            zp_ref[:, cols] = _silu(v)
        else:
            gp_ref[:, cols] = _sigmoid(
                v + b_gate_ref[:, D_MODEL + p * piece:D_MODEL + (p + 1) * piece])

    pieces = [(k, p) for k in (2, 1, 4, 3, 5) for p in range(D_MODEL // piece)]
    first_pool_blk = 2

    if pos0 + 1 >= max(POOL_WINDOWS):
        pos = None
    else:
        t_loc = lax.broadcasted_iota(jnp.int32, (m, 1), 0) // n_b
        pos = pos0 + step * n_t + t_loc

    def pool_group(gi):
        w = POOL_WINDOWS[gi]
        cols = slice(gi * POOL_GROUP_CH, (gi + 1) * POOL_GROUP_CH)
        s = ext_ref[(POOL_BUF + 1 - w) * n_b:hist_rows + m, cols]
        span = 1
        while span < w:
            keep = s.shape[0] - span * n_b
            s = s[:keep] + s[span * n_b:]
            span *= 2
        if pos is None:
            mean = s * (1.0 / w)
        else:
            cnt = jnp.maximum(jnp.minimum(pos + 1, w), 1)
            mean = s * (1.0 / cnt.astype(F32))
        pooled = mean - ext_ref[hist_rows:hist_rows + m, cols]
        yp_ref[:, cols] = jnp.dot(pooled.astype(BF16), pmix_ref[gi],
                                  preferred_element_type=F32) * pscale_ref[:, cols]

    for blk in range(N_BLOCKS):
        lanes = slice(blk * LANES, (blk + 1) * LANES)
        st = slice(blk * BLOCK_STATE, (blk + 1) * BLOCK_STATE)
        for k, p in pieces[blk * len(pieces) // N_BLOCKS:(blk + 1) * len(pieces) // N_BLOCKS]:
            proj_piece(k, p)
        if first_pool_blk <= blk < first_pool_blk + len(POOL_WINDOWS):
            pool_group(blk - first_pool_blk)
        a_re = jnp.broadcast_to(a2_re_ref[:, st], (SUBLANES, BLOCK_STATE))
        a_im = jnp.broadcast_to(a2_im_ref[:, st], (SUBLANES, BLOCK_STATE))
        for j in range(n_b // SUBLANES):
            tile = slice(j * SUBLANES, (j + 1) * SUBLANES)
            s_re = sre_ref[tile, st]
            s_im = sim_ref[tile, st]
            for k in range(n_pairs):
                rows = slice((k + 1) * n_b + j * SUBLANES, (k + 1) * n_b + (j + 1) * SUBLANES)
                n_re = a_re * s_re - a_im * s_im + bu_ref[blk, rows, 0:BLOCK_STATE]
                n_im = a_re * s_im + a_im * s_re + bu_ref[blk, rows, BLOCK_STATE:]
                bu_ref[blk, rows, 0:BLOCK_STATE] = n_re
                bu_ref[blk, rows, BLOCK_STATE:] = n_im
                s_re, s_im = n_re, n_im
            sre_ref[tile, st] = s_re
            sim_ref[tile, st] = s_im
        y2 = (jnp.dot(bu_ref[blk, 0:half, :].astype(BF16), wc_ref[blk], preferred_element_type=F32)
              + jnp.dot(u2_ref[blk], wd_ref[blk], preferred_element_type=F32))
        ys_ref[:, 0:n_b, lanes] = y2[:, 0:LANES].reshape(n_pairs, n_b, LANES)
        ys_ref[:, n_b:, lanes] = y2[:, LANES:].reshape(n_pairs, n_b, LANES)

    for k in range(POOL_BUF):
        ext_ref[k * n_b:(k + 1) * n_b, :] = ext_ref[(k + n_t) * n_b:(k + n_t + 1) * n_b, :]

    y = ys_ref[...].reshape(m, D_MODEL) + d_ref[...] * us_ref[...].reshape(m, D_MODEL)
    y = jax.nn.gelu(y)
    b = yp_ref[...] * zp_ref[...]
    acc_p = jnp.dot(b.astype(BF16), w_bp_ref[...], preferred_element_type=F32)
    glu = jnp.dot(y.astype(BF16), w_glu_ref[...], preferred_element_type=F32)
    a = y * _sigmoid(glu + b_glu_ref[...]) * zs_ref[...]
    acc_s = jnp.dot(a.astype(BF16), w_bs_ref[...], preferred_element_type=F32)
    if carry:
        prepare_u(xn_next_ref)
        prepare_pairs()
    merged = gs_ref[...] * acc_s + gp_ref[...] * acc_p
    delta = jnp.dot(merged.astype(BF16), w_out_ref[...], preferred_element_type=F32)
    if carry:
        prepare_slabs()
    delta_bt = jnp.swapaxes(delta.reshape(n_t, n_b, D_MODEL), 0, 1)
    y_ref[...] = _rmsnorm(x_ref[...] + delta_bt, fgain_ref[...])
    if carry:
        xn_ref[...] = xn_next_ref[...]

    def write_state():
        ore_ref[...] = sre_ref[...]
        oim_ref[...] = sim_ref[...]
        obuf_ref[...] = ext_ref[0:hist_rows, :].reshape(POOL_BUF, n_b, D_MODEL)

    if carry:
        pl.when(step == pl.num_programs(0) - 1)(write_state)
    else:
        write_state()


def _layer_call(x, state, weights, *, n_t, n_b, carry, pos0, name, prefix=None):
    batch, time, _ = x.shape
    m = n_t * n_b
    x_block = (n_b, n_t, D_MODEL)
    buf_block = (POOL_BUF, n_b, D_MODEL)
    if carry:
        assert batch == n_b and time % n_t == 0 and prefix.shape == x_block
        n_blocks = time // n_t
        n_steps = n_blocks + 1
        x_specs = [pl.BlockSpec(x_block, lambda i: (0, 0, 0)),
                   pl.BlockSpec(x_block, lambda i: (0, jnp.maximum(i - 1, 0), 0)),
                   pl.BlockSpec(x_block, lambda i: (0, jnp.minimum(i, n_blocks - 1), 0))]
        y_map = lambda i: (0, jnp.maximum(i - 1, 0), 0)
        per_step = lambda i: (0, 0)
        per_step3 = lambda i: (0, 0, 0)
        xs = (prefix, x, x)
        extra_scratch = [pltpu.VMEM((m, D_MODEL), BF16)]
    else:
        assert time == n_t and batch % n_b == 0
        n_steps = batch // n_b
        x_specs = [pl.BlockSpec(x_block, lambda i: (i, 0, 0))]
        y_map = lambda i: (i, 0, 0)
        per_step = lambda i: (i, 0)
        per_step3 = lambda i: (0, i, 0)
        xs = (x,)
        extra_scratch = []
    whole = pl.BlockSpec(memory_space=pltpu.VMEM)
    state_specs = [
        pl.BlockSpec((n_b, STATE_W), per_step),
        pl.BlockSpec((n_b, STATE_W), per_step),
        pl.BlockSpec(buf_block, per_step3),
    ]
    kernel = functools.partial(_layer_kernel, n_t=n_t, n_b=n_b, carry=carry, pos0=pos0,
                               zero_state=state is None)
    return pl.pallas_call(
        kernel,
        grid=(n_steps,),
        in_specs=x_specs + ([] if state is None else state_specs) + [whole] * len(weights),
        out_specs=[pl.BlockSpec(x_block, y_map)] + state_specs,
        out_shape=[
            jax.ShapeDtypeStruct((batch, time, D_MODEL), F32),
            jax.ShapeDtypeStruct((batch, STATE_W), F32),
            jax.ShapeDtypeStruct((batch, STATE_W), F32),
            jax.ShapeDtypeStruct((POOL_BUF, batch, D_MODEL), F32),
        ],
        scratch_shapes=[
            pltpu.VMEM((n_b, STATE_W), F32),
            pltpu.VMEM((n_b, STATE_W), F32),
            pltpu.VMEM(((POOL_BUF + n_t) * n_b, D_MODEL), F32),
            pltpu.VMEM((N_BLOCKS, m // 2 + n_b, 2 * BLOCK_STATE), F32),
            pltpu.VMEM((N_BLOCKS, m // 2, 2 * LANES), BF16),
            pltpu.VMEM((m, D_MODEL), BF16),
            pltpu.VMEM((n_t // 2, 2 * n_b, D_MODEL), F32),
            pltpu.VMEM((m, D_MODEL), F32),
            pltpu.VMEM((m, D_MODEL), F32),
            pltpu.VMEM((m, D_MODEL), F32),
            pltpu.VMEM((m, D_MODEL), F32),
            pltpu.VMEM((n_t // 2, 2 * n_b, D_MODEL), F32),
            pltpu.VMEM((m, D_MODEL), F32),
        ] + extra_scratch,
        compiler_params=pltpu.CompilerParams(
            dimension_semantics=("arbitrary",), vmem_limit_bytes=VMEM_LIMIT_BYTES),
        name=name,
    )(*xs, *(() if state is None else state), *weights)


PREP_BLOCKS_PER_STEP = 4


def _s5_weights_kernel(abar_re_ref, abar_im_ref, a2_re_ref, a2_im_ref, q_re_ref, q_im_ref,
                       bt_re_ref, bt_im_ref, c_re_ref, c_im_ref, wb_ref, wc_ref, wd_ref):
    gpb, n_c, n_p = GROUPS_PER_BLOCK, SSM_GROUP_CH, SSM_STATE
    lane = lax.broadcasted_iota(jnp.int32, (n_p, BLOCK_STATE), 1)
    sel = jnp.where(jnp.bitwise_and(lane, n_p - 1)
                    == lax.broadcasted_iota(jnp.int32, (n_p, BLOCK_STATE), 0), 1.0, 0.0).astype(BF16)
    same_group = (
        jnp.right_shift(lax.broadcasted_iota(jnp.int32, (LANES, BLOCK_STATE), 0),
                        n_c.bit_length() - 1)
        == jnp.right_shift(lax.broadcasted_iota(jnp.int32, (LANES, BLOCK_STATE), 1),
                           n_p.bit_length() - 1))
    same_group_cc = (
        jnp.right_shift(lax.broadcasted_iota(jnp.int32, (LANES, LANES), 0), n_c.bit_length() - 1)
        == jnp.right_shift(lax.broadcasted_iota(jnp.int32, (LANES, LANES), 1),
                           n_c.bit_length() - 1))

    def per_row(ref, blk):
        v = ref[blk]
        return jnp.broadcast_to(v[:, None, :], (gpb, n_c, n_p)).reshape(LANES, n_p)

    def expand(mat):
        tiled = jnp.dot(mat.astype(BF16), sel, preferred_element_type=F32)
        return jnp.where(same_group, tiled, 0.0)

    def gram(x, y):
        p = lax.dot_general(x, y, (((1,), (1,)), ((), ())), precision=lax.Precision.HIGHEST,
                            preferred_element_type=F32)
        return jnp.where(same_group_cc, p, 0.0)

    for blk in range(PREP_BLOCKS_PER_STEP):
        ar, ai = per_row(abar_re_ref, blk), per_row(abar_im_ref, blk)
        ar2, ai2 = per_row(a2_re_ref, blk), per_row(a2_im_ref, blk)
        qr, qi = per_row(q_re_ref, blk), per_row(q_im_ref, blk)
        bt_re, bt_im = bt_re_ref[blk], bt_im_ref[blk]
        c_re, c_im = c_re_ref[blk], c_im_ref[blk]
        bbar_re = qr * bt_re - qi * bt_im
        bbar_im = qr * bt_im + qi * bt_re
        ab_re = ar * bbar_re - ai * bbar_im
        ab_im = ar * bbar_im + ai * bbar_re
        ca_re = c_re * ar - c_im * ai
        ca_im = c_re * ai + c_im * ar
        ca2_re = c_re * ar2 - c_im * ai2
        ca2_im = c_re * ai2 + c_im * ar2
        cb = gram(bbar_re, c_re) - gram(bbar_im, c_im)
        cab = gram(bbar_re, ca_re) - gram(bbar_im, ca_im)

        wb_ref[blk, 0:LANES, 0:BLOCK_STATE] = expand(ab_re).astype(BF16)
        wb_ref[blk, 0:LANES, BLOCK_STATE:] = expand(ab_im).astype(BF16)
        wb_ref[blk, LANES:, 0:BLOCK_STATE] = expand(bbar_re).astype(BF16)
        wb_ref[blk, LANES:, BLOCK_STATE:] = expand(bbar_im).astype(BF16)
        wc_ref[blk, 0:BLOCK_STATE, 0:LANES] = expand(ca_re).T.astype(BF16)
        wc_ref[blk, 0:BLOCK_STATE, LANES:] = expand(ca2_re).T.astype(BF16)
        wc_ref[blk, BLOCK_STATE:, 0:LANES] = expand(-ca_im).T.astype(BF16)
        wc_ref[blk, BLOCK_STATE:, LANES:] = expand(-ca2_im).T.astype(BF16)
        wd_ref[blk, 0:LANES, 0:LANES] = cb.astype(BF16)
        wd_ref[blk, 0:LANES, LANES:] = cab.astype(BF16)
        wd_ref[blk, LANES:, 0:LANES] = jnp.zeros((LANES, LANES), BF16)
        wd_ref[blk, LANES:, LANES:] = cb.astype(BF16)


def _s5_weights_call(per_state, per_channel):
    gpb, per_step = GROUPS_PER_BLOCK, PREP_BLOCKS_PER_STEP
    ins = ([v.reshape(N_BLOCKS, gpb, SSM_STATE) for v in per_state]
           + [v.reshape(N_BLOCKS, LANES, SSM_STATE) for v in per_channel])
    in_specs = ([pl.BlockSpec((per_step, gpb, SSM_STATE), lambda i: (i, 0, 0))] * len(per_state)
                + [pl.BlockSpec((per_step, LANES, SSM_STATE), lambda i: (i, 0, 0))]
                * len(per_channel))
    shapes = [(2 * LANES, 2 * BLOCK_STATE), (2 * BLOCK_STATE, 2 * LANES), (2 * LANES, 2 * LANES)]
    return pl.pallas_call(
        _s5_weights_kernel,
        grid=(N_BLOCKS // per_step,), in_specs=in_specs,
        out_specs=[pl.BlockSpec((per_step,) + s, lambda i: (i, 0, 0)) for s in shapes],
        out_shape=[jax.ShapeDtypeStruct((N_BLOCKS,) + s, BF16) for s in shapes],
        name="s5_block_diag_weights",
    )(*ins)


def _ssm_weights(a_re, a_im, log_dt, b_re, b_im, c_re, c_im):
    dt = jnp.exp(log_dt)[:, None]
    mag = jnp.exp(dt * a_re)
    ang = dt * a_im
    abar_re = mag * jnp.cos(ang)
    abar_im = mag * jnp.sin(ang)
    den = a_re * a_re + a_im * a_im
    nr = abar_re - 1.0
    ni = abar_im
    q_re = (nr * a_re + ni * a_im) / den
    q_im = (ni * a_re - nr * a_im) / den
    a2_re = abar_re * abar_re - abar_im * abar_im
    a2_im = 2.0 * abar_re * abar_im
    wb, wc, wd = _s5_weights_call(
        (abar_re, abar_im, a2_re, a2_im, q_re, q_im),
        (jnp.swapaxes(b_re, 1, 2), jnp.swapaxes(b_im, 1, 2), c_re, c_im))
    return a2_re, a2_im, wb, wc, wd


PROMPT_T = 32
SAMPLE_B = 32


def kernel(x_prompt, x_sample, state_ssm_re, state_ssm_im, state_pool, meta_tokens, norm_gain, w_in, b_gate, ssm_a_re, ssm_a_im, ssm_log_dt, ssm_b_re, ssm_b_im, ssm_c_re, ssm_c_im, ssm_d, w_glu, b_glu, pool_mix, pool_scale, w_branch_ssm, w_branch_pool, w_out, final_norm_gain):
    batch, seq, _ = x_prompt.shape
    dec_batch, dec_seq, _ = x_sample.shape
    depth = norm_gain.shape[0]
    assert depth == 1
    l = 0
    a2_re, a2_im, wb, wc, wd = _ssm_weights(
        ssm_a_re[l], ssm_a_im[l], ssm_log_dt[l], ssm_b_re[l], ssm_b_im[l],
        ssm_c_re[l], ssm_c_im[l])
    row = lambda v: v.reshape(1, -1).astype(F32)
    weights = (
        row(norm_gain[l]), w_in[l].astype(BF16), row(b_gate[l]),
        row(a2_re), row(a2_im), wb, wc, wd, row(ssm_d[l]),
        w_glu[l].astype(BF16), row(b_glu[l]), pool_mix[l].astype(BF16), row(pool_scale[l]),
        w_branch_ssm[l].astype(BF16), w_branch_pool[l].astype(BF16), w_out[l].astype(BF16),
        row(final_norm_gain),
    )

    assert batch == SUBLANES and N_META <= PROMPT_T
    meta = jnp.broadcast_to(meta_tokens[None].astype(x_prompt.dtype), (batch, N_META, D_MODEL))
    prefix = jnp.concatenate(
        [jnp.zeros((batch, PROMPT_T - N_META, D_MODEL), x_prompt.dtype), meta], axis=1)
    y_prompt, p_re, p_im, p_buf = _layer_call(
        x_prompt, None, weights, prefix=prefix,
        n_t=PROMPT_T, n_b=batch, carry=True, pos0=N_META - PROMPT_T, name="prompt_layer")

    sample_state = (state_ssm_re[l].reshape(dec_batch, STATE_W),
                    state_ssm_im[l].reshape(dec_batch, STATE_W),
                    jnp.swapaxes(state_pool[l], 0, 1))
    y_sample, s_re, s_im, s_buf = _layer_call(
        x_sample, sample_state, weights,
        n_t=dec_seq, n_b=SAMPLE_B, carry=False, pos0=PAST_LEN, name="sample_layer")

    state_shape = (1, -1, SSM_GROUPS, SSM_STATE)
    return (y_prompt, y_sample,
            p_re.reshape(state_shape), p_im.reshape(state_shape), jnp.swapaxes(p_buf, 0, 1)[None],
            s_re.reshape(state_shape), s_im.reshape(state_shape), jnp.swapaxes(s_buf, 0, 1)[None])
```

```python
import functools

import jax
import jax.numpy as jnp
from jax import lax
from jax.experimental import pallas as pl
from jax.experimental.pallas import tpu as pltpu

D_MODEL = 1024
N_META = 16
SSM_GROUP_CH = 16
SSM_GROUPS = D_MODEL // SSM_GROUP_CH
SSM_STATE = 64
POOL_WINDOWS = (2, 4, 8, 16)
POOL_GROUP_CH = D_MODEL // len(POOL_WINDOWS)
POOL_BUF = max(POOL_WINDOWS) - 1
EPS = 1e-6
PAST_LEN = 16384

SUBLANES = 8
LANES = 128
GROUPS_PER_BLOCK = LANES // SSM_GROUP_CH
N_BLOCKS = SSM_GROUPS // GROUPS_PER_BLOCK
BLOCK_STATE = GROUPS_PER_BLOCK * SSM_STATE
STATE_W = SSM_GROUPS * SSM_STATE
VMEM_LIMIT_BYTES = 56 * 1024 * 1024

F32 = jnp.float32
BF16 = jnp.bfloat16


def _rmsnorm(x, gain):
    ms = jnp.mean(x * x, axis=-1, keepdims=True)
    return (x * lax.rsqrt(ms + EPS)) * gain


_sigmoid = jax.nn.sigmoid


def _silu(x):
    return x * _sigmoid(x)


def _layer_kernel(*refs, n_t, n_b, carry, pos0, zero_state):
    if carry:
        prefix_ref, x_ref, x_next_ref, *refs = refs
    else:
        x_ref, *refs = refs
    if not zero_state:
        s0re_ref, s0im_ref, buf0_ref, *refs = refs
    (gain_ref, w_in_ref, b_gate_ref, a2_re_ref, a2_im_ref, wb_ref, wc_ref, wd_ref, d_ref,
     w_glu_ref, b_glu_ref, pmix_ref, pscale_ref, w_bs_ref, w_bp_ref, w_out_ref, fgain_ref,
     y_ref, ore_ref, oim_ref, obuf_ref,
     sre_ref, sim_ref, ext_ref, bu_ref, u2_ref, xn_ref, us_ref, zs_ref, gs_ref, zp_ref,
     gp_ref, ys_ref, yp_ref) = refs
    m = n_t * n_b
    n_pairs = n_t // 2
    half = n_pairs * n_b
    step = pl.program_id(0)
    hist_rows = POOL_BUF * n_b

    def load_state():
        if zero_state:
            sre_ref[...] = jnp.zeros(sre_ref.shape, F32)
            sim_ref[...] = jnp.zeros(sim_ref.shape, F32)
            ext_ref[0:hist_rows, :] = jnp.zeros((hist_rows, D_MODEL), F32)
        else:
            sre_ref[...] = s0re_ref[...]
            sim_ref[...] = s0im_ref[...]
            ext_ref[0:hist_rows, :] = buf0_ref[...].reshape(hist_rows, D_MODEL)

    def rows_tb(src_ref):
        return jnp.swapaxes(src_ref[...], 0, 1).reshape(m, D_MODEL)

    def proj(k):
        return jnp.dot(xn_ref[...], w_in_ref[:, k * D_MODEL:(k + 1) * D_MODEL],
                       preferred_element_type=F32)

    def prepare_norm(src_ref):
        xn_ref[...] = _rmsnorm(rows_tb(src_ref), gain_ref[...]).astype(BF16)

    def prepare_u():
        us_ref[...] = proj(0).reshape(n_pairs, 2 * n_b, D_MODEL)

    def prepare_pairs():
        for blk in range(N_BLOCKS):
            lanes = slice(blk * LANES, (blk + 1) * LANES)
            st = slice(blk * BLOCK_STATE, (blk + 1) * BLOCK_STATE)
            u_t0 = us_ref[:, 0:n_b, lanes].reshape(half, LANES)
            u_t1 = us_ref[:, n_b:, lanes].reshape(half, LANES)
            u2_ref[blk] = jnp.concatenate([u_t0, u_t1], axis=1).astype(BF16)
            bu_ref[blk, 0:n_b, 0:BLOCK_STATE] = sre_ref[:, st]
            bu_ref[blk, 0:n_b, BLOCK_STATE:] = sim_ref[:, st]

    def prepare_slabs():
        for blk in range(N_BLOCKS):
            bu_ref[blk, n_b:, :] = jnp.dot(u2_ref[blk], wb_ref[blk], preferred_element_type=F32)

    def prepare(src_ref):
        prepare_norm(src_ref)
        prepare_u()
        prepare_pairs()
        prepare_slabs()

    if carry:
        @pl.when(step == 0)
        def _():
            load_state()
            prepare(prefix_ref)
    else:
        load_state()
        prepare(x_ref)

    piece = 2 * LANES

    def proj_piece(k, p):
        cols = slice(p * piece, (p + 1) * piece)
        v = jnp.dot(xn_ref[...], w_in_ref[:, k * D_MODEL + p * piece:k * D_MODEL + (p + 1) * piece],
                    preferred_element_type=F32)
        if k == 1:
            zs_ref[:, cols] = _silu(v)
        elif k == 4:
            gs_ref[:, cols] = _sigmoid(v + b_gate_ref[:, cols])
        elif k == 2:
            ext_ref[hist_rows:hist_rows + m, cols] = v
        elif k == 3:
            zp_ref[:, cols] = _silu(v)
        else:
            gp_ref[:, cols] = _sigmoid(
                v + b_gate_ref[:, D_MODEL + p * piece:D_MODEL + (p + 1) * piece])

    pieces = [(k, p) for k in (1, 4, 2, 3, 5) for p in range(D_MODEL // piece)]

    for blk in range(N_BLOCKS):
        lanes = slice(blk * LANES, (blk + 1) * LANES)
        st = slice(blk * BLOCK_STATE, (blk + 1) * BLOCK_STATE)
        for k, p in pieces[blk * len(pieces) // N_BLOCKS:(blk + 1) * len(pieces) // N_BLOCKS]:
            proj_piece(k, p)
        a_re = jnp.broadcast_to(a2_re_ref[:, st], (SUBLANES, BLOCK_STATE))
        a_im = jnp.broadcast_to(a2_im_ref[:, st], (SUBLANES, BLOCK_STATE))
        for j in range(n_b // SUBLANES):
            tile = slice(j * SUBLANES, (j + 1) * SUBLANES)
            s_re = sre_ref[tile, st]
            s_im = sim_ref[tile, st]
            for k in range(n_pairs):
                rows = slice((k + 1) * n_b + j * SUBLANES, (k + 1) * n_b + (j + 1) * SUBLANES)
                n_re = a_re * s_re - a_im * s_im + bu_ref[blk, rows, 0:BLOCK_STATE]
                n_im = a_re * s_im + a_im * s_re + bu_ref[blk, rows, BLOCK_STATE:]
                bu_ref[blk, rows, 0:BLOCK_STATE] = n_re
                bu_ref[blk, rows, BLOCK_STATE:] = n_im
                s_re, s_im = n_re, n_im
            sre_ref[tile, st] = s_re
            sim_ref[tile, st] = s_im
        y2 = (jnp.dot(bu_ref[blk, 0:half, :].astype(BF16), wc_ref[blk], preferred_element_type=F32)
              + jnp.dot(u2_ref[blk], wd_ref[blk], preferred_element_type=F32))
        ys_ref[:, 0:n_b, lanes] = y2[:, 0:LANES].reshape(n_pairs, n_b, LANES)
        ys_ref[:, n_b:, lanes] = y2[:, LANES:].reshape(n_pairs, n_b, LANES)

    if pos0 + 1 >= max(POOL_WINDOWS):
        pos = None
    else:
        t_loc = lax.broadcasted_iota(jnp.int32, (m, 1), 0) // n_b
        pos = pos0 + step * n_t + t_loc
    for gi, w in enumerate(POOL_WINDOWS):
        cols = slice(gi * POOL_GROUP_CH, (gi + 1) * POOL_GROUP_CH)
        s = ext_ref[(POOL_BUF + 1 - w) * n_b:hist_rows + m, cols]
        span = 1
        while span < w:
            keep = s.shape[0] - span * n_b
            s = s[:keep] + s[span * n_b:]
            span *= 2
        if pos is None:
            mean = s * (1.0 / w)
        else:
            cnt = jnp.maximum(jnp.minimum(pos + 1, w), 1)
            mean = s * (1.0 / cnt.astype(F32))
        pooled = mean - ext_ref[hist_rows:hist_rows + m, cols]
        yp_ref[:, cols] = jnp.dot(pooled.astype(BF16), pmix_ref[gi],
                                  preferred_element_type=F32) * pscale_ref[:, cols]
    for k in range(POOL_BUF):
        ext_ref[k * n_b:(k + 1) * n_b, :] = ext_ref[(k + n_t) * n_b:(k + n_t + 1) * n_b, :]

    y = ys_ref[...].reshape(m, D_MODEL) + d_ref[...] * us_ref[...].reshape(m, D_MODEL)
    y = jax.nn.gelu(y)
    b = yp_ref[...] * zp_ref[...]
    acc_p = jnp.dot(b.astype(BF16), w_bp_ref[...], preferred_element_type=F32)
    glu = jnp.dot(y.astype(BF16), w_glu_ref[...], preferred_element_type=F32)
    if carry:
        prepare_norm(x_next_ref)
    a = y * _sigmoid(glu + b_glu_ref[...]) * zs_ref[...]
    acc_s = jnp.dot(a.astype(BF16), w_bs_ref[...], preferred_element_type=F32)
    if carry:
        prepare_u()
        prepare_pairs()
    merged = gs_ref[...] * acc_s + gp_ref[...] * acc_p
    delta = jnp.dot(merged.astype(BF16), w_out_ref[...], preferred_element_type=F32)
    if carry:
        prepare_slabs()
    delta_bt = jnp.swapaxes(delta.reshape(n_t, n_b, D_MODEL), 0, 1)
    y_ref[...] = _rmsnorm(x_ref[...] + delta_bt, fgain_ref[...])

    def write_state():
        ore_ref[...] = sre_ref[...]
        oim_ref[...] = sim_ref[...]
        obuf_ref[...] = ext_ref[0:hist_rows, :].reshape(POOL_BUF, n_b, D_MODEL)

    if carry:
        pl.when(step == pl.num_programs(0) - 1)(write_state)
    else:
        write_state()


def _layer_call(x, state, weights, *, n_t, n_b, carry, pos0, name, prefix=None):
    batch, time, _ = x.shape
    m = n_t * n_b
    x_block = (n_b, n_t, D_MODEL)
    buf_block = (POOL_BUF, n_b, D_MODEL)
    if carry:
        assert batch == n_b and time % n_t == 0 and prefix.shape == x_block
        n_blocks = time // n_t
        n_steps = n_blocks + 1
        x_specs = [pl.BlockSpec(x_block, lambda i: (0, 0, 0)),
                   pl.BlockSpec(x_block, lambda i: (0, jnp.maximum(i - 1, 0), 0)),
                   pl.BlockSpec(x_block, lambda i: (0, jnp.minimum(i, n_blocks - 1), 0))]
        y_map = lambda i: (0, jnp.maximum(i - 1, 0), 0)
        per_step = lambda i: (0, 0)
        per_step3 = lambda i: (0, 0, 0)
        xs = (prefix, x, x)
    else:
        assert time == n_t and batch % n_b == 0
        n_steps = batch // n_b
        x_specs = [pl.BlockSpec(x_block, lambda i: (i, 0, 0))]
        y_map = lambda i: (i, 0, 0)
        per_step = lambda i: (i, 0)
        per_step3 = lambda i: (0, i, 0)
        xs = (x,)
    whole = pl.BlockSpec(memory_space=pltpu.VMEM)
    state_specs = [
        pl.BlockSpec((n_b, STATE_W), per_step),
        pl.BlockSpec((n_b, STATE_W), per_step),
        pl.BlockSpec(buf_block, per_step3),
    ]
    kernel = functools.partial(_layer_kernel, n_t=n_t, n_b=n_b, carry=carry, pos0=pos0,
                               zero_state=state is None)
    return pl.pallas_call(
        kernel,
        grid=(n_steps,),
        in_specs=x_specs + ([] if state is None else state_specs) + [whole] * len(weights),
        out_specs=[pl.BlockSpec(x_block, y_map)] + state_specs,
        out_shape=[
            jax.ShapeDtypeStruct((batch, time, D_MODEL), F32),
            jax.ShapeDtypeStruct((batch, STATE_W), F32),
            jax.ShapeDtypeStruct((batch, STATE_W), F32),
            jax.ShapeDtypeStruct((POOL_BUF, batch, D_MODEL), F32),
        ],
        scratch_shapes=[
            pltpu.VMEM((n_b, STATE_W), F32),
            pltpu.VMEM((n_b, STATE_W), F32),
            pltpu.VMEM(((POOL_BUF + n_t) * n_b, D_MODEL), F32),
            pltpu.VMEM((N_BLOCKS, m // 2 + n_b, 2 * BLOCK_STATE), F32),
            pltpu.VMEM((N_BLOCKS, m // 2, 2 * LANES), BF16),
            pltpu.VMEM((m, D_MODEL), BF16),
            pltpu.VMEM((n_t // 2, 2 * n_b, D_MODEL), F32),
            pltpu.VMEM((m, D_MODEL), F32),
            pltpu.VMEM((m, D_MODEL), F32),
            pltpu.VMEM((m, D_MODEL), F32),
            pltpu.VMEM((m, D_MODEL), F32),
            pltpu.VMEM((n_t // 2, 2 * n_b, D_MODEL), F32),
            pltpu.VMEM((m, D_MODEL), F32),
        ],
        compiler_params=pltpu.CompilerParams(
            dimension_semantics=("arbitrary",), vmem_limit_bytes=VMEM_LIMIT_BYTES),
        name=name,
    )(*xs, *(() if state is None else state), *weights)


PREP_BLOCKS_PER_STEP = 8


def _s5_weights_kernel(abar_re_ref, abar_im_ref, a2_re_ref, a2_im_ref, q_re_ref, q_im_ref,
                       bt_re_ref, bt_im_ref, c_re_ref, c_im_ref, wb_ref, wc_ref, wd_ref):
    gpb, n_c, n_p = GROUPS_PER_BLOCK, SSM_GROUP_CH, SSM_STATE
    lane = lax.broadcasted_iota(jnp.int32, (n_p, BLOCK_STATE), 1)
    sel = jnp.where(jnp.bitwise_and(lane, n_p - 1)
                    == lax.broadcasted_iota(jnp.int32, (n_p, BLOCK_STATE), 0), 1.0, 0.0).astype(BF16)
    same_group = (
        jnp.right_shift(lax.broadcasted_iota(jnp.int32, (LANES, BLOCK_STATE), 0),
                        n_c.bit_length() - 1)
        == jnp.right_shift(lax.broadcasted_iota(jnp.int32, (LANES, BLOCK_STATE), 1),
                           n_p.bit_length() - 1))
    same_group_cc = (
        jnp.right_shift(lax.broadcasted_iota(jnp.int32, (LANES, LANES), 0), n_c.bit_length() - 1)
        == jnp.right_shift(lax.broadcasted_iota(jnp.int32, (LANES, LANES), 1),
                           n_c.bit_length() - 1))

    def per_row(ref, blk):
        v = ref[blk]
        return jnp.broadcast_to(v[:, None, :], (gpb, n_c, n_p)).reshape(LANES, n_p)

    def expand(mat):
        tiled = jnp.dot(mat.astype(BF16), sel, preferred_element_type=F32)
        return jnp.where(same_group, tiled, 0.0)

    def gram(x, y):
        p = lax.dot_general(x, y, (((1,), (1,)), ((), ())), precision=lax.Precision.HIGHEST,
                            preferred_element_type=F32)
        return jnp.where(same_group_cc, p, 0.0)

    for blk in range(PREP_BLOCKS_PER_STEP):
        ar, ai = per_row(abar_re_ref, blk), per_row(abar_im_ref, blk)
        ar2, ai2 = per_row(a2_re_ref, blk), per_row(a2_im_ref, blk)
        qr, qi = per_row(q_re_ref, blk), per_row(q_im_ref, blk)
        bt_re, bt_im = bt_re_ref[blk], bt_im_ref[blk]
        c_re, c_im = c_re_ref[blk], c_im_ref[blk]
        bbar_re = qr * bt_re - qi * bt_im
        bbar_im = qr * bt_im + qi * bt_re
        ab_re = ar * bbar_re - ai * bbar_im
        ab_im = ar * bbar_im + ai * bbar_re
        ca_re = c_re * ar - c_im * ai
        ca_im = c_re * ai + c_im * ar
        ca2_re = c_re * ar2 - c_im * ai2
        ca2_im = c_re * ai2 + c_im * ar2
        cb = gram(bbar_re, c_re) - gram(bbar_im, c_im)
        cab = gram(bbar_re, ca_re) - gram(bbar_im, ca_im)

        wb_ref[blk, 0:LANES, 0:BLOCK_STATE] = expand(ab_re).astype(BF16)
        wb_ref[blk, 0:LANES, BLOCK_STATE:] = expand(ab_im).astype(BF16)
        wb_ref[blk, LANES:, 0:BLOCK_STATE] = expand(bbar_re).astype(BF16)
        wb_ref[blk, LANES:, BLOCK_STATE:] = expand(bbar_im).astype(BF16)
        wc_ref[blk, 0:BLOCK_STATE, 0:LANES] = expand(ca_re).T.astype(BF16)
        wc_ref[blk, 0:BLOCK_STATE, LANES:] = expand(ca2_re).T.astype(BF16)
        wc_ref[blk, BLOCK_STATE:, 0:LANES] = expand(-ca_im).T.astype(BF16)
        wc_ref[blk, BLOCK_STATE:, LANES:] = expand(-ca2_im).T.astype(BF16)
        wd_ref[blk, 0:LANES, 0:LANES] = cb.astype(BF16)
        wd_ref[blk, 0:LANES, LANES:] = cab.astype(BF16)
        wd_ref[blk, LANES:, 0:LANES] = jnp.zeros((LANES, LANES), BF16)
        wd_ref[blk, LANES:, LANES:] = cb.astype(BF16)


def _s5_weights_call(per_state, per_channel):
    gpb, per_step = GROUPS_PER_BLOCK, PREP_BLOCKS_PER_STEP
    ins = ([v.reshape(N_BLOCKS, gpb, SSM_STATE) for v in per_state]
           + [v.reshape(N_BLOCKS, LANES, SSM_STATE) for v in per_channel])
    in_specs = ([pl.BlockSpec((per_step, gpb, SSM_STATE), lambda i: (i, 0, 0))] * len(per_state)
                + [pl.BlockSpec((per_step, LANES, SSM_STATE), lambda i: (i, 0, 0))]
                * len(per_channel))
    shapes = [(2 * LANES, 2 * BLOCK_STATE), (2 * BLOCK_STATE, 2 * LANES), (2 * LANES, 2 * LANES)]
    return pl.pallas_call(
        _s5_weights_kernel,
        grid=(N_BLOCKS // per_step,), in_specs=in_specs,
        out_specs=[pl.BlockSpec((per_step,) + s, lambda i: (i, 0, 0)) for s in shapes],
        out_shape=[jax.ShapeDtypeStruct((N_BLOCKS,) + s, BF16) for s in shapes],
        name="s5_block_diag_weights",
    )(*ins)


def _ssm_weights(a_re, a_im, log_dt, b_re, b_im, c_re, c_im):
    dt = jnp.exp(log_dt)[:, None]
    mag = jnp.exp(dt * a_re)
    ang = dt * a_im
    abar_re = mag * jnp.cos(ang)
    abar_im = mag * jnp.sin(ang)
    den = a_re * a_re + a_im * a_im
    nr = abar_re - 1.0
    ni = abar_im
    q_re = (nr * a_re + ni * a_im) / den
    q_im = (ni * a_re - nr * a_im) / den
    a2_re = abar_re * abar_re - abar_im * abar_im
    a2_im = 2.0 * abar_re * abar_im
    wb, wc, wd = _s5_weights_call(
        (abar_re, abar_im, a2_re, a2_im, q_re, q_im),
        (jnp.swapaxes(b_re, 1, 2), jnp.swapaxes(b_im, 1, 2), c_re, c_im))
    return a2_re, a2_im, wb, wc, wd


PROMPT_T = 32
SAMPLE_B = 32


def kernel(x_prompt, x_sample, state_ssm_re, state_ssm_im, state_pool, meta_tokens, norm_gain, w_in, b_gate, ssm_a_re, ssm_a_im, ssm_log_dt, ssm_b_re, ssm_b_im, ssm_c_re, ssm_c_im, ssm_d, w_glu, b_glu, pool_mix, pool_scale, w_branch_ssm, w_branch_pool, w_out, final_norm_gain):
    batch, seq, _ = x_prompt.shape
    dec_batch, dec_seq, _ = x_sample.shape
    depth = norm_gain.shape[0]
    assert depth == 1
    l = 0
    a2_re, a2_im, wb, wc, wd = _ssm_weights(
        ssm_a_re[l], ssm_a_im[l], ssm_log_dt[l], ssm_b_re[l], ssm_b_im[l],
        ssm_c_re[l], ssm_c_im[l])
    row = lambda v: v.reshape(1, -1).astype(F32)
    weights = (
        row(norm_gain[l]), w_in[l].astype(BF16), row(b_gate[l]),
        row(a2_re), row(a2_im), wb, wc, wd, row(ssm_d[l]),
        w_glu[l].astype(BF16), row(b_glu[l]), pool_mix[l].astype(BF16), row(pool_scale[l]),
        w_branch_ssm[l].astype(BF16), w_branch_pool[l].astype(BF16), w_out[l].astype(BF16),
        row(final_norm_gain),
    )

    assert batch == SUBLANES and N_META <= PROMPT_T
    meta = jnp.broadcast_to(meta_tokens[None].astype(x_prompt.dtype), (batch, N_META, D_MODEL))
    prefix = jnp.concatenate(
        [jnp.zeros((batch, PROMPT_T - N_META, D_MODEL), x_prompt.dtype), meta], axis=1)
    y_prompt, p_re, p_im, p_buf = _layer_call(
        x_prompt, None, weights, prefix=prefix,
        n_t=PROMPT_T, n_b=batch, carry=True, pos0=N_META - PROMPT_T, name="prompt_layer")

    sample_state = (state_ssm_re[l].reshape(dec_batch, STATE_W),
                    state_ssm_im[l].reshape(dec_batch, STATE_W),
                    jnp.swapaxes(state_pool[l], 0, 1))
    y_sample, s_re, s_im, s_buf = _layer_call(
        x_sample, sample_state, weights,
        n_t=dec_seq, n_b=SAMPLE_B, carry=False, pos0=PAST_LEN, name="sample_layer")

    state_shape = (1, -1, SSM_GROUPS, SSM_STATE)
    return (y_prompt, y_sample,
            p_re.reshape(state_shape), p_im.reshape(state_shape), jnp.swapaxes(p_buf, 0, 1)[None],
            s_re.reshape(state_shape), s_im.reshape(state_shape), jnp.swapaxes(s_buf, 0, 1)[None])
```

```python
import functools

import jax
import jax.numpy as jnp
from jax import lax
from jax.experimental import pallas as pl
from jax.experimental.pallas import tpu as pltpu

D_MODEL = 1024
N_META = 16
SSM_GROUP_CH = 16
SSM_GROUPS = D_MODEL // SSM_GROUP_CH
SSM_STATE = 64
POOL_WINDOWS = (2, 4, 8, 16)
POOL_GROUP_CH = D_MODEL // len(POOL_WINDOWS)
POOL_BUF = max(POOL_WINDOWS) - 1
EPS = 1e-6
PAST_LEN = 16384

SUBLANES = 8
LANES = 128
GROUPS_PER_BLOCK = LANES // SSM_GROUP_CH
N_BLOCKS = SSM_GROUPS // GROUPS_PER_BLOCK
BLOCK_STATE = GROUPS_PER_BLOCK * SSM_STATE
STATE_W = SSM_GROUPS * SSM_STATE
VMEM_LIMIT_BYTES = 56 * 1024 * 1024

F32 = jnp.float32
BF16 = jnp.bfloat16


def _rmsnorm(x, gain):
    ms = jnp.mean(x * x, axis=-1, keepdims=True)
    return (x * lax.rsqrt(ms + EPS)) * gain


_sigmoid = jax.nn.sigmoid


def _silu(x):
    return x * _sigmoid(x)


def _layer_kernel(*refs, n_t, n_b, carry, pos0, zero_state):
    if carry:
        prefix_ref, x_ref, x_next_ref, *refs = refs
    else:
        x_ref, *refs = refs
    if not zero_state:
        s0re_ref, s0im_ref, buf0_ref, *refs = refs
    (gain_ref, w_in_ref, b_gate_ref, a2_re_ref, a2_im_ref, wb_ref, wc_ref, wd_ref, d_ref,
     w_glu_ref, b_glu_ref, pmix_ref, pscale_ref, w_bs_ref, w_bp_ref, w_out_ref, fgain_ref,
     y_ref, ore_ref, oim_ref, obuf_ref,
     sre_ref, sim_ref, ext_ref, bu_ref, u2_ref, xn_ref, us_ref, zs_ref, gs_ref, zp_ref,
     gp_ref, ys_ref, yp_ref) = refs
    m = n_t * n_b
    n_pairs = n_t // 2
    half = n_pairs * n_b
    step = pl.program_id(0)
    hist_rows = POOL_BUF * n_b

    def load_state():
        if zero_state:
            sre_ref[...] = jnp.zeros(sre_ref.shape, F32)
            sim_ref[...] = jnp.zeros(sim_ref.shape, F32)
            ext_ref[0:hist_rows, :] = jnp.zeros((hist_rows, D_MODEL), F32)
        else:
            sre_ref[...] = s0re_ref[...]
            sim_ref[...] = s0im_ref[...]
            ext_ref[0:hist_rows, :] = buf0_ref[...].reshape(hist_rows, D_MODEL)

    def rows_tb(src_ref):
        return jnp.swapaxes(src_ref[...], 0, 1).reshape(m, D_MODEL)

    def proj(k):
        return jnp.dot(xn_ref[...], w_in_ref[:, k * D_MODEL:(k + 1) * D_MODEL],
                       preferred_element_type=F32)

    def prepare_norm(src_ref):
        xn_ref[...] = _rmsnorm(rows_tb(src_ref), gain_ref[...]).astype(BF16)

    def prepare_u():
        us_ref[...] = proj(0).reshape(n_pairs, 2 * n_b, D_MODEL)

    def prepare_pairs():
        for blk in range(N_BLOCKS):
            lanes = slice(blk * LANES, (blk + 1) * LANES)
            st = slice(blk * BLOCK_STATE, (blk + 1) * BLOCK_STATE)
            u_t0 = us_ref[:, 0:n_b, lanes].reshape(half, LANES)
            u_t1 = us_ref[:, n_b:, lanes].reshape(half, LANES)
            u2_ref[blk] = jnp.concatenate([u_t0, u_t1], axis=1).astype(BF16)
            bu_ref[blk, 0:n_b, 0:BLOCK_STATE] = sre_ref[:, st]
            bu_ref[blk, 0:n_b, BLOCK_STATE:] = sim_ref[:, st]

    def prepare_slabs():
        for blk in range(N_BLOCKS):
            bu_ref[blk, n_b:, :] = jnp.dot(u2_ref[blk], wb_ref[blk], preferred_element_type=F32)

    def prepare(src_ref):
        prepare_norm(src_ref)
        prepare_u()
        prepare_pairs()
        prepare_slabs()

    if carry:
        @pl.when(step == 0)
        def _():
            load_state()
            prepare(prefix_ref)
    else:
        load_state()
        prepare(x_ref)

    piece = 2 * LANES

    def proj_piece(k, p):
        cols = slice(p * piece, (p + 1) * piece)
        v = jnp.dot(xn_ref[...], w_in_ref[:, k * D_MODEL + p * piece:k * D_MODEL + (p + 1) * piece],
                    preferred_element_type=F32)
        if k == 1:
            zs_ref[:, cols] = _silu(v)
        elif k == 4:
            gs_ref[:, cols] = _sigmoid(v + b_gate_ref[:, cols])
        elif k == 2:
            ext_ref[hist_rows:hist_rows + m, cols] = v
        elif k == 3:
            zp_ref[:, cols] = _silu(v)
        else:
            gp_ref[:, cols] = _sigmoid(
                v + b_gate_ref[:, D_MODEL + p * piece:D_MODEL + (p + 1) * piece])

    pieces = [(k, p) for k in (1, 4, 2, 3, 5) for p in range(D_MODEL // piece)]

    for blk in range(N_BLOCKS):
        lanes = slice(blk * LANES, (blk + 1) * LANES)
        st = slice(blk * BLOCK_STATE, (blk + 1) * BLOCK_STATE)
        for k, p in pieces[blk * len(pieces) // N_BLOCKS:(blk + 1) * len(pieces) // N_BLOCKS]:
            proj_piece(k, p)
        a_re = jnp.broadcast_to(a2_re_ref[:, st], (SUBLANES, BLOCK_STATE))
        a_im = jnp.broadcast_to(a2_im_ref[:, st], (SUBLANES, BLOCK_STATE))
        for j in range(n_b // SUBLANES):
            tile = slice(j * SUBLANES, (j + 1) * SUBLANES)
            s_re = sre_ref[tile, st]
            s_im = sim_ref[tile, st]
            for k in range(n_pairs):
                rows = slice((k + 1) * n_b + j * SUBLANES, (k + 1) * n_b + (j + 1) * SUBLANES)
                n_re = a_re * s_re - a_im * s_im + bu_ref[blk, rows, 0:BLOCK_STATE]
                n_im = a_re * s_im + a_im * s_re + bu_ref[blk, rows, BLOCK_STATE:]
                bu_ref[blk, rows, 0:BLOCK_STATE] = n_re
                bu_ref[blk, rows, BLOCK_STATE:] = n_im
                s_re, s_im = n_re, n_im
            sre_ref[tile, st] = s_re
            sim_ref[tile, st] = s_im
        y2 = (jnp.dot(bu_ref[blk, 0:half, :].astype(BF16), wc_ref[blk], preferred_element_type=F32)
              + jnp.dot(u2_ref[blk], wd_ref[blk], preferred_element_type=F32))
        ys_ref[:, 0:n_b, lanes] = y2[:, 0:LANES].reshape(n_pairs, n_b, LANES)
        ys_ref[:, n_b:, lanes] = y2[:, LANES:].reshape(n_pairs, n_b, LANES)

    if pos0 + 1 >= max(POOL_WINDOWS):
        pos = None
    else:
        t_loc = lax.broadcasted_iota(jnp.int32, (m, 1), 0) // n_b
        pos = pos0 + step * n_t + t_loc
    for gi, w in enumerate(POOL_WINDOWS):
        cols = slice(gi * POOL_GROUP_CH, (gi + 1) * POOL_GROUP_CH)
        s = ext_ref[(POOL_BUF + 1 - w) * n_b:hist_rows + m, cols]
        span = 1
        while span < w:
            keep = s.shape[0] - span * n_b
            s = s[:keep] + s[span * n_b:]
            span *= 2
        if pos is None:
            mean = s * (1.0 / w)
        else:
            cnt = jnp.maximum(jnp.minimum(pos + 1, w), 1)
            mean = s * (1.0 / cnt.astype(F32))
        pooled = mean - ext_ref[hist_rows:hist_rows + m, cols]
        yp_ref[:, cols] = jnp.dot(pooled.astype(BF16), pmix_ref[gi],
                                  preferred_element_type=F32) * pscale_ref[:, cols]
    for k in range(POOL_BUF):
        ext_ref[k * n_b:(k + 1) * n_b, :] = ext_ref[(k + n_t) * n_b:(k + n_t + 1) * n_b, :]

    y = ys_ref[...].reshape(m, D_MODEL) + d_ref[...] * us_ref[...].reshape(m, D_MODEL)
    y = jax.nn.gelu(y)
    b = yp_ref[...] * zp_ref[...]
    acc_p = jnp.dot(b.astype(BF16), w_bp_ref[...], preferred_element_type=F32)
    glu = jnp.dot(y.astype(BF16), w_glu_ref[...], preferred_element_type=F32)
    if carry:
        prepare_norm(x_next_ref)
    a = y * _sigmoid(glu + b_glu_ref[...]) * zs_ref[...]
    acc_s = jnp.dot(a.astype(BF16), w_bs_ref[...], preferred_element_type=F32)
    if carry:
        prepare_u()
        prepare_pairs()
    merged = gs_ref[...] * acc_s + gp_ref[...] * acc_p
    delta = jnp.dot(merged.astype(BF16), w_out_ref[...], preferred_element_type=F32)
    if carry:
        prepare_slabs()
    delta_bt = jnp.swapaxes(delta.reshape(n_t, n_b, D_MODEL), 0, 1)
    y_ref[...] = _rmsnorm(x_ref[...] + delta_bt, fgain_ref[...])

    def write_state():
        ore_ref[...] = sre_ref[...]
        oim_ref[...] = sim_ref[...]
        obuf_ref[...] = ext_ref[0:hist_rows, :].reshape(POOL_BUF, n_b, D_MODEL)

    if carry:
        pl.when(step == pl.num_programs(0) - 1)(write_state)
    else:
        write_state()


def _layer_call(x, state, weights, *, n_t, n_b, carry, pos0, name, prefix=None):
    batch, time, _ = x.shape
    m = n_t * n_b
    x_block = (n_b, n_t, D_MODEL)
    buf_block = (POOL_BUF, n_b, D_MODEL)
    if carry:
        assert batch == n_b and time % n_t == 0 and prefix.shape == x_block
        n_blocks = time // n_t
        n_steps = n_blocks + 1
        x_specs = [pl.BlockSpec(x_block, lambda i: (0, 0, 0)),
                   pl.BlockSpec(x_block, lambda i: (0, jnp.maximum(i - 1, 0), 0)),
                   pl.BlockSpec(x_block, lambda i: (0, jnp.minimum(i, n_blocks - 1), 0))]
        y_map = lambda i: (0, jnp.maximum(i - 1, 0), 0)
        per_step = lambda i: (0, 0)
        per_step3 = lambda i: (0, 0, 0)
        xs = (prefix, x, x)
    else:
        assert time == n_t and batch % n_b == 0
        n_steps = batch // n_b
        x_specs = [pl.BlockSpec(x_block, lambda i: (i, 0, 0))]
        y_map = lambda i: (i, 0, 0)
        per_step = lambda i: (i, 0)
        per_step3 = lambda i: (0, i, 0)
        xs = (x,)
    whole = pl.BlockSpec(memory_space=pltpu.VMEM)
    state_specs = [
        pl.BlockSpec((n_b, STATE_W), per_step),
        pl.BlockSpec((n_b, STATE_W), per_step),
        pl.BlockSpec(buf_block, per_step3),
    ]
    kernel = functools.partial(_layer_kernel, n_t=n_t, n_b=n_b, carry=carry, pos0=pos0,
                               zero_state=state is None)
    return pl.pallas_call(
        kernel,
        grid=(n_steps,),
        in_specs=x_specs + ([] if state is None else state_specs) + [whole] * len(weights),
        out_specs=[pl.BlockSpec(x_block, y_map)] + state_specs,
        out_shape=[
            jax.ShapeDtypeStruct((batch, time, D_MODEL), F32),
            jax.ShapeDtypeStruct((batch, STATE_W), F32),
            jax.ShapeDtypeStruct((batch, STATE_W), F32),
            jax.ShapeDtypeStruct((POOL_BUF, batch, D_MODEL), F32),
        ],
        scratch_shapes=[
            pltpu.VMEM((n_b, STATE_W), F32),
            pltpu.VMEM((n_b, STATE_W), F32),
            pltpu.VMEM(((POOL_BUF + n_t) * n_b, D_MODEL), F32),
            pltpu.VMEM((N_BLOCKS, m // 2 + n_b, 2 * BLOCK_STATE), F32),
            pltpu.VMEM((N_BLOCKS, m // 2, 2 * LANES), BF16),
            pltpu.VMEM((m, D_MODEL), BF16),
            pltpu.VMEM((n_t // 2, 2 * n_b, D_MODEL), F32),
            pltpu.VMEM((m, D_MODEL), F32),
            pltpu.VMEM((m, D_MODEL), F32),
            pltpu.VMEM((m, D_MODEL), F32),
            pltpu.VMEM((m, D_MODEL), F32),
            pltpu.VMEM((n_t // 2, 2 * n_b, D_MODEL), F32),
            pltpu.VMEM((m, D_MODEL), F32),
        ],
        compiler_params=pltpu.CompilerParams(
            dimension_semantics=("arbitrary",), vmem_limit_bytes=VMEM_LIMIT_BYTES),
        name=name,
    )(*xs, *(() if state is None else state), *weights)


PREP_BLOCKS_PER_STEP = 2


def _s5_weights_kernel(abar_re_ref, abar_im_ref, a2_re_ref, a2_im_ref, q_re_ref, q_im_ref,
                       bt_re_ref, bt_im_ref, c_re_ref, c_im_ref, wb_ref, wc_ref, wd_ref):
    gpb, n_c, n_p = GROUPS_PER_BLOCK, SSM_GROUP_CH, SSM_STATE
    lane = lax.broadcasted_iota(jnp.int32, (n_p, BLOCK_STATE), 1)
    sel = jnp.where(jnp.bitwise_and(lane, n_p - 1)
                    == lax.broadcasted_iota(jnp.int32, (n_p, BLOCK_STATE), 0), 1.0, 0.0).astype(BF16)
    same_group = (
        jnp.right_shift(lax.broadcasted_iota(jnp.int32, (LANES, BLOCK_STATE), 0),
                        n_c.bit_length() - 1)
        == jnp.right_shift(lax.broadcasted_iota(jnp.int32, (LANES, BLOCK_STATE), 1),
                           n_p.bit_length() - 1))
    same_group_cc = (
        jnp.right_shift(lax.broadcasted_iota(jnp.int32, (LANES, LANES), 0), n_c.bit_length() - 1)
        == jnp.right_shift(lax.broadcasted_iota(jnp.int32, (LANES, LANES), 1),
                           n_c.bit_length() - 1))

    def per_row(ref, blk):
        v = ref[blk]
        return jnp.broadcast_to(v[:, None, :], (gpb, n_c, n_p)).reshape(LANES, n_p)

    def expand(mat):
        tiled = jnp.dot(mat.astype(BF16), sel, preferred_element_type=F32)
        return jnp.where(same_group, tiled, 0.0)

    def gram(x, y):
        p = lax.dot_general(x, y, (((1,), (1,)), ((), ())), precision=lax.Precision.HIGHEST,
                            preferred_element_type=F32)
        return jnp.where(same_group_cc, p, 0.0)

    for blk in range(PREP_BLOCKS_PER_STEP):
        ar, ai = per_row(abar_re_ref, blk), per_row(abar_im_ref, blk)
        ar2, ai2 = per_row(a2_re_ref, blk), per_row(a2_im_ref, blk)
        qr, qi = per_row(q_re_ref, blk), per_row(q_im_ref, blk)
        bt_re, bt_im = bt_re_ref[blk], bt_im_ref[blk]
        c_re, c_im = c_re_ref[blk], c_im_ref[blk]
        bbar_re = qr * bt_re - qi * bt_im
        bbar_im = qr * bt_im + qi * bt_re
        ab_re = ar * bbar_re - ai * bbar_im
        ab_im = ar * bbar_im + ai * bbar_re
        ca_re = c_re * ar - c_im * ai
        ca_im = c_re * ai + c_im * ar
        ca2_re = c_re * ar2 - c_im * ai2
        ca2_im = c_re * ai2 + c_im * ar2
        cb = gram(bbar_re, c_re) - gram(bbar_im, c_im)
        cab = gram(bbar_re, ca_re) - gram(bbar_im, ca_im)

        wb_ref[blk, 0:LANES, 0:BLOCK_STATE] = expand(ab_re).astype(BF16)
        wb_ref[blk, 0:LANES, BLOCK_STATE:] = expand(ab_im).astype(BF16)
        wb_ref[blk, LANES:, 0:BLOCK_STATE] = expand(bbar_re).astype(BF16)
        wb_ref[blk, LANES:, BLOCK_STATE:] = expand(bbar_im).astype(BF16)
        wc_ref[blk, 0:BLOCK_STATE, 0:LANES] = expand(ca_re).T.astype(BF16)
        wc_ref[blk, 0:BLOCK_STATE, LANES:] = expand(ca2_re).T.astype(BF16)
        wc_ref[blk, BLOCK_STATE:, 0:LANES] = expand(-ca_im).T.astype(BF16)
        wc_ref[blk, BLOCK_STATE:, LANES:] = expand(-ca2_im).T.astype(BF16)
        wd_ref[blk, 0:LANES, 0:LANES] = cb.astype(BF16)
        wd_ref[blk, 0:LANES, LANES:] = cab.astype(BF16)
        wd_ref[blk, LANES:, 0:LANES] = jnp.zeros((LANES, LANES), BF16)
        wd_ref[blk, LANES:, LANES:] = cb.astype(BF16)


def _s5_weights_call(per_state, per_channel):
    gpb, per_step = GROUPS_PER_BLOCK, PREP_BLOCKS_PER_STEP
    ins = ([v.reshape(N_BLOCKS, gpb, SSM_STATE) for v in per_state]
           + [v.reshape(N_BLOCKS, LANES, SSM_STATE) for v in per_channel])
    in_specs = ([pl.BlockSpec((per_step, gpb, SSM_STATE), lambda i: (i, 0, 0))] * len(per_state)
                + [pl.BlockSpec((per_step, LANES, SSM_STATE), lambda i: (i, 0, 0))]
                * len(per_channel))
    shapes = [(2 * LANES, 2 * BLOCK_STATE), (2 * BLOCK_STATE, 2 * LANES), (2 * LANES, 2 * LANES)]
    return pl.pallas_call(
        _s5_weights_kernel,
        grid=(N_BLOCKS // per_step,), in_specs=in_specs,
        out_specs=[pl.BlockSpec((per_step,) + s, lambda i: (i, 0, 0)) for s in shapes],
        out_shape=[jax.ShapeDtypeStruct((N_BLOCKS,) + s, BF16) for s in shapes],
        name="s5_block_diag_weights",
    )(*ins)


def _ssm_weights(a_re, a_im, log_dt, b_re, b_im, c_re, c_im):
    dt = jnp.exp(log_dt)[:, None]
    mag = jnp.exp(dt * a_re)
    ang = dt * a_im
    abar_re = mag * jnp.cos(ang)
    abar_im = mag * jnp.sin(ang)
    den = a_re * a_re + a_im * a_im
    nr = abar_re - 1.0
    ni = abar_im
    q_re = (nr * a_re + ni * a_im) / den
    q_im = (ni * a_re - nr * a_im) / den
    a2_re = abar_re * abar_re - abar_im * abar_im
    a2_im = 2.0 * abar_re * abar_im
    wb, wc, wd = _s5_weights_call(
        (abar_re, abar_im, a2_re, a2_im, q_re, q_im),
        (jnp.swapaxes(b_re, 1, 2), jnp.swapaxes(b_im, 1, 2), c_re, c_im))
    return a2_re, a2_im, wb, wc, wd


PROMPT_T = 32
SAMPLE_B = 32


def kernel(x_prompt, x_sample, state_ssm_re, state_ssm_im, state_pool, meta_tokens, norm_gain, w_in, b_gate, ssm_a_re, ssm_a_im, ssm_log_dt, ssm_b_re, ssm_b_im, ssm_c_re, ssm_c_im, ssm_d, w_glu, b_glu, pool_mix, pool_scale, w_branch_ssm, w_branch_pool, w_out, final_norm_gain):
    batch, seq, _ = x_prompt.shape
    dec_batch, dec_seq, _ = x_sample.shape
    depth = norm_gain.shape[0]
    assert depth == 1
    l = 0
    a2_re, a2_im, wb, wc, wd = _ssm_weights(
        ssm_a_re[l], ssm_a_im[l], ssm_log_dt[l], ssm_b_re[l], ssm_b_im[l],
        ssm_c_re[l], ssm_c_im[l])
    row = lambda v: v.reshape(1, -1).astype(F32)
    weights = (
        row(norm_gain[l]), w_in[l].astype(BF16), row(b_gate[l]),
        row(a2_re), row(a2_im), wb, wc, wd, row(ssm_d[l]),
        w_glu[l].astype(BF16), row(b_glu[l]), pool_mix[l].astype(BF16), row(pool_scale[l]),
        w_branch_ssm[l].astype(BF16), w_branch_pool[l].astype(BF16), w_out[l].astype(BF16),
        row(final_norm_gain),
    )

    assert batch == SUBLANES and N_META <= PROMPT_T
    meta = jnp.broadcast_to(meta_tokens[None].astype(x_prompt.dtype), (batch, N_META, D_MODEL))
    prefix = jnp.concatenate(
        [jnp.zeros((batch, PROMPT_T - N_META, D_MODEL), x_prompt.dtype), meta], axis=1)
    y_prompt, p_re, p_im, p_buf = _layer_call(
        x_prompt, None, weights, prefix=prefix,
        n_t=PROMPT_T, n_b=batch, carry=True, pos0=N_META - PROMPT_T, name="prompt_layer")

    sample_state = (state_ssm_re[l].reshape(dec_batch, STATE_W),
                    state_ssm_im[l].reshape(dec_batch, STATE_W),
                    jnp.swapaxes(state_pool[l], 0, 1))
    y_sample, s_re, s_im, s_buf = _layer_call(
        x_sample, sample_state, weights,
        n_t=dec_seq, n_b=SAMPLE_B, carry=False, pos0=PAST_LEN, name="sample_layer")

    state_shape = (1, -1, SSM_GROUPS, SSM_STATE)
    return (y_prompt, y_sample,
            p_re.reshape(state_shape), p_im.reshape(state_shape), jnp.swapaxes(p_buf, 0, 1)[None],
            s_re.reshape(state_shape), s_im.reshape(state_shape), jnp.swapaxes(s_buf, 0, 1)[None])
```

```python
import functools

import jax
import jax.numpy as jnp
from jax import lax
from jax.experimental import pallas as pl
from jax.experimental.pallas import tpu as pltpu

D_MODEL = 1024
N_META = 16
SSM_GROUP_CH = 16
SSM_GROUPS = D_MODEL // SSM_GROUP_CH
SSM_STATE = 64
POOL_WINDOWS = (2, 4, 8, 16)
POOL_GROUP_CH = D_MODEL // len(POOL_WINDOWS)
POOL_BUF = max(POOL_WINDOWS) - 1
EPS = 1e-6
PAST_LEN = 16384

SUBLANES = 8
LANES = 128
GROUPS_PER_BLOCK = LANES // SSM_GROUP_CH
N_BLOCKS = SSM_GROUPS // GROUPS_PER_BLOCK
BLOCK_STATE = GROUPS_PER_BLOCK * SSM_STATE
STATE_W = SSM_GROUPS * SSM_STATE
VMEM_LIMIT_BYTES = 56 * 1024 * 1024

F32 = jnp.float32
BF16 = jnp.bfloat16


def _rmsnorm(x, gain):
    ms = jnp.mean(x * x, axis=-1, keepdims=True)
    return (x * lax.rsqrt(ms + EPS)) * gain


_sigmoid = jax.nn.sigmoid


def _silu(x):
    return x * _sigmoid(x)


def _layer_kernel(*refs, n_t, n_b, carry, pos0, zero_state):
    if carry:
        prefix_ref, x_ref, x_next_ref, *refs = refs
    else:
        x_ref, *refs = refs
    if not zero_state:
        s0re_ref, s0im_ref, buf0_ref, *refs = refs
    (gain_ref, w_in_ref, b_gate_ref, a2_re_ref, a2_im_ref, wb_ref, wc_ref, wd_ref, d_ref,
     w_glu_ref, b_glu_ref, pmix_ref, pscale_ref, w_bs_ref, w_bp_ref, w_out_ref, fgain_ref,
     y_ref, ore_ref, oim_ref, obuf_ref,
     sre_ref, sim_ref, ext_ref, bu_ref, u2_ref, xn_ref, us_ref, zs_ref, gs_ref, zp_ref,
     gp_ref, ys_ref, yp_ref) = refs
    m = n_t * n_b
    n_pairs = n_t // 2
    half = n_pairs * n_b
    step = pl.program_id(0)
    hist_rows = POOL_BUF * n_b

    def load_state():
        if zero_state:
            sre_ref[...] = jnp.zeros(sre_ref.shape, F32)
            sim_ref[...] = jnp.zeros(sim_ref.shape, F32)
            ext_ref[0:hist_rows, :] = jnp.zeros((hist_rows, D_MODEL), F32)
        else:
            sre_ref[...] = s0re_ref[...]
            sim_ref[...] = s0im_ref[...]
            ext_ref[0:hist_rows, :] = buf0_ref[...].reshape(hist_rows, D_MODEL)

    def rows_tb(src_ref):
        return jnp.swapaxes(src_ref[...], 0, 1).reshape(m, D_MODEL)

    def proj(k):
        return jnp.dot(xn_ref[...], w_in_ref[:, k * D_MODEL:(k + 1) * D_MODEL],
                       preferred_element_type=F32)

    def prepare_norm(src_ref):
        xn_ref[...] = _rmsnorm(rows_tb(src_ref), gain_ref[...]).astype(BF16)

    def prepare_u():
        us_ref[...] = proj(0).reshape(n_pairs, 2 * n_b, D_MODEL)

    def prepare_pairs():
        for blk in range(N_BLOCKS):
            lanes = slice(blk * LANES, (blk + 1) * LANES)
            st = slice(blk * BLOCK_STATE, (blk + 1) * BLOCK_STATE)
            u_t0 = us_ref[:, 0:n_b, lanes].reshape(half, LANES)
            u_t1 = us_ref[:, n_b:, lanes].reshape(half, LANES)
            u2_ref[blk] = jnp.concatenate([u_t0, u_t1], axis=1).astype(BF16)
            bu_ref[blk, 0:n_b, 0:BLOCK_STATE] = sre_ref[:, st]
            bu_ref[blk, 0:n_b, BLOCK_STATE:] = sim_ref[:, st]

    def prepare_slabs():
        for blk in range(N_BLOCKS):
            bu_ref[blk, n_b:, :] = jnp.dot(u2_ref[blk], wb_ref[blk], preferred_element_type=F32)

    def prepare(src_ref):
        prepare_norm(src_ref)
        prepare_u()
        prepare_pairs()
        prepare_slabs()

    if carry:
        @pl.when(step == 0)
        def _():
            load_state()
            prepare(prefix_ref)
    else:
        load_state()
        prepare(x_ref)

    piece = 2 * LANES

    def proj_piece(k, p):
        cols = slice(p * piece, (p + 1) * piece)
        v = jnp.dot(xn_ref[...], w_in_ref[:, k * D_MODEL + p * piece:k * D_MODEL + (p + 1) * piece],
                    preferred_element_type=F32)
        if k == 1:
            zs_ref[:, cols] = _silu(v)
        elif k == 4:
            gs_ref[:, cols] = _sigmoid(v + b_gate_ref[:, cols])
        elif k == 2:
            ext_ref[hist_rows:hist_rows + m, cols] = v
        elif k == 3:
            zp_ref[:, cols] = _silu(v)
        else:
            gp_ref[:, cols] = _sigmoid(
                v + b_gate_ref[:, D_MODEL + p * piece:D_MODEL + (p + 1) * piece])

    pieces = [(k, p) for k in (1, 4, 2, 3, 5) for p in range(D_MODEL // piece)]

    for blk in range(N_BLOCKS):
        lanes = slice(blk * LANES, (blk + 1) * LANES)
        st = slice(blk * BLOCK_STATE, (blk + 1) * BLOCK_STATE)
        for k, p in pieces[blk * len(pieces) // N_BLOCKS:(blk + 1) * len(pieces) // N_BLOCKS]:
            proj_piece(k, p)
        a_re = jnp.broadcast_to(a2_re_ref[:, st], (SUBLANES, BLOCK_STATE))
        a_im = jnp.broadcast_to(a2_im_ref[:, st], (SUBLANES, BLOCK_STATE))
        for j in range(n_b // SUBLANES):
            tile = slice(j * SUBLANES, (j + 1) * SUBLANES)
            s_re = sre_ref[tile, st]
            s_im = sim_ref[tile, st]
            for k in range(n_pairs):
                rows = slice((k + 1) * n_b + j * SUBLANES, (k + 1) * n_b + (j + 1) * SUBLANES)
                n_re = a_re * s_re - a_im * s_im + bu_ref[blk, rows, 0:BLOCK_STATE]
                n_im = a_re * s_im + a_im * s_re + bu_ref[blk, rows, BLOCK_STATE:]
                bu_ref[blk, rows, 0:BLOCK_STATE] = n_re
                bu_ref[blk, rows, BLOCK_STATE:] = n_im
                s_re, s_im = n_re, n_im
            sre_ref[tile, st] = s_re
            sim_ref[tile, st] = s_im
        y2 = (jnp.dot(bu_ref[blk, 0:half, :].astype(BF16), wc_ref[blk], preferred_element_type=F32)
              + jnp.dot(u2_ref[blk], wd_ref[blk], preferred_element_type=F32))
        ys_ref[:, 0:n_b, lanes] = y2[:, 0:LANES].reshape(n_pairs, n_b, LANES)
        ys_ref[:, n_b:, lanes] = y2[:, LANES:].reshape(n_pairs, n_b, LANES)

    if pos0 + 1 >= max(POOL_WINDOWS):
        pos = None
    else:
        t_loc = lax.broadcasted_iota(jnp.int32, (m, 1), 0) // n_b
        pos = pos0 + step * n_t + t_loc
    for gi, w in enumerate(POOL_WINDOWS):
        cols = slice(gi * POOL_GROUP_CH, (gi + 1) * POOL_GROUP_CH)
        s = ext_ref[(POOL_BUF + 1 - w) * n_b:hist_rows + m, cols]
        span = 1
        while span < w:
            keep = s.shape[0] - span * n_b
            s = s[:keep] + s[span * n_b:]
            span *= 2
        if pos is None:
            mean = s * (1.0 / w)
        else:
            cnt = jnp.maximum(jnp.minimum(pos + 1, w), 1)
            mean = s * (1.0 / cnt.astype(F32))
        pooled = mean - ext_ref[hist_rows:hist_rows + m, cols]
        yp_ref[:, cols] = jnp.dot(pooled.astype(BF16), pmix_ref[gi],
                                  preferred_element_type=F32) * pscale_ref[:, cols]
    for k in range(POOL_BUF):
        ext_ref[k * n_b:(k + 1) * n_b, :] = ext_ref[(k + n_t) * n_b:(k + n_t + 1) * n_b, :]

    y = ys_ref[...].reshape(m, D_MODEL) + d_ref[...] * us_ref[...].reshape(m, D_MODEL)
    y = jax.nn.gelu(y)
    b = yp_ref[...] * zp_ref[...]
    acc_p = jnp.dot(b.astype(BF16), w_bp_ref[...], preferred_element_type=F32)
    glu = jnp.dot(y.astype(BF16), w_glu_ref[...], preferred_element_type=F32)
    if carry:
        prepare_norm(x_next_ref)
    a = y * _sigmoid(glu + b_glu_ref[...]) * zs_ref[...]
    acc_s = jnp.dot(a.astype(BF16), w_bs_ref[...], preferred_element_type=F32)
    if carry:
        prepare_u()
        prepare_pairs()
    merged = gs_ref[...] * acc_s + gp_ref[...] * acc_p
    delta = jnp.dot(merged.astype(BF16), w_out_ref[...], preferred_element_type=F32)
    if carry:
        prepare_slabs()
    delta_bt = jnp.swapaxes(delta.reshape(n_t, n_b, D_MODEL), 0, 1)
    y_ref[...] = _rmsnorm(x_ref[...] + delta_bt, fgain_ref[...])

    def write_state():
        ore_ref[...] = sre_ref[...]
        oim_ref[...] = sim_ref[...]
        obuf_ref[...] = ext_ref[0:hist_rows, :].reshape(POOL_BUF, n_b, D_MODEL)

    if carry:
        pl.when(step == pl.num_programs(0) - 1)(write_state)
    else:
        write_state()


def _layer_call(x, state, weights, *, n_t, n_b, carry, pos0, name, prefix=None):
    batch, time, _ = x.shape
    m = n_t * n_b
    x_block = (n_b, n_t, D_MODEL)
    buf_block = (POOL_BUF, n_b, D_MODEL)
    if carry:
        assert batch == n_b and time % n_t == 0 and prefix.shape == x_block
        n_blocks = time // n_t
        n_steps = n_blocks + 1
        x_specs = [pl.BlockSpec(x_block, lambda i: (0, 0, 0)),
                   pl.BlockSpec(x_block, lambda i: (0, jnp.maximum(i - 1, 0), 0)),
                   pl.BlockSpec(x_block, lambda i: (0, jnp.minimum(i, n_blocks - 1), 0))]
        y_map = lambda i: (0, jnp.maximum(i - 1, 0), 0)
        per_step = lambda i: (0, 0)
        per_step3 = lambda i: (0, 0, 0)
        xs = (prefix, x, x)
    else:
        assert time == n_t and batch % n_b == 0
        n_steps = batch // n_b
        x_specs = [pl.BlockSpec(x_block, lambda i: (i, 0, 0))]
        y_map = lambda i: (i, 0, 0)
        per_step = lambda i: (i, 0)
        per_step3 = lambda i: (0, i, 0)
        xs = (x,)
    whole = pl.BlockSpec(memory_space=pltpu.VMEM)
    state_specs = [
        pl.BlockSpec((n_b, STATE_W), per_step),
        pl.BlockSpec((n_b, STATE_W), per_step),
        pl.BlockSpec(buf_block, per_step3),
    ]
    kernel = functools.partial(_layer_kernel, n_t=n_t, n_b=n_b, carry=carry, pos0=pos0,
                               zero_state=state is None)
    return pl.pallas_call(
        kernel,
        grid=(n_steps,),
        in_specs=x_specs + ([] if state is None else state_specs) + [whole] * len(weights),
        out_specs=[pl.BlockSpec(x_block, y_map)] + state_specs,
        out_shape=[
            jax.ShapeDtypeStruct((batch, time, D_MODEL), F32),
            jax.ShapeDtypeStruct((batch, STATE_W), F32),
            jax.ShapeDtypeStruct((batch, STATE_W), F32),
            jax.ShapeDtypeStruct((POOL_BUF, batch, D_MODEL), F32),
        ],
        scratch_shapes=[
            pltpu.VMEM((n_b, STATE_W), F32),
            pltpu.VMEM((n_b, STATE_W), F32),
            pltpu.VMEM(((POOL_BUF + n_t) * n_b, D_MODEL), F32),
            pltpu.VMEM((N_BLOCKS, m // 2 + n_b, 2 * BLOCK_STATE), F32),
            pltpu.VMEM((N_BLOCKS, m // 2, 2 * LANES), BF16),
            pltpu.VMEM((m, D_MODEL), BF16),
            pltpu.VMEM((n_t // 2, 2 * n_b, D_MODEL), F32),
            pltpu.VMEM((m, D_MODEL), F32),
            pltpu.VMEM((m, D_MODEL), F32),
            pltpu.VMEM((m, D_MODEL), F32),
            pltpu.VMEM((m, D_MODEL), F32),
            pltpu.VMEM((n_t // 2, 2 * n_b, D_MODEL), F32),
            pltpu.VMEM((m, D_MODEL), F32),
        ],
        compiler_params=pltpu.CompilerParams(
            dimension_semantics=("arbitrary",), vmem_limit_bytes=VMEM_LIMIT_BYTES),
        name=name,
    )(*xs, *(() if state is None else state), *weights)


PREP_BLOCKS_PER_STEP = 2


def _s5_weights_kernel(abar_re_ref, abar_im_ref, a2_re_ref, a2_im_ref, q_re_ref, q_im_ref,
                       bt_re_ref, bt_im_ref, c_re_ref, c_im_ref, sre_t_ref, sim_t_ref,
                       wb_ref, wc_ref, wd_ref, sre_ref, sim_ref):
    gpb, n_c, n_p = GROUPS_PER_BLOCK, SSM_GROUP_CH, SSM_STATE
    lane = lax.broadcasted_iota(jnp.int32, (n_p, BLOCK_STATE), 1)
    sel = jnp.where(jnp.bitwise_and(lane, n_p - 1)
                    == lax.broadcasted_iota(jnp.int32, (n_p, BLOCK_STATE), 0), 1.0, 0.0).astype(BF16)
    same_group = (
        jnp.right_shift(lax.broadcasted_iota(jnp.int32, (LANES, BLOCK_STATE), 0),
                        n_c.bit_length() - 1)
        == jnp.right_shift(lax.broadcasted_iota(jnp.int32, (LANES, BLOCK_STATE), 1),
                           n_p.bit_length() - 1))
    same_group_cc = (
        jnp.right_shift(lax.broadcasted_iota(jnp.int32, (LANES, LANES), 0), n_c.bit_length() - 1)
        == jnp.right_shift(lax.broadcasted_iota(jnp.int32, (LANES, LANES), 1),
                           n_c.bit_length() - 1))

    def per_row(ref, blk):
        v = ref[blk]
        return jnp.broadcast_to(v[:, None, :], (gpb, n_c, n_p)).reshape(LANES, n_p)

    def expand(mat):
        tiled = jnp.dot(mat.astype(BF16), sel, preferred_element_type=F32)
        return jnp.where(same_group, tiled, 0.0)

    def gram(x, y):
        p = lax.dot_general(x, y, (((1,), (1,)), ((), ())), precision=lax.Precision.HIGHEST,
                            preferred_element_type=F32)
        return jnp.where(same_group_cc, p, 0.0)

    for blk in range(PREP_BLOCKS_PER_STEP):
        ar, ai = per_row(abar_re_ref, blk), per_row(abar_im_ref, blk)
        ar2, ai2 = per_row(a2_re_ref, blk), per_row(a2_im_ref, blk)
        qr, qi = per_row(q_re_ref, blk), per_row(q_im_ref, blk)
        bt_re, bt_im = bt_re_ref[blk], bt_im_ref[blk]
        c_re, c_im = c_re_ref[blk], c_im_ref[blk]
        bbar_re = qr * bt_re - qi * bt_im
        bbar_im = qr * bt_im + qi * bt_re
        ab_re = ar * bbar_re - ai * bbar_im
        ab_im = ar * bbar_im + ai * bbar_re
        ca_re = c_re * ar - c_im * ai
        ca_im = c_re * ai + c_im * ar
        ca2_re = c_re * ar2 - c_im * ai2
        ca2_im = c_re * ai2 + c_im * ar2
        cb = gram(bbar_re, c_re) - gram(bbar_im, c_im)
        cab = gram(bbar_re, ca_re) - gram(bbar_im, ca_im)

        wb_ref[blk, 0:LANES, 0:BLOCK_STATE] = expand(ab_re).astype(BF16)
        wb_ref[blk, 0:LANES, BLOCK_STATE:] = expand(ab_im).astype(BF16)
        wb_ref[blk, LANES:, 0:BLOCK_STATE] = expand(bbar_re).astype(BF16)
        wb_ref[blk, LANES:, BLOCK_STATE:] = expand(bbar_im).astype(BF16)
        wc_ref[blk, 0:BLOCK_STATE, 0:LANES] = expand(ca_re).T.astype(BF16)
        wc_ref[blk, 0:BLOCK_STATE, LANES:] = expand(ca2_re).T.astype(BF16)
        wc_ref[blk, BLOCK_STATE:, 0:LANES] = expand(-ca_im).T.astype(BF16)
        wc_ref[blk, BLOCK_STATE:, LANES:] = expand(-ca2_im).T.astype(BF16)
        wd_ref[blk, 0:LANES, 0:LANES] = cb.astype(BF16)
        wd_ref[blk, 0:LANES, LANES:] = cab.astype(BF16)
        wd_ref[blk, LANES:, 0:LANES] = jnp.zeros((LANES, LANES), BF16)
        wd_ref[blk, LANES:, LANES:] = cb.astype(BF16)

    sre_ref[...] = sre_t_ref[...].T
    sim_ref[...] = sim_t_ref[...].T


def _s5_weights_call(per_state, per_channel, states_t):
    gpb, per_step = GROUPS_PER_BLOCK, PREP_BLOCKS_PER_STEP
    n_steps = N_BLOCKS // per_step
    batch = states_t[0].shape[1]
    slab = STATE_W // n_steps
    ins = ([v.reshape(N_BLOCKS, gpb, SSM_STATE) for v in per_state]
           + [v.reshape(N_BLOCKS, LANES, SSM_STATE) for v in per_channel] + list(states_t))
    in_specs = ([pl.BlockSpec((per_step, gpb, SSM_STATE), lambda i: (i, 0, 0))] * len(per_state)
                + [pl.BlockSpec((per_step, LANES, SSM_STATE), lambda i: (i, 0, 0))]
                * len(per_channel)
                + [pl.BlockSpec((slab, batch), lambda i: (i, 0))] * len(states_t))
    shapes = [(2 * LANES, 2 * BLOCK_STATE), (2 * BLOCK_STATE, 2 * LANES), (2 * LANES, 2 * LANES)]
    return pl.pallas_call(
        _s5_weights_kernel,
        grid=(n_steps,), in_specs=in_specs,
        out_specs=([pl.BlockSpec((per_step,) + s, lambda i: (i, 0, 0)) for s in shapes]
                   + [pl.BlockSpec((batch, slab), lambda i: (0, i))] * len(states_t)),
        out_shape=([jax.ShapeDtypeStruct((N_BLOCKS,) + s, BF16) for s in shapes]
                   + [jax.ShapeDtypeStruct((batch, STATE_W), F32)] * len(states_t)),
        name="s5_block_diag_weights",
    )(*ins)


def _ssm_weights(a_re, a_im, log_dt, b_re, b_im, c_re, c_im, states_t):
    dt = jnp.exp(log_dt)[:, None]
    mag = jnp.exp(dt * a_re)
    ang = dt * a_im
    abar_re = mag * jnp.cos(ang)
    abar_im = mag * jnp.sin(ang)
    den = a_re * a_re + a_im * a_im
    nr = abar_re - 1.0
    ni = abar_im
    q_re = (nr * a_re + ni * a_im) / den
    q_im = (ni * a_re - nr * a_im) / den
    a2_re = abar_re * abar_re - abar_im * abar_im
    a2_im = 2.0 * abar_re * abar_im
    wb, wc, wd, *states = _s5_weights_call(
        (abar_re, abar_im, a2_re, a2_im, q_re, q_im),
        (jnp.swapaxes(b_re, 1, 2), jnp.swapaxes(b_im, 1, 2), c_re, c_im), states_t)
    return a2_re, a2_im, wb, wc, wd, states


PROMPT_T = 32
SAMPLE_B = 32


def kernel(x_prompt, x_sample, state_ssm_re, state_ssm_im, state_pool, meta_tokens, norm_gain, w_in, b_gate, ssm_a_re, ssm_a_im, ssm_log_dt, ssm_b_re, ssm_b_im, ssm_c_re, ssm_c_im, ssm_d, w_glu, b_glu, pool_mix, pool_scale, w_branch_ssm, w_branch_pool, w_out, final_norm_gain):
    batch, seq, _ = x_prompt.shape
    dec_batch, dec_seq, _ = x_sample.shape
    depth = norm_gain.shape[0]
    assert depth == 1
    l = 0
    states_t = [jnp.transpose(s[l], (1, 2, 0)).reshape(STATE_W, dec_batch)
                for s in (state_ssm_re, state_ssm_im)]
    a2_re, a2_im, wb, wc, wd, (s0_re, s0_im) = _ssm_weights(
        ssm_a_re[l], ssm_a_im[l], ssm_log_dt[l], ssm_b_re[l], ssm_b_im[l],
        ssm_c_re[l], ssm_c_im[l], states_t)
    row = lambda v: v.reshape(1, -1).astype(F32)
    weights = (
        row(norm_gain[l]), w_in[l].astype(BF16), row(b_gate[l]),
        row(a2_re), row(a2_im), wb, wc, wd, row(ssm_d[l]),
        w_glu[l].astype(BF16), row(b_glu[l]), pool_mix[l].astype(BF16), row(pool_scale[l]),
        w_branch_ssm[l].astype(BF16), w_branch_pool[l].astype(BF16), w_out[l].astype(BF16),
        row(final_norm_gain),
    )

    assert batch == SUBLANES and N_META <= PROMPT_T
    meta = jnp.broadcast_to(meta_tokens[None].astype(x_prompt.dtype), (batch, N_META, D_MODEL))
    prefix = jnp.concatenate(
        [jnp.zeros((batch, PROMPT_T - N_META, D_MODEL), x_prompt.dtype), meta], axis=1)
    y_prompt, p_re, p_im, p_buf = _layer_call(
        x_prompt, None, weights, prefix=prefix,
        n_t=PROMPT_T, n_b=batch, carry=True, pos0=N_META - PROMPT_T, name="prompt_layer")

    sample_state = (s0_re, s0_im, jnp.swapaxes(state_pool[l], 0, 1))
    y_sample, s_re, s_im, s_buf = _layer_call(
        x_sample, sample_state, weights,
        n_t=dec_seq, n_b=SAMPLE_B, carry=False, pos0=PAST_LEN, name="sample_layer")

    state_shape = (1, -1, SSM_GROUPS, SSM_STATE)
    return (y_prompt, y_sample,
            p_re.reshape(state_shape), p_im.reshape(state_shape), jnp.swapaxes(p_buf, 0, 1)[None],
            s_re.reshape(state_shape), s_im.reshape(state_shape), jnp.swapaxes(s_buf, 0, 1)[None])
```

```python
import functools

import jax
import jax.numpy as jnp
from jax import lax
from jax.experimental import pallas as pl
from jax.experimental.pallas import tpu as pltpu

D_MODEL = 1024
N_META = 16
SSM_GROUP_CH = 16
SSM_GROUPS = D_MODEL // SSM_GROUP_CH
SSM_STATE = 64
POOL_WINDOWS = (2, 4, 8, 16)
POOL_GROUP_CH = D_MODEL // len(POOL_WINDOWS)
POOL_BUF = max(POOL_WINDOWS) - 1
EPS = 1e-6
PAST_LEN = 16384

SUBLANES = 8
LANES = 128
GROUPS_PER_BLOCK = LANES // SSM_GROUP_CH
N_BLOCKS = SSM_GROUPS // GROUPS_PER_BLOCK
BLOCK_STATE = GROUPS_PER_BLOCK * SSM_STATE
STATE_W = SSM_GROUPS * SSM_STATE
VMEM_LIMIT_BYTES = 56 * 1024 * 1024

F32 = jnp.float32
BF16 = jnp.bfloat16


def _rmsnorm(x, gain):
    ms = jnp.mean(x * x, axis=-1, keepdims=True)
    return (x * lax.rsqrt(ms + EPS)) * gain


_sigmoid = jax.nn.sigmoid


def _silu(x):
    return x * _sigmoid(x)


def _layer_kernel(*refs, n_t, n_b, carry, pos0, zero_state):
    if carry:
        prefix_ref, x_ref, x_next_ref, *refs = refs
    else:
        x_ref, *refs = refs
    if not zero_state:
        s0re_ref, s0im_ref, buf0_ref, *refs = refs
    (gain_ref, w_in_ref, b_gate_ref, a2_re_ref, a2_im_ref, wb_ref, wc_ref, wd_ref, d_ref,
     w_glu_ref, b_glu_ref, pmix_ref, pscale_ref, w_bs_ref, w_bp_ref, w_out_ref, fgain_ref,
     y_ref, ore_ref, oim_ref, obuf_ref,
     sre_ref, sim_ref, ext_ref, bu_ref, u2_ref, xn_ref, us_ref, zs_ref, gs_ref, zp_ref,
     gp_ref, ys_ref, yp_ref) = refs
    m = n_t * n_b
    n_pairs = n_t // 2
    half = n_pairs * n_b
    step = pl.program_id(0)
    hist_rows = POOL_BUF * n_b

    def load_state():
        if zero_state:
            sre_ref[...] = jnp.zeros(sre_ref.shape, F32)
            sim_ref[...] = jnp.zeros(sim_ref.shape, F32)
            ext_ref[0:hist_rows, :] = jnp.zeros((hist_rows, D_MODEL), F32)
        else:
            sre_ref[...] = s0re_ref[...]
            sim_ref[...] = s0im_ref[...]
            ext_ref[0:hist_rows, :] = buf0_ref[...].reshape(hist_rows, D_MODEL)

    def rows_tb(src_ref):
        return jnp.swapaxes(src_ref[...], 0, 1).reshape(m, D_MODEL)

    def proj(k):
        return jnp.dot(xn_ref[...], w_in_ref[:, k * D_MODEL:(k + 1) * D_MODEL],
                       preferred_element_type=F32)

    def prepare_norm_rows(h):
        xn_ref[...] = _rmsnorm(h, gain_ref[...]).astype(BF16)

    def prepare_norm(src_ref):
        prepare_norm_rows(rows_tb(src_ref))

    def prefix_rows():
        tok = prefix_ref[...]
        n_tok = tok.shape[0]
        rows = jnp.broadcast_to(tok[:, None, :], (n_tok, n_b, D_MODEL)).reshape(n_tok * n_b, D_MODEL)
        return jnp.concatenate([jnp.zeros((m - n_tok * n_b, D_MODEL), F32), rows], axis=0)

    def prepare_u():
        us_ref[...] = proj(0).reshape(n_pairs, 2 * n_b, D_MODEL)

    def prepare_pairs():
        for blk in range(N_BLOCKS):
            lanes = slice(blk * LANES, (blk + 1) * LANES)
            st = slice(blk * BLOCK_STATE, (blk + 1) * BLOCK_STATE)
            u_t0 = us_ref[:, 0:n_b, lanes].reshape(half, LANES)
            u_t1 = us_ref[:, n_b:, lanes].reshape(half, LANES)
            u2_ref[blk] = jnp.concatenate([u_t0, u_t1], axis=1).astype(BF16)
            bu_ref[blk, 0:n_b, 0:BLOCK_STATE] = sre_ref[:, st]
            bu_ref[blk, 0:n_b, BLOCK_STATE:] = sim_ref[:, st]

    def prepare_slabs():
        for blk in range(N_BLOCKS):
            bu_ref[blk, n_b:, :] = jnp.dot(u2_ref[blk], wb_ref[blk], preferred_element_type=F32)

    def prepare(src_ref):
        prepare_norm(src_ref)
        prepare_u()
        prepare_pairs()
        prepare_slabs()

    if carry:
        @pl.when(step == 0)
        def _():
            load_state()
            prepare_norm_rows(prefix_rows())
            prepare_u()
            prepare_pairs()
            prepare_slabs()
    else:
        load_state()
        prepare(x_ref)

    piece = 2 * LANES

    def proj_piece(k, p):
        cols = slice(p * piece, (p + 1) * piece)
        v = jnp.dot(xn_ref[...], w_in_ref[:, k * D_MODEL + p * piece:k * D_MODEL + (p + 1) * piece],
                    preferred_element_type=F32)
        if k == 1:
            zs_ref[:, cols] = _silu(v)
        elif k == 4:
            gs_ref[:, cols] = _sigmoid(v + b_gate_ref[:, cols])
        elif k == 2:
            ext_ref[hist_rows:hist_rows + m, cols] = v
        elif k == 3:
            zp_ref[:, cols] = _silu(v)
        else:
            gp_ref[:, cols] = _sigmoid(
                v + b_gate_ref[:, D_MODEL + p * piece:D_MODEL + (p + 1) * piece])

    pieces = [(k, p) for k in (1, 4, 2, 3, 5) for p in range(D_MODEL // piece)]

    for blk in range(N_BLOCKS):
        lanes = slice(blk * LANES, (blk + 1) * LANES)
        st = slice(blk * BLOCK_STATE, (blk + 1) * BLOCK_STATE)
        for k, p in pieces[blk * len(pieces) // N_BLOCKS:(blk + 1) * len(pieces) // N_BLOCKS]:
            proj_piece(k, p)
        a_re = jnp.broadcast_to(a2_re_ref[:, st], (SUBLANES, BLOCK_STATE))
        a_im = jnp.broadcast_to(a2_im_ref[:, st], (SUBLANES, BLOCK_STATE))
        for j in range(n_b // SUBLANES):
            tile = slice(j * SUBLANES, (j + 1) * SUBLANES)
            s_re = sre_ref[tile, st]
            s_im = sim_ref[tile, st]
            for k in range(n_pairs):
                rows = slice((k + 1) * n_b + j * SUBLANES, (k + 1) * n_b + (j + 1) * SUBLANES)
                n_re = a_re * s_re - a_im * s_im + bu_ref[blk, rows, 0:BLOCK_STATE]
                n_im = a_re * s_im + a_im * s_re + bu_ref[blk, rows, BLOCK_STATE:]
                bu_ref[blk, rows, 0:BLOCK_STATE] = n_re
                bu_ref[blk, rows, BLOCK_STATE:] = n_im
                s_re, s_im = n_re, n_im
            sre_ref[tile, st] = s_re
            sim_ref[tile, st] = s_im
        y2 = (jnp.dot(bu_ref[blk, 0:half, :].astype(BF16), wc_ref[blk], preferred_element_type=F32)
              + jnp.dot(u2_ref[blk], wd_ref[blk], preferred_element_type=F32))
        ys_ref[:, 0:n_b, lanes] = y2[:, 0:LANES].reshape(n_pairs, n_b, LANES)
        ys_ref[:, n_b:, lanes] = y2[:, LANES:].reshape(n_pairs, n_b, LANES)

    if pos0 + 1 >= max(POOL_WINDOWS):
        pos = None
    else:
        t_loc = lax.broadcasted_iota(jnp.int32, (m, 1), 0) // n_b
        pos = pos0 + step * n_t + t_loc
    for gi, w in enumerate(POOL_WINDOWS):
        cols = slice(gi * POOL_GROUP_CH, (gi + 1) * POOL_GROUP_CH)
        s = ext_ref[(POOL_BUF + 1 - w) * n_b:hist_rows + m, cols]
        span = 1
        while span < w:
            keep = s.shape[0] - span * n_b
            s = s[:keep] + s[span * n_b:]
            span *= 2
        if pos is None:
            mean = s * (1.0 / w)
        else:
            cnt = jnp.maximum(jnp.minimum(pos + 1, w), 1)
            mean = s * (1.0 / cnt.astype(F32))
        pooled = mean - ext_ref[hist_rows:hist_rows + m, cols]
        yp_ref[:, cols] = jnp.dot(pooled.astype(BF16), pmix_ref[gi],
                                  preferred_element_type=F32) * pscale_ref[:, cols]
    for k in range(POOL_BUF):
        ext_ref[k * n_b:(k + 1) * n_b, :] = ext_ref[(k + n_t) * n_b:(k + n_t + 1) * n_b, :]

    y = ys_ref[...].reshape(m, D_MODEL) + d_ref[...] * us_ref[...].reshape(m, D_MODEL)
    y = jax.nn.gelu(y)
    b = yp_ref[...] * zp_ref[...]
    acc_p = jnp.dot(b.astype(BF16), w_bp_ref[...], preferred_element_type=F32)
    glu = jnp.dot(y.astype(BF16), w_glu_ref[...], preferred_element_type=F32)
    if carry:
        prepare_norm(x_next_ref)
    a = y * _sigmoid(glu + b_glu_ref[...]) * zs_ref[...]
    acc_s = jnp.dot(a.astype(BF16), w_bs_ref[...], preferred_element_type=F32)
    if carry:
        prepare_u()
        prepare_pairs()
    merged = gs_ref[...] * acc_s + gp_ref[...] * acc_p
    delta = jnp.dot(merged.astype(BF16), w_out_ref[...], preferred_element_type=F32)
    if carry:
        prepare_slabs()
    delta_bt = jnp.swapaxes(delta.reshape(n_t, n_b, D_MODEL), 0, 1)
    y_ref[...] = _rmsnorm(x_ref[...] + delta_bt, fgain_ref[...])

    def write_state():
        if carry:
            for g in range(SSM_GROUPS):
                ore_ref[:, g, :] = sre_ref[:, g * SSM_STATE:(g + 1) * SSM_STATE]
                oim_ref[:, g, :] = sim_ref[:, g * SSM_STATE:(g + 1) * SSM_STATE]
        else:
            ore_ref[...] = sre_ref[...]
            oim_ref[...] = sim_ref[...]
        obuf_ref[...] = ext_ref[0:hist_rows, :].reshape(POOL_BUF, n_b, D_MODEL)

    if carry:
        pl.when(step == pl.num_programs(0) - 1)(write_state)
    else:
        write_state()


def _layer_call(x, state, weights, *, n_t, n_b, carry, pos0, name, prefix=None):
    batch, time, _ = x.shape
    m = n_t * n_b
    x_block = (n_b, n_t, D_MODEL)
    buf_block = (POOL_BUF, n_b, D_MODEL)
    if carry:
        assert batch == n_b and time % n_t == 0 and prefix.shape[0] <= n_t
        n_blocks = time // n_t
        n_steps = n_blocks + 1
        x_specs = [pl.BlockSpec(prefix.shape, lambda i: (0, 0)),
                   pl.BlockSpec(x_block, lambda i: (0, jnp.maximum(i - 1, 0), 0)),
                   pl.BlockSpec(x_block, lambda i: (0, jnp.minimum(i, n_blocks - 1), 0))]
        y_map = lambda i: (0, jnp.maximum(i - 1, 0), 0)
        per_step = lambda i: (0, 0)
        per_step3 = lambda i: (0, 0, 0)
        xs = (prefix, x, x)
        state_out = (batch, SSM_GROUPS, SSM_STATE)
        state_out_spec = pl.BlockSpec(state_out, per_step3)
    else:
        assert time == n_t and batch % n_b == 0
        n_steps = batch // n_b
        x_specs = [pl.BlockSpec(x_block, lambda i: (i, 0, 0))]
        y_map = lambda i: (i, 0, 0)
        per_step = lambda i: (i, 0)
        per_step3 = lambda i: (0, i, 0)
        xs = (x,)
        state_out = (batch, STATE_W)
        state_out_spec = pl.BlockSpec((n_b, STATE_W), per_step)
    whole = pl.BlockSpec(memory_space=pltpu.VMEM)
    state_specs = [
        pl.BlockSpec((n_b, STATE_W), per_step),
        pl.BlockSpec((n_b, STATE_W), per_step),
        pl.BlockSpec(buf_block, per_step3),
    ]
    kernel = functools.partial(_layer_kernel, n_t=n_t, n_b=n_b, carry=carry, pos0=pos0,
                               zero_state=state is None)
    return pl.pallas_call(
        kernel,
        grid=(n_steps,),
        in_specs=x_specs + ([] if state is None else state_specs) + [whole] * len(weights),
        out_specs=[pl.BlockSpec(x_block, y_map), state_out_spec, state_out_spec, state_specs[2]],
        out_shape=[
            jax.ShapeDtypeStruct((batch, time, D_MODEL), F32),
            jax.ShapeDtypeStruct(state_out, F32),
            jax.ShapeDtypeStruct(state_out, F32),
            jax.ShapeDtypeStruct((POOL_BUF, batch, D_MODEL), F32),
        ],
        scratch_shapes=[
            pltpu.VMEM((n_b, STATE_W), F32),
            pltpu.VMEM((n_b, STATE_W), F32),
            pltpu.VMEM(((POOL_BUF + n_t) * n_b, D_MODEL), F32),
            pltpu.VMEM((N_BLOCKS, m // 2 + n_b, 2 * BLOCK_STATE), F32),
            pltpu.VMEM((N_BLOCKS, m // 2, 2 * LANES), BF16),
            pltpu.VMEM((m, D_MODEL), BF16),
            pltpu.VMEM((n_t // 2, 2 * n_b, D_MODEL), F32),
            pltpu.VMEM((m, D_MODEL), F32),
            pltpu.VMEM((m, D_MODEL), F32),
            pltpu.VMEM((m, D_MODEL), F32),
            pltpu.VMEM((m, D_MODEL), F32),
            pltpu.VMEM((n_t // 2, 2 * n_b, D_MODEL), F32),
            pltpu.VMEM((m, D_MODEL), F32),
        ],
        compiler_params=pltpu.CompilerParams(
            dimension_semantics=("arbitrary",), vmem_limit_bytes=VMEM_LIMIT_BYTES),
        name=name,
    )(*xs, *(() if state is None else state), *weights)


PREP_BLOCKS_PER_STEP = 2


def _s5_weights_kernel(abar_re_ref, abar_im_ref, a2_re_ref, a2_im_ref, q_re_ref, q_im_ref,
                       bt_re_ref, bt_im_ref, c_re_ref, c_im_ref, sre_t_ref, sim_t_ref,
                       wb_ref, wc_ref, wd_ref, sre_ref, sim_ref):
    gpb, n_c, n_p = GROUPS_PER_BLOCK, SSM_GROUP_CH, SSM_STATE
    lane = lax.broadcasted_iota(jnp.int32, (n_p, BLOCK_STATE), 1)
    sel = jnp.where(jnp.bitwise_and(lane, n_p - 1)
                    == lax.broadcasted_iota(jnp.int32, (n_p, BLOCK_STATE), 0), 1.0, 0.0).astype(BF16)
    same_group = (
        jnp.right_shift(lax.broadcasted_iota(jnp.int32, (LANES, BLOCK_STATE), 0),
                        n_c.bit_length() - 1)
        == jnp.right_shift(lax.broadcasted_iota(jnp.int32, (LANES, BLOCK_STATE), 1),
                           n_p.bit_length() - 1))
    same_group_cc = (
        jnp.right_shift(lax.broadcasted_iota(jnp.int32, (LANES, LANES), 0), n_c.bit_length() - 1)
        == jnp.right_shift(lax.broadcasted_iota(jnp.int32, (LANES, LANES), 1),
                           n_c.bit_length() - 1))

    def per_row(ref, blk):
        v = ref[blk]
        return jnp.broadcast_to(v[:, None, :], (gpb, n_c, n_p)).reshape(LANES, n_p)

    def expand(mat):
        tiled = jnp.dot(mat.astype(BF16), sel, preferred_element_type=F32)
        return jnp.where(same_group, tiled, 0.0)

    def gram(x, y):
        p = lax.dot_general(x, y, (((1,), (1,)), ((), ())), precision=lax.Precision.HIGHEST,
                            preferred_element_type=F32)
        return jnp.where(same_group_cc, p, 0.0)

    for blk in range(PREP_BLOCKS_PER_STEP):
        ar, ai = per_row(abar_re_ref, blk), per_row(abar_im_ref, blk)
        ar2, ai2 = per_row(a2_re_ref, blk), per_row(a2_im_ref, blk)
        qr, qi = per_row(q_re_ref, blk), per_row(q_im_ref, blk)
        bt_re, bt_im = bt_re_ref[blk], bt_im_ref[blk]
        c_re, c_im = c_re_ref[blk], c_im_ref[blk]
        bbar_re = qr * bt_re - qi * bt_im
        bbar_im = qr * bt_im + qi * bt_re
        ab_re = ar * bbar_re - ai * bbar_im
        ab_im = ar * bbar_im + ai * bbar_re
        ca_re = c_re * ar - c_im * ai
        ca_im = c_re * ai + c_im * ar
        ca2_re = c_re * ar2 - c_im * ai2
        ca2_im = c_re * ai2 + c_im * ar2
        cb = gram(bbar_re, c_re) - gram(bbar_im, c_im)
        cab = gram(bbar_re, ca_re) - gram(bbar_im, ca_im)

        wb_ref[blk, 0:LANES, 0:BLOCK_STATE] = expand(ab_re).astype(BF16)
        wb_ref[blk, 0:LANES, BLOCK_STATE:] = expand(ab_im).astype(BF16)
        wb_ref[blk, LANES:, 0:BLOCK_STATE] = expand(bbar_re).astype(BF16)
        wb_ref[blk, LANES:, BLOCK_STATE:] = expand(bbar_im).astype(BF16)
        wc_ref[blk, 0:BLOCK_STATE, 0:LANES] = expand(ca_re).T.astype(BF16)
        wc_ref[blk, 0:BLOCK_STATE, LANES:] = expand(ca2_re).T.astype(BF16)
        wc_ref[blk, BLOCK_STATE:, 0:LANES] = expand(-ca_im).T.astype(BF16)
        wc_ref[blk, BLOCK_STATE:, LANES:] = expand(-ca2_im).T.astype(BF16)
        wd_ref[blk, 0:LANES, 0:LANES] = cb.astype(BF16)
        wd_ref[blk, 0:LANES, LANES:] = cab.astype(BF16)
        wd_ref[blk, LANES:, 0:LANES] = jnp.zeros((LANES, LANES), BF16)
        wd_ref[blk, LANES:, LANES:] = cb.astype(BF16)

    sre_ref[...] = sre_t_ref[...].T
    sim_ref[...] = sim_t_ref[...].T


def _s5_weights_call(per_state, per_channel, states_t):
    gpb, per_step = GROUPS_PER_BLOCK, PREP_BLOCKS_PER_STEP
    n_steps = N_BLOCKS // per_step
    batch = states_t[0].shape[1]
    slab = STATE_W // n_steps
    ins = ([v.reshape(N_BLOCKS, gpb, SSM_STATE) for v in per_state]
           + [v.reshape(N_BLOCKS, LANES, SSM_STATE) for v in per_channel] + list(states_t))
    in_specs = ([pl.BlockSpec((per_step, gpb, SSM_STATE), lambda i: (i, 0, 0))] * len(per_state)
                + [pl.BlockSpec((per_step, LANES, SSM_STATE), lambda i: (i, 0, 0))]
                * len(per_channel)
                + [pl.BlockSpec((slab, batch), lambda i: (i, 0))] * len(states_t))
    shapes = [(2 * LANES, 2 * BLOCK_STATE), (2 * BLOCK_STATE, 2 * LANES), (2 * LANES, 2 * LANES)]
    return pl.pallas_call(
        _s5_weights_kernel,
        grid=(n_steps,), in_specs=in_specs,
        out_specs=([pl.BlockSpec((per_step,) + s, lambda i: (i, 0, 0)) for s in shapes]
                   + [pl.BlockSpec((batch, slab), lambda i: (0, i))] * len(states_t)),
        out_shape=([jax.ShapeDtypeStruct((N_BLOCKS,) + s, BF16) for s in shapes]
                   + [jax.ShapeDtypeStruct((batch, STATE_W), F32)] * len(states_t)),
        name="s5_block_diag_weights",
    )(*ins)


def _ssm_weights(a_re, a_im, log_dt, b_re, b_im, c_re, c_im, states_t):
    dt = jnp.exp(log_dt)[:, None]
    mag = jnp.exp(dt * a_re)
    ang = dt * a_im
    abar_re = mag * jnp.cos(ang)
    abar_im = mag * jnp.sin(ang)
    den = a_re * a_re + a_im * a_im
    nr = abar_re - 1.0
    ni = abar_im
    q_re = (nr * a_re + ni * a_im) / den
    q_im = (ni * a_re - nr * a_im) / den
    a2_re = abar_re * abar_re - abar_im * abar_im
    a2_im = 2.0 * abar_re * abar_im
    wb, wc, wd, *states = _s5_weights_call(
        (abar_re, abar_im, a2_re, a2_im, q_re, q_im),
        (jnp.swapaxes(b_re, 1, 2), jnp.swapaxes(b_im, 1, 2), c_re, c_im), states_t)
    return a2_re, a2_im, wb, wc, wd, states


PROMPT_T = 32
SAMPLE_B = 32


def kernel(x_prompt, x_sample, state_ssm_re, state_ssm_im, state_pool, meta_tokens, norm_gain, w_in, b_gate, ssm_a_re, ssm_a_im, ssm_log_dt, ssm_b_re, ssm_b_im, ssm_c_re, ssm_c_im, ssm_d, w_glu, b_glu, pool_mix, pool_scale, w_branch_ssm, w_branch_pool, w_out, final_norm_gain):
    batch, seq, _ = x_prompt.shape
    dec_batch, dec_seq, _ = x_sample.shape
    depth = norm_gain.shape[0]
    assert depth == 1
    l = 0
    states_t = [jnp.transpose(s[l], (1, 2, 0)).reshape(STATE_W, dec_batch)
                for s in (state_ssm_re, state_ssm_im)]
    a2_re, a2_im, wb, wc, wd, (s0_re, s0_im) = _ssm_weights(
        ssm_a_re[l], ssm_a_im[l], ssm_log_dt[l], ssm_b_re[l], ssm_b_im[l],
        ssm_c_re[l], ssm_c_im[l], states_t)
    row = lambda v: v.reshape(1, -1).astype(F32)
    weights = (
        row(norm_gain[l]), w_in[l].astype(BF16), row(b_gate[l]),
        row(a2_re), row(a2_im), wb, wc, wd, row(ssm_d[l]),
        w_glu[l].astype(BF16), row(b_glu[l]), pool_mix[l].astype(BF16), row(pool_scale[l]),
        w_branch_ssm[l].astype(BF16), w_branch_pool[l].astype(BF16), w_out[l].astype(BF16),
        row(final_norm_gain),
    )

    assert batch == SUBLANES and N_META <= PROMPT_T
    y_prompt, p_re, p_im, p_buf = _layer_call(
        x_prompt, None, weights, prefix=meta_tokens.astype(x_prompt.dtype),
        n_t=PROMPT_T, n_b=batch, carry=True, pos0=N_META - PROMPT_T, name="prompt_layer")

    sample_state = (s0_re, s0_im, jnp.swapaxes(state_pool[l], 0, 1))
    y_sample, s_re, s_im, s_buf = _layer_call(
        x_sample, sample_state, weights,
        n_t=dec_seq, n_b=SAMPLE_B, carry=False, pos0=PAST_LEN, name="sample_layer")

    state_shape = (1, -1, SSM_GROUPS, SSM_STATE)
    return (y_prompt, y_sample,
            p_re[None], p_im[None], jnp.swapaxes(p_buf, 0, 1)[None],
            s_re.reshape(state_shape), s_im.reshape(state_shape), jnp.swapaxes(s_buf, 0, 1)[None])
```

```python
import functools

import jax
import jax.numpy as jnp
from jax import lax
from jax.experimental import pallas as pl
from jax.experimental.pallas import tpu as pltpu

D_MODEL = 1024
N_META = 16
SSM_GROUP_CH = 16
SSM_GROUPS = D_MODEL // SSM_GROUP_CH
SSM_STATE = 64
POOL_WINDOWS = (2, 4, 8, 16)
POOL_GROUP_CH = D_MODEL // len(POOL_WINDOWS)
POOL_BUF = max(POOL_WINDOWS) - 1
EPS = 1e-6
PAST_LEN = 16384

SUBLANES = 8
LANES = 128
GROUPS_PER_BLOCK = LANES // SSM_GROUP_CH
N_BLOCKS = SSM_GROUPS // GROUPS_PER_BLOCK
BLOCK_STATE = GROUPS_PER_BLOCK * SSM_STATE
STATE_W = SSM_GROUPS * SSM_STATE
VMEM_LIMIT_BYTES = 56 * 1024 * 1024

F32 = jnp.float32
BF16 = jnp.bfloat16


def _rmsnorm(x, gain):
    ms = jnp.mean(x * x, axis=-1, keepdims=True)
    return (x * lax.rsqrt(ms + EPS)) * gain


_sigmoid = jax.nn.sigmoid


def _silu(x):
    return x * _sigmoid(x)


def _layer_kernel(*refs, n_t, n_b, carry, pos0, zero_state):
    if carry:
        prefix_ref, x_ref, x_next_ref, *refs = refs
    else:
        x_ref, *refs = refs
    if not zero_state:
        s0re_ref, s0im_ref, buf0_ref, *refs = refs
    (gain_ref, w_in_ref, b_gate_ref, a2_re_ref, a2_im_ref, wb_ref, wc_ref, wd_ref, d_ref,
     w_glu_ref, b_glu_ref, pmix_ref, pscale_ref, w_bs_ref, w_bp_ref, w_out_ref, fgain_ref,
     y_ref, ore_ref, oim_ref, obuf_ref,
     sre_ref, sim_ref, ext_ref, bu_ref, u2_ref, xn_ref, us_ref, zs_ref, gs_ref, zp_ref,
     gp_ref, ys_ref, yp_ref) = refs
    m = n_t * n_b
    n_pairs = n_t // 2
    half = n_pairs * n_b
    step = pl.program_id(0)
    hist_rows = POOL_BUF * n_b

    def load_state():
        if zero_state:
            sre_ref[...] = jnp.zeros(sre_ref.shape, F32)
            sim_ref[...] = jnp.zeros(sim_ref.shape, F32)
            ext_ref[0:hist_rows, :] = jnp.zeros((hist_rows, D_MODEL), F32)
        else:
            sre_ref[...] = s0re_ref[...]
            sim_ref[...] = s0im_ref[...]
            ext_ref[0:hist_rows, :] = buf0_ref[...].reshape(hist_rows, D_MODEL)

    def rows_tb(src_ref):
        return jnp.swapaxes(src_ref[...], 0, 1).reshape(m, D_MODEL)

    def proj(k):
        return jnp.dot(xn_ref[...], w_in_ref[:, k * D_MODEL:(k + 1) * D_MODEL],
                       preferred_element_type=F32)

    def prepare_norm_rows(h):
        xn_ref[...] = _rmsnorm(h, gain_ref[...]).astype(BF16)

    def prepare_norm(src_ref):
        prepare_norm_rows(rows_tb(src_ref))

    def prefix_rows():
        tok = prefix_ref[...]
        n_tok = tok.shape[0]
        rows = jnp.broadcast_to(tok[:, None, :], (n_tok, n_b, D_MODEL)).reshape(n_tok * n_b, D_MODEL)
        return jnp.concatenate([jnp.zeros((m - n_tok * n_b, D_MODEL), F32), rows], axis=0)

    def prepare_u():
        us_ref[...] = proj(0).reshape(n_pairs, 2 * n_b, D_MODEL)

    def prepare_pairs():
        for blk in range(N_BLOCKS):
            lanes = slice(blk * LANES, (blk + 1) * LANES)
            st = slice(blk * BLOCK_STATE, (blk + 1) * BLOCK_STATE)
            u_t0 = us_ref[:, 0:n_b, lanes].reshape(half, LANES)
            u_t1 = us_ref[:, n_b:, lanes].reshape(half, LANES)
            u2_ref[blk] = jnp.concatenate([u_t0, u_t1], axis=1).astype(BF16)
            bu_ref[blk, 0:n_b, 0:BLOCK_STATE] = sre_ref[:, st]
            bu_ref[blk, 0:n_b, BLOCK_STATE:] = sim_ref[:, st]

    def prepare_slabs():
        for blk in range(N_BLOCKS):
            bu_ref[blk, n_b:, :] = jnp.dot(u2_ref[blk], wb_ref[blk], preferred_element_type=F32)

    def prepare(src_ref):
        prepare_norm(src_ref)
        prepare_u()
        prepare_pairs()
        prepare_slabs()

    if carry:
        @pl.when(step == 0)
        def _():
            load_state()
            prepare_norm_rows(prefix_rows())
            prepare_u()
            prepare_pairs()
            prepare_slabs()
    else:
        load_state()
        prepare(x_ref)

    piece = 2 * LANES

    def proj_piece(k, p):
        cols = slice(p * piece, (p + 1) * piece)
        v = jnp.dot(xn_ref[...], w_in_ref[:, k * D_MODEL + p * piece:k * D_MODEL + (p + 1) * piece],
                    preferred_element_type=F32)
        if k == 1:
            zs_ref[:, cols] = _silu(v)
        elif k == 4:
            gs_ref[:, cols] = _sigmoid(v + b_gate_ref[:, cols])
        elif k == 2:
            ext_ref[hist_rows:hist_rows + m, cols] = v
        elif k == 3:
            zp_ref[:, cols] = _silu(v)
        else:
            gp_ref[:, cols] = _sigmoid(
                v + b_gate_ref[:, D_MODEL + p * piece:D_MODEL + (p + 1) * piece])

    pieces = [(k, p) for k in (1, 4, 2, 3, 5) for p in range(D_MODEL // piece)]

    for blk in range(N_BLOCKS):
        lanes = slice(blk * LANES, (blk + 1) * LANES)
        st = slice(blk * BLOCK_STATE, (blk + 1) * BLOCK_STATE)
        for k, p in pieces[blk * len(pieces) // N_BLOCKS:(blk + 1) * len(pieces) // N_BLOCKS]:
            proj_piece(k, p)
        a_re = jnp.broadcast_to(a2_re_ref[:, st], (SUBLANES, BLOCK_STATE))
        a_im = jnp.broadcast_to(a2_im_ref[:, st], (SUBLANES, BLOCK_STATE))
        for j in range(n_b // SUBLANES):
            tile = slice(j * SUBLANES, (j + 1) * SUBLANES)
            s_re = sre_ref[tile, st]
            s_im = sim_ref[tile, st]
            for k in range(n_pairs):
                rows = slice((k + 1) * n_b + j * SUBLANES, (k + 1) * n_b + (j + 1) * SUBLANES)
                n_re = a_re * s_re - a_im * s_im + bu_ref[blk, rows, 0:BLOCK_STATE]
                n_im = a_re * s_im + a_im * s_re + bu_ref[blk, rows, BLOCK_STATE:]
                bu_ref[blk, rows, 0:BLOCK_STATE] = n_re
                bu_ref[blk, rows, BLOCK_STATE:] = n_im
                s_re, s_im = n_re, n_im
            sre_ref[tile, st] = s_re
            sim_ref[tile, st] = s_im
        y2 = (jnp.dot(bu_ref[blk, 0:half, :].astype(BF16), wc_ref[blk], preferred_element_type=F32)
              + jnp.dot(u2_ref[blk], wd_ref[blk], preferred_element_type=F32))
        ys_ref[:, 0:n_b, lanes] = y2[:, 0:LANES].reshape(n_pairs, n_b, LANES)
        ys_ref[:, n_b:, lanes] = y2[:, LANES:].reshape(n_pairs, n_b, LANES)

    if pos0 + 1 >= max(POOL_WINDOWS):
        pos = None
    else:
        t_loc = lax.broadcasted_iota(jnp.int32, (m, 1), 0) // n_b
        pos = pos0 + step * n_t + t_loc
    for gi, w in enumerate(POOL_WINDOWS):
        cols = slice(gi * POOL_GROUP_CH, (gi + 1) * POOL_GROUP_CH)
        s = ext_ref[(POOL_BUF + 1 - w) * n_b:hist_rows + m, cols]
        span = 1
        while span < w:
            keep = s.shape[0] - span * n_b
            s = s[:keep] + s[span * n_b:]
            span *= 2
        if pos is None:
            mean = s * (1.0 / w)
        else:
            cnt = jnp.maximum(jnp.minimum(pos + 1, w), 1)
            mean = s * (1.0 / cnt.astype(F32))
        pooled = mean - ext_ref[hist_rows:hist_rows + m, cols]
        yp_ref[:, cols] = jnp.dot(pooled.astype(BF16), pmix_ref[gi],
                                  preferred_element_type=F32) * pscale_ref[:, cols]
    for k in range(POOL_BUF):
        ext_ref[k * n_b:(k + 1) * n_b, :] = ext_ref[(k + n_t) * n_b:(k + n_t + 1) * n_b, :]

    y = ys_ref[...].reshape(m, D_MODEL) + d_ref[...] * us_ref[...].reshape(m, D_MODEL)
    y = jax.nn.gelu(y)
    b = yp_ref[...] * zp_ref[...]
    acc_p = jnp.dot(b.astype(BF16), w_bp_ref[...], preferred_element_type=F32)
    glu = jnp.dot(y.astype(BF16), w_glu_ref[...], preferred_element_type=F32)
    if carry:
        prepare_norm(x_next_ref)
    a = y * _sigmoid(glu + b_glu_ref[...]) * zs_ref[...]
    acc_s = jnp.dot(a.astype(BF16), w_bs_ref[...], preferred_element_type=F32)
    if carry:
        prepare_u()
        prepare_pairs()
    merged = gs_ref[...] * acc_s + gp_ref[...] * acc_p
    delta = jnp.dot(merged.astype(BF16), w_out_ref[...], preferred_element_type=F32)
    if carry:
        prepare_slabs()
    delta_bt = jnp.swapaxes(delta.reshape(n_t, n_b, D_MODEL), 0, 1)
    y_ref[...] = _rmsnorm(x_ref[...] + delta_bt, fgain_ref[...])

    def write_state():
        if carry:
            for g in range(SSM_GROUPS):
                ore_ref[:, g, :] = sre_ref[:, g * SSM_STATE:(g + 1) * SSM_STATE]
                oim_ref[:, g, :] = sim_ref[:, g * SSM_STATE:(g + 1) * SSM_STATE]
        else:
            ore_ref[...] = sre_ref[...]
            oim_ref[...] = sim_ref[...]
        obuf_ref[...] = ext_ref[0:hist_rows, :].reshape(POOL_BUF, n_b, D_MODEL)

    if carry:
        pl.when(step == pl.num_programs(0) - 1)(write_state)
    else:
        write_state()


def _layer_call(x, state, weights, *, n_t, n_b, carry, pos0, name, prefix=None):
    batch, time, _ = x.shape
    m = n_t * n_b
    x_block = (n_b, n_t, D_MODEL)
    buf_block = (POOL_BUF, n_b, D_MODEL)
    if carry:
        assert batch == n_b and time % n_t == 0 and prefix.shape[0] <= n_t
        n_blocks = time // n_t
        n_steps = n_blocks + 1
        x_specs = [pl.BlockSpec(prefix.shape, lambda i: (0, 0)),
                   pl.BlockSpec(x_block, lambda i: (0, jnp.maximum(i - 1, 0), 0)),
                   pl.BlockSpec(x_block, lambda i: (0, jnp.minimum(i, n_blocks - 1), 0))]
        y_map = lambda i: (0, jnp.maximum(i - 1, 0), 0)
        per_step = lambda i: (0, 0)
        per_step3 = lambda i: (0, 0, 0)
        xs = (prefix, x, x)
        state_out = (batch, SSM_GROUPS, SSM_STATE)
        state_out_spec = pl.BlockSpec(state_out, per_step3)
    else:
        assert time == n_t and batch % n_b == 0
        n_steps = batch // n_b
        x_specs = [pl.BlockSpec(x_block, lambda i: (i, 0, 0))]
        y_map = lambda i: (i, 0, 0)
        per_step = lambda i: (i, 0)
        per_step3 = lambda i: (0, i, 0)
        xs = (x,)
        state_out = (batch, STATE_W)
        state_out_spec = pl.BlockSpec((n_b, STATE_W), per_step)
    whole = pl.BlockSpec(memory_space=pltpu.VMEM)
    state_specs = [
        pl.BlockSpec((n_b, STATE_W), per_step),
        pl.BlockSpec((n_b, STATE_W), per_step),
        pl.BlockSpec(buf_block, per_step3),
    ]
    kernel = functools.partial(_layer_kernel, n_t=n_t, n_b=n_b, carry=carry, pos0=pos0,
                               zero_state=state is None)
    return pl.pallas_call(
        kernel,
        grid=(n_steps,),
        in_specs=x_specs + ([] if state is None else state_specs) + [whole] * len(weights),
        out_specs=[pl.BlockSpec(x_block, y_map), state_out_spec, state_out_spec, state_specs[2]],
        out_shape=[
            jax.ShapeDtypeStruct((batch, time, D_MODEL), F32),
            jax.ShapeDtypeStruct(state_out, F32),
            jax.ShapeDtypeStruct(state_out, F32),
            jax.ShapeDtypeStruct((POOL_BUF, batch, D_MODEL), F32),
        ],
        scratch_shapes=[
            pltpu.VMEM((n_b, STATE_W), F32),
            pltpu.VMEM((n_b, STATE_W), F32),
            pltpu.VMEM(((POOL_BUF + n_t) * n_b, D_MODEL), F32),
            pltpu.VMEM((N_BLOCKS, m // 2 + n_b, 2 * BLOCK_STATE), F32),
            pltpu.VMEM((N_BLOCKS, m // 2, 2 * LANES), BF16),
            pltpu.VMEM((m, D_MODEL), BF16),
            pltpu.VMEM((n_t // 2, 2 * n_b, D_MODEL), F32),
            pltpu.VMEM((m, D_MODEL), F32),
            pltpu.VMEM((m, D_MODEL), F32),
            pltpu.VMEM((m, D_MODEL), F32),
            pltpu.VMEM((m, D_MODEL), F32),
            pltpu.VMEM((n_t // 2, 2 * n_b, D_MODEL), F32),
            pltpu.VMEM((m, D_MODEL), F32),
        ],
        compiler_params=pltpu.CompilerParams(
            dimension_semantics=("arbitrary",), vmem_limit_bytes=VMEM_LIMIT_BYTES),
        name=name,
    )(*xs, *(() if state is None else state), *weights)


PREP_BLOCKS_PER_STEP = 2


def _s5_weights_kernel(a_re_ref, a_im_ref, log_dt_ref, bt_re_ref, bt_im_ref, c_re_ref, c_im_ref,
                       sre_t_ref, sim_t_ref,
                       wb_ref, wc_ref, wd_ref, a2_re_ref, a2_im_ref, sre_ref, sim_ref):
    gpb, n_c, n_p = GROUPS_PER_BLOCK, SSM_GROUP_CH, SSM_STATE
    lane = lax.broadcasted_iota(jnp.int32, (n_p, BLOCK_STATE), 1)
    sel = jnp.where(jnp.bitwise_and(lane, n_p - 1)
                    == lax.broadcasted_iota(jnp.int32, (n_p, BLOCK_STATE), 0), 1.0, 0.0).astype(BF16)
    same_group = (
        jnp.right_shift(lax.broadcasted_iota(jnp.int32, (LANES, BLOCK_STATE), 0),
                        n_c.bit_length() - 1)
        == jnp.right_shift(lax.broadcasted_iota(jnp.int32, (LANES, BLOCK_STATE), 1),
                           n_p.bit_length() - 1))
    same_group_cc = (
        jnp.right_shift(lax.broadcasted_iota(jnp.int32, (LANES, LANES), 0), n_c.bit_length() - 1)
        == jnp.right_shift(lax.broadcasted_iota(jnp.int32, (LANES, LANES), 1),
                           n_c.bit_length() - 1))

    def per_row(v):
        return jnp.broadcast_to(v[:, None, :], (gpb, n_c, n_p)).reshape(LANES, n_p)

    def expand(mat):
        tiled = jnp.dot(mat.astype(BF16), sel, preferred_element_type=F32)
        return jnp.where(same_group, tiled, 0.0)

    def gram(x, y):
        p = lax.dot_general(x, y, (((1,), (1,)), ((), ())), precision=lax.Precision.HIGHEST,
                            preferred_element_type=F32)
        return jnp.where(same_group_cc, p, 0.0)

    for blk in range(PREP_BLOCKS_PER_STEP):
        a_re, a_im = a_re_ref[blk], a_im_ref[blk]
        dt = jnp.exp(log_dt_ref[blk])
        mag = jnp.exp(dt * a_re)
        ang = dt * a_im
        abar_re = mag * jnp.cos(ang)
        abar_im = mag * jnp.sin(ang)
        den = a_re * a_re + a_im * a_im
        nr = abar_re - 1.0
        ni = abar_im
        q_re = (nr * a_re + ni * a_im) / den
        q_im = (ni * a_re - nr * a_im) / den
        a2_re = abar_re * abar_re - abar_im * abar_im
        a2_im = 2.0 * abar_re * abar_im
        a2_re_ref[blk] = a2_re
        a2_im_ref[blk] = a2_im

        ar, ai = per_row(abar_re), per_row(abar_im)
        ar2, ai2 = per_row(a2_re), per_row(a2_im)
        qr, qi = per_row(q_re), per_row(q_im)
        bt_re, bt_im = bt_re_ref[blk], bt_im_ref[blk]
        c_re, c_im = c_re_ref[blk], c_im_ref[blk]
        bbar_re = qr * bt_re - qi * bt_im
        bbar_im = qr * bt_im + qi * bt_re
        ab_re = ar * bbar_re - ai * bbar_im
        ab_im = ar * bbar_im + ai * bbar_re
        ca_re = c_re * ar - c_im * ai
        ca_im = c_re * ai + c_im * ar
        ca2_re = c_re * ar2 - c_im * ai2
        ca2_im = c_re * ai2 + c_im * ar2
        cb = gram(bbar_re, c_re) - gram(bbar_im, c_im)
        cab = gram(bbar_re, ca_re) - gram(bbar_im, ca_im)

        wb_ref[blk, 0:LANES, 0:BLOCK_STATE] = expand(ab_re).astype(BF16)
        wb_ref[blk, 0:LANES, BLOCK_STATE:] = expand(ab_im).astype(BF16)
        wb_ref[blk, LANES:, 0:BLOCK_STATE] = expand(bbar_re).astype(BF16)
        wb_ref[blk, LANES:, BLOCK_STATE:] = expand(bbar_im).astype(BF16)
        wc_ref[blk, 0:BLOCK_STATE, 0:LANES] = expand(ca_re).T.astype(BF16)
        wc_ref[blk, 0:BLOCK_STATE, LANES:] = expand(ca2_re).T.astype(BF16)
        wc_ref[blk, BLOCK_STATE:, 0:LANES] = expand(-ca_im).T.astype(BF16)
        wc_ref[blk, BLOCK_STATE:, LANES:] = expand(-ca2_im).T.astype(BF16)
        wd_ref[blk, 0:LANES, 0:LANES] = cb.astype(BF16)
        wd_ref[blk, 0:LANES, LANES:] = cab.astype(BF16)
        wd_ref[blk, LANES:, 0:LANES] = jnp.zeros((LANES, LANES), BF16)
        wd_ref[blk, LANES:, LANES:] = cb.astype(BF16)

    sre_ref[...] = sre_t_ref[...].T
    sim_ref[...] = sim_t_ref[...].T


def _ssm_weights(a_re, a_im, log_dt, b_re, b_im, c_re, c_im, states_t):
    gpb, per_step = GROUPS_PER_BLOCK, PREP_BLOCKS_PER_STEP
    n_steps = N_BLOCKS // per_step
    batch = states_t[0].shape[1]
    slab = STATE_W // n_steps
    per_state = [a_re, a_im]
    per_channel = [jnp.swapaxes(b_re, 1, 2), jnp.swapaxes(b_im, 1, 2), c_re, c_im]
    ins = ([v.reshape(N_BLOCKS, gpb, SSM_STATE) for v in per_state]
           + [log_dt.reshape(N_BLOCKS, gpb, 1)]
           + [v.reshape(N_BLOCKS, LANES, SSM_STATE) for v in per_channel] + list(states_t))
    state_spec = pl.BlockSpec((per_step, gpb, SSM_STATE), lambda i: (i, 0, 0))
    in_specs = ([state_spec] * len(per_state)
                + [pl.BlockSpec((per_step, gpb, 1), lambda i: (i, 0, 0))]
                + [pl.BlockSpec((per_step, LANES, SSM_STATE), lambda i: (i, 0, 0))]
                * len(per_channel)
                + [pl.BlockSpec((slab, batch), lambda i: (i, 0))] * len(states_t))
    shapes = [(2 * LANES, 2 * BLOCK_STATE), (2 * BLOCK_STATE, 2 * LANES), (2 * LANES, 2 * LANES)]
    wb, wc, wd, a2_re, a2_im, *states = pl.pallas_call(
        _s5_weights_kernel,
        grid=(n_steps,), in_specs=in_specs,
        out_specs=([pl.BlockSpec((per_step,) + s, lambda i: (i, 0, 0)) for s in shapes]
                   + [state_spec] * 2
                   + [pl.BlockSpec((batch, slab), lambda i: (0, i))] * len(states_t)),
        out_shape=([jax.ShapeDtypeStruct((N_BLOCKS,) + s, BF16) for s in shapes]
                   + [jax.ShapeDtypeStruct((N_BLOCKS, gpb, SSM_STATE), F32)] * 2
                   + [jax.ShapeDtypeStruct((batch, STATE_W), F32)] * len(states_t)),
        name="s5_block_diag_weights",
    )(*ins)
    return a2_re, a2_im, wb, wc, wd, states


PROMPT_T = 32
SAMPLE_B = 32


def kernel(x_prompt, x_sample, state_ssm_re, state_ssm_im, state_pool, meta_tokens, norm_gain, w_in, b_gate, ssm_a_re, ssm_a_im, ssm_log_dt, ssm_b_re, ssm_b_im, ssm_c_re, ssm_c_im, ssm_d, w_glu, b_glu, pool_mix, pool_scale, w_branch_ssm, w_branch_pool, w_out, final_norm_gain):
    batch, seq, _ = x_prompt.shape
    dec_batch, dec_seq, _ = x_sample.shape
    depth = norm_gain.shape[0]
    assert depth == 1
    l = 0
    states_t = [jnp.transpose(s[l], (1, 2, 0)).reshape(STATE_W, dec_batch)
                for s in (state_ssm_re, state_ssm_im)]
    a2_re, a2_im, wb, wc, wd, (s0_re, s0_im) = _ssm_weights(
        ssm_a_re[l], ssm_a_im[l], ssm_log_dt[l], ssm_b_re[l], ssm_b_im[l],
        ssm_c_re[l], ssm_c_im[l], states_t)
    row = lambda v: v.reshape(1, -1).astype(F32)
    weights = (
        row(norm_gain[l]), w_in[l].astype(BF16), row(b_gate[l]),
        row(a2_re), row(a2_im), wb, wc, wd, row(ssm_d[l]),
        w_glu[l].astype(BF16), row(b_glu[l]), pool_mix[l].astype(BF16), row(pool_scale[l]),
        w_branch_ssm[l].astype(BF16), w_branch_pool[l].astype(BF16), w_out[l].astype(BF16),
        row(final_norm_gain),
    )

    assert batch == SUBLANES and N_META <= PROMPT_T
    y_prompt, p_re, p_im, p_buf = _layer_call(
        x_prompt, None, weights, prefix=meta_tokens.astype(x_prompt.dtype),
        n_t=PROMPT_T, n_b=batch, carry=True, pos0=N_META - PROMPT_T, name="prompt_layer")

    sample_state = (s0_re, s0_im, jnp.swapaxes(state_pool[l], 0, 1))
    y_sample, s_re, s_im, s_buf = _layer_call(
        x_sample, sample_state, weights,
        n_t=dec_seq, n_b=SAMPLE_B, carry=False, pos0=PAST_LEN, name="sample_layer")

    state_shape = (1, -1, SSM_GROUPS, SSM_STATE)
    return (y_prompt, y_sample,
            p_re[None], p_im[None], jnp.swapaxes(p_buf, 0, 1)[None],
            s_re.reshape(state_shape), s_im.reshape(state_shape), jnp.swapaxes(s_buf, 0, 1)[None])
```

```python
import functools

import jax
import jax.numpy as jnp
from jax import lax
from jax.experimental import pallas as pl
from jax.experimental.pallas import tpu as pltpu

D_MODEL = 1024
N_META = 16
SSM_GROUP_CH = 16
SSM_GROUPS = D_MODEL // SSM_GROUP_CH
SSM_STATE = 64
POOL_WINDOWS = (2, 4, 8, 16)
POOL_GROUP_CH = D_MODEL // len(POOL_WINDOWS)
POOL_BUF = max(POOL_WINDOWS) - 1
EPS = 1e-6
PAST_LEN = 16384

SUBLANES = 8
LANES = 128
GROUPS_PER_BLOCK = LANES // SSM_GROUP_CH
N_BLOCKS = SSM_GROUPS // GROUPS_PER_BLOCK
BLOCK_STATE = GROUPS_PER_BLOCK * SSM_STATE
STATE_W = SSM_GROUPS * SSM_STATE
VMEM_LIMIT_BYTES = 56 * 1024 * 1024

F32 = jnp.float32
BF16 = jnp.bfloat16


def _rmsnorm(x, gain):
    ms = jnp.mean(x * x, axis=-1, keepdims=True)
    return (x * lax.rsqrt(ms + EPS)) * gain


_sigmoid = jax.nn.sigmoid


def _silu(x):
    return x * _sigmoid(x)


def _layer_kernel(*refs, n_t, n_b, carry, pos0, zero_state):
    if carry:
        prefix_ref, x_ref, x_next_ref, *refs = refs
    else:
        x_ref, *refs = refs
    if not zero_state:
        s0re_ref, s0im_ref, buf0_ref, *refs = refs
    (gain_ref, w_in_ref, b_gate_ref, a2_re_ref, a2_im_ref, wb_ref, wc_ref, wd_ref, d_ref,
     w_glu_ref, b_glu_ref, pmix_ref, pscale_ref, w_bs_ref, w_bp_ref, w_out_ref, fgain_ref,
     y_ref, ore_ref, oim_ref, obuf_ref,
     sre_ref, sim_ref, ext_ref, bu_ref, u2_ref, xn_ref, us_ref, zs_ref, gs_ref, zp_ref,
     gp_ref, ys_ref, yp_ref) = refs
    m = n_t * n_b
    n_pairs = n_t // 2
    half = n_pairs * n_b
    step = pl.program_id(0)
    hist_rows = POOL_BUF * n_b

    def load_state():
        if zero_state:
            sre_ref[...] = jnp.zeros(sre_ref.shape, F32)
            sim_ref[...] = jnp.zeros(sim_ref.shape, F32)
            ext_ref[0:hist_rows, :] = jnp.zeros((hist_rows, D_MODEL), F32)
        else:
            sre_ref[...] = s0re_ref[...]
            sim_ref[...] = s0im_ref[...]
            ext_ref[0:hist_rows, :] = buf0_ref[...].reshape(hist_rows, D_MODEL)

    def rows_tb(src_ref):
        return jnp.swapaxes(src_ref[...], 0, 1).reshape(m, D_MODEL)

    def proj(k):
        return jnp.dot(xn_ref[...], w_in_ref[:, k * D_MODEL:(k + 1) * D_MODEL],
                       preferred_element_type=F32)

    def prepare_norm_rows(h):
        xn_ref[...] = _rmsnorm(h, gain_ref[...]).astype(BF16)

    def prepare_norm(src_ref):
        prepare_norm_rows(rows_tb(src_ref))

    def prefix_rows():
        tok = prefix_ref[...]
        n_tok = tok.shape[0]
        rows = jnp.broadcast_to(tok[:, None, :], (n_tok, n_b, D_MODEL)).reshape(n_tok * n_b, D_MODEL)
        return jnp.concatenate([jnp.zeros((m - n_tok * n_b, D_MODEL), F32), rows], axis=0)

    def prepare_u():
        us_ref[...] = proj(0).reshape(n_pairs, 2 * n_b, D_MODEL)

    def prepare_pairs():
        for blk in range(N_BLOCKS):
            lanes = slice(blk * LANES, (blk + 1) * LANES)
            st = slice(blk * BLOCK_STATE, (blk + 1) * BLOCK_STATE)
            u_t0 = us_ref[:, 0:n_b, lanes].reshape(half, LANES)
            u_t1 = us_ref[:, n_b:, lanes].reshape(half, LANES)
            u2_ref[blk] = jnp.concatenate([u_t0, u_t1], axis=1).astype(BF16)
            bu_ref[blk, 0:n_b, 0:BLOCK_STATE] = sre_ref[:, st]
            bu_ref[blk, 0:n_b, BLOCK_STATE:] = sim_ref[:, st]

    def prepare_slabs():
        for blk in range(N_BLOCKS):
            bu_ref[blk, n_b:, :] = jnp.dot(u2_ref[blk], wb_ref[blk], preferred_element_type=F32)

    def prepare(src_ref):
        prepare_norm(src_ref)
        prepare_u()
        prepare_pairs()
        prepare_slabs()

    if carry:
        @pl.when(step == 0)
        def _():
            load_state()
            prepare_norm_rows(prefix_rows())
            prepare_u()
            prepare_pairs()
            prepare_slabs()
    else:
        load_state()
        prepare(x_ref)

    piece = 2 * LANES

    def proj_piece(k, p):
        cols = slice(p * piece, (p + 1) * piece)
        v = jnp.dot(xn_ref[...], w_in_ref[:, k * D_MODEL + p * piece:k * D_MODEL + (p + 1) * piece],
                    preferred_element_type=F32)
        if k == 1:
            zs_ref[:, cols] = _silu(v)
        elif k == 4:
            gs_ref[:, cols] = _sigmoid(v + b_gate_ref[:, cols])
        elif k == 2:
            ext_ref[hist_rows:hist_rows + m, cols] = v
        elif k == 3:
            zp_ref[:, cols] = _silu(v)
        else:
            gp_ref[:, cols] = _sigmoid(
                v + b_gate_ref[:, D_MODEL + p * piece:D_MODEL + (p + 1) * piece])

    pieces = [(k, p) for k in (1, 4, 2, 3, 5) for p in range(D_MODEL // piece)]

    for blk in range(N_BLOCKS):
        lanes = slice(blk * LANES, (blk + 1) * LANES)
        st = slice(blk * BLOCK_STATE, (blk + 1) * BLOCK_STATE)
        for k, p in pieces[blk * len(pieces) // N_BLOCKS:(blk + 1) * len(pieces) // N_BLOCKS]:
            proj_piece(k, p)
        a_re = jnp.broadcast_to(a2_re_ref[blk], (SUBLANES, BLOCK_STATE))
        a_im = jnp.broadcast_to(a2_im_ref[blk], (SUBLANES, BLOCK_STATE))
        for j in range(n_b // SUBLANES):
            tile = slice(j * SUBLANES, (j + 1) * SUBLANES)
            s_re = sre_ref[tile, st]
            s_im = sim_ref[tile, st]
            for k in range(n_pairs):
                rows = slice((k + 1) * n_b + j * SUBLANES, (k + 1) * n_b + (j + 1) * SUBLANES)
                n_re = a_re * s_re - a_im * s_im + bu_ref[blk, rows, 0:BLOCK_STATE]
                n_im = a_re * s_im + a_im * s_re + bu_ref[blk, rows, BLOCK_STATE:]
                bu_ref[blk, rows, 0:BLOCK_STATE] = n_re
                bu_ref[blk, rows, BLOCK_STATE:] = n_im
                s_re, s_im = n_re, n_im
            sre_ref[tile, st] = s_re
            sim_ref[tile, st] = s_im
        y2 = (jnp.dot(bu_ref[blk, 0:half, :].astype(BF16), wc_ref[blk], preferred_element_type=F32)
              + jnp.dot(u2_ref[blk], wd_ref[blk], preferred_element_type=F32))
        ys_ref[:, 0:n_b, lanes] = y2[:, 0:LANES].reshape(n_pairs, n_b, LANES)
        ys_ref[:, n_b:, lanes] = y2[:, LANES:].reshape(n_pairs, n_b, LANES)

    if pos0 + 1 >= max(POOL_WINDOWS):
        pos = None
    else:
        t_loc = lax.broadcasted_iota(jnp.int32, (m, 1), 0) // n_b
        pos = pos0 + step * n_t + t_loc
    for gi, w in enumerate(POOL_WINDOWS):
        cols = slice(gi * POOL_GROUP_CH, (gi + 1) * POOL_GROUP_CH)
        s = ext_ref[(POOL_BUF + 1 - w) * n_b:hist_rows + m, cols]
        span = 1
        while span < w:
            keep = s.shape[0] - span * n_b
            s = s[:keep] + s[span * n_b:]
            span *= 2
        if pos is None:
            mean = s * (1.0 / w)
        else:
            cnt = jnp.maximum(jnp.minimum(pos + 1, w), 1)
            mean = s * (1.0 / cnt.astype(F32))
        pooled = mean - ext_ref[hist_rows:hist_rows + m, cols]
        yp_ref[:, cols] = jnp.dot(pooled.astype(BF16), pmix_ref[gi],
                                  preferred_element_type=F32) * pscale_ref[:, cols]
    for k in range(POOL_BUF):
        ext_ref[k * n_b:(k + 1) * n_b, :] = ext_ref[(k + n_t) * n_b:(k + n_t + 1) * n_b, :]

    y = ys_ref[...].reshape(m, D_MODEL) + d_ref[...] * us_ref[...].reshape(m, D_MODEL)
    y = jax.nn.gelu(y)
    b = yp_ref[...] * zp_ref[...]
    acc_p = jnp.dot(b.astype(BF16), w_bp_ref[...], preferred_element_type=F32)
    glu = jnp.dot(y.astype(BF16), w_glu_ref[...], preferred_element_type=F32)
    if carry:
        prepare_norm(x_next_ref)
    a = y * _sigmoid(glu + b_glu_ref[...]) * zs_ref[...]
    acc_s = jnp.dot(a.astype(BF16), w_bs_ref[...], preferred_element_type=F32)
    if carry:
        prepare_u()
        prepare_pairs()
    merged = gs_ref[...] * acc_s + gp_ref[...] * acc_p
    delta = jnp.dot(merged.astype(BF16), w_out_ref[...], preferred_element_type=F32)
    if carry:
        prepare_slabs()
    delta_bt = jnp.swapaxes(delta.reshape(n_t, n_b, D_MODEL), 0, 1)
    y_ref[...] = _rmsnorm(x_ref[...] + delta_bt, fgain_ref[...])

    def write_state():
        if carry:
            for g in range(SSM_GROUPS):
                ore_ref[:, g, :] = sre_ref[:, g * SSM_STATE:(g + 1) * SSM_STATE]
                oim_ref[:, g, :] = sim_ref[:, g * SSM_STATE:(g + 1) * SSM_STATE]
        else:
            ore_ref[...] = sre_ref[...]
            oim_ref[...] = sim_ref[...]
        obuf_ref[...] = ext_ref[0:hist_rows, :].reshape(POOL_BUF, n_b, D_MODEL)

    if carry:
        pl.when(step == pl.num_programs(0) - 1)(write_state)
    else:
        write_state()


def _layer_call(x, state, weights, *, n_t, n_b, carry, pos0, name, prefix=None):
    batch, time, _ = x.shape
    m = n_t * n_b
    x_block = (n_b, n_t, D_MODEL)
    buf_block = (POOL_BUF, n_b, D_MODEL)
    if carry:
        assert batch == n_b and time % n_t == 0 and prefix.shape[0] <= n_t
        n_blocks = time // n_t
        n_steps = n_blocks + 1
        x_specs = [pl.BlockSpec(prefix.shape, lambda i: (0, 0)),
                   pl.BlockSpec(x_block, lambda i: (0, jnp.maximum(i - 1, 0), 0)),
                   pl.BlockSpec(x_block, lambda i: (0, jnp.minimum(i, n_blocks - 1), 0))]
        y_map = lambda i: (0, jnp.maximum(i - 1, 0), 0)
        per_step = lambda i: (0, 0)
        per_step3 = lambda i: (0, 0, 0)
        xs = (prefix, x, x)
        state_out = (batch, SSM_GROUPS, SSM_STATE)
        state_out_spec = pl.BlockSpec(state_out, per_step3)
    else:
        assert time == n_t and batch % n_b == 0
        n_steps = batch // n_b
        x_specs = [pl.BlockSpec(x_block, lambda i: (i, 0, 0))]
        y_map = lambda i: (i, 0, 0)
        per_step = lambda i: (i, 0)
        per_step3 = lambda i: (0, i, 0)
        xs = (x,)
        state_out = (batch, STATE_W)
        state_out_spec = pl.BlockSpec((n_b, STATE_W), per_step)
    whole = pl.BlockSpec(memory_space=pltpu.VMEM)
    state_specs = [
        pl.BlockSpec((n_b, STATE_W), per_step),
        pl.BlockSpec((n_b, STATE_W), per_step),
        pl.BlockSpec(buf_block, per_step3),
    ]
    kernel = functools.partial(_layer_kernel, n_t=n_t, n_b=n_b, carry=carry, pos0=pos0,
                               zero_state=state is None)
    return pl.pallas_call(
        kernel,
        grid=(n_steps,),
        in_specs=x_specs + ([] if state is None else state_specs) + [whole] * len(weights),
        out_specs=[pl.BlockSpec(x_block, y_map), state_out_spec, state_out_spec, state_specs[2]],
        out_shape=[
            jax.ShapeDtypeStruct((batch, time, D_MODEL), F32),
            jax.ShapeDtypeStruct(state_out, F32),
            jax.ShapeDtypeStruct(state_out, F32),
            jax.ShapeDtypeStruct((POOL_BUF, batch, D_MODEL), F32),
        ],
        scratch_shapes=[
            pltpu.VMEM((n_b, STATE_W), F32),
            pltpu.VMEM((n_b, STATE_W), F32),
            pltpu.VMEM(((POOL_BUF + n_t) * n_b, D_MODEL), F32),
            pltpu.VMEM((N_BLOCKS, m // 2 + n_b, 2 * BLOCK_STATE), F32),
            pltpu.VMEM((N_BLOCKS, m // 2, 2 * LANES), BF16),
            pltpu.VMEM((m, D_MODEL), BF16),
            pltpu.VMEM((n_t // 2, 2 * n_b, D_MODEL), F32),
            pltpu.VMEM((m, D_MODEL), F32),
            pltpu.VMEM((m, D_MODEL), F32),
            pltpu.VMEM((m, D_MODEL), F32),
            pltpu.VMEM((m, D_MODEL), F32),
            pltpu.VMEM((n_t // 2, 2 * n_b, D_MODEL), F32),
            pltpu.VMEM((m, D_MODEL), F32),
        ],
        compiler_params=pltpu.CompilerParams(
            dimension_semantics=("arbitrary",), vmem_limit_bytes=VMEM_LIMIT_BYTES),
        name=name,
    )(*xs, *(() if state is None else state), *weights)


PREP_BLOCKS_PER_STEP = 2


def _s5_weights_kernel(a_re_ref, a_im_ref, log_dt_ref, bt_re_ref, bt_im_ref, c_re_ref, c_im_ref,
                       sre_t_ref, sim_t_ref,
                       wb_ref, wc_ref, wd_ref, a2_re_ref, a2_im_ref, sre_ref, sim_ref):
    gpb, n_c, n_p = GROUPS_PER_BLOCK, SSM_GROUP_CH, SSM_STATE
    lane = lax.broadcasted_iota(jnp.int32, (n_p, BLOCK_STATE), 1)
    sel = jnp.where(jnp.bitwise_and(lane, n_p - 1)
                    == lax.broadcasted_iota(jnp.int32, (n_p, BLOCK_STATE), 0), 1.0, 0.0).astype(BF16)
    same_group = (
        jnp.right_shift(lax.broadcasted_iota(jnp.int32, (LANES, BLOCK_STATE), 0),
                        n_c.bit_length() - 1)
        == jnp.right_shift(lax.broadcasted_iota(jnp.int32, (LANES, BLOCK_STATE), 1),
                           n_p.bit_length() - 1))
    same_group_cc = (
        jnp.right_shift(lax.broadcasted_iota(jnp.int32, (LANES, LANES), 0), n_c.bit_length() - 1)
        == jnp.right_shift(lax.broadcasted_iota(jnp.int32, (LANES, LANES), 1),
                           n_c.bit_length() - 1))

    def per_row(v):
        return jnp.broadcast_to(v[:, None, :], (gpb, n_c, n_p)).reshape(LANES, n_p)

    def expand(mat):
        tiled = jnp.dot(mat.astype(BF16), sel, preferred_element_type=F32)
        return jnp.where(same_group, tiled, 0.0)

    def state_row(v):
        tiled = jnp.dot(v, sel.astype(F32), precision=lax.Precision.HIGHEST,
                        preferred_element_type=F32)
        own = (jnp.right_shift(lax.broadcasted_iota(jnp.int32, (gpb, BLOCK_STATE), 1),
                               n_p.bit_length() - 1)
               == lax.broadcasted_iota(jnp.int32, (gpb, BLOCK_STATE), 0))
        return jnp.sum(jnp.where(own, tiled, 0.0), axis=0, keepdims=True)

    def gram(x, y):
        p = lax.dot_general(x, y, (((1,), (1,)), ((), ())), precision=lax.Precision.HIGHEST,
                            preferred_element_type=F32)
        return jnp.where(same_group_cc, p, 0.0)

    for blk in range(PREP_BLOCKS_PER_STEP):
        a_re, a_im = a_re_ref[blk], a_im_ref[blk]
        dt = jnp.exp(log_dt_ref[blk])
        mag = jnp.exp(dt * a_re)
        ang = dt * a_im
        abar_re = mag * jnp.cos(ang)
        abar_im = mag * jnp.sin(ang)
        den = a_re * a_re + a_im * a_im
        nr = abar_re - 1.0
        ni = abar_im
        q_re = (nr * a_re + ni * a_im) / den
        q_im = (ni * a_re - nr * a_im) / den
        a2_re = abar_re * abar_re - abar_im * abar_im
        a2_im = 2.0 * abar_re * abar_im
        a2_re_ref[blk] = state_row(a2_re)
        a2_im_ref[blk] = state_row(a2_im)

        ar, ai = per_row(abar_re), per_row(abar_im)
        ar2, ai2 = per_row(a2_re), per_row(a2_im)
        qr, qi = per_row(q_re), per_row(q_im)
        bt_re, bt_im = bt_re_ref[blk], bt_im_ref[blk]
        c_re, c_im = c_re_ref[blk], c_im_ref[blk]
        bbar_re = qr * bt_re - qi * bt_im
        bbar_im = qr * bt_im + qi * bt_re
        ab_re = ar * bbar_re - ai * bbar_im
        ab_im = ar * bbar_im + ai * bbar_re
        ca_re = c_re * ar - c_im * ai
        ca_im = c_re * ai + c_im * ar
        ca2_re = c_re * ar2 - c_im * ai2
        ca2_im = c_re * ai2 + c_im * ar2
        cb = gram(bbar_re, c_re) - gram(bbar_im, c_im)
        cab = gram(bbar_re, ca_re) - gram(bbar_im, ca_im)

        wb_ref[blk, 0:LANES, 0:BLOCK_STATE] = expand(ab_re).astype(BF16)
        wb_ref[blk, 0:LANES, BLOCK_STATE:] = expand(ab_im).astype(BF16)
        wb_ref[blk, LANES:, 0:BLOCK_STATE] = expand(bbar_re).astype(BF16)
        wb_ref[blk, LANES:, BLOCK_STATE:] = expand(bbar_im).astype(BF16)
        wc_ref[blk, 0:BLOCK_STATE, 0:LANES] = expand(ca_re).T.astype(BF16)
        wc_ref[blk, 0:BLOCK_STATE, LANES:] = expand(ca2_re).T.astype(BF16)
        wc_ref[blk, BLOCK_STATE:, 0:LANES] = expand(-ca_im).T.astype(BF16)
        wc_ref[blk, BLOCK_STATE:, LANES:] = expand(-ca2_im).T.astype(BF16)
        wd_ref[blk, 0:LANES, 0:LANES] = cb.astype(BF16)
        wd_ref[blk, 0:LANES, LANES:] = cab.astype(BF16)
        wd_ref[blk, LANES:, 0:LANES] = jnp.zeros((LANES, LANES), BF16)
        wd_ref[blk, LANES:, LANES:] = cb.astype(BF16)

    sre_ref[...] = sre_t_ref[...].T
    sim_ref[...] = sim_t_ref[...].T


def _ssm_weights(a_re, a_im, log_dt, b_re, b_im, c_re, c_im, states_t):
    gpb, per_step = GROUPS_PER_BLOCK, PREP_BLOCKS_PER_STEP
    n_steps = N_BLOCKS // per_step
    batch = states_t[0].shape[1]
    slab = STATE_W // n_steps
    per_state = [a_re, a_im, jnp.broadcast_to(log_dt[:, None], a_re.shape)]
    per_channel = [jnp.swapaxes(b_re, 1, 2), jnp.swapaxes(b_im, 1, 2), c_re, c_im]
    ins = ([v.reshape(N_BLOCKS, gpb, SSM_STATE) for v in per_state]
           + [v.reshape(N_BLOCKS, LANES, SSM_STATE) for v in per_channel] + list(states_t))
    in_specs = ([pl.BlockSpec((per_step, gpb, SSM_STATE), lambda i: (i, 0, 0))] * len(per_state)
                + [pl.BlockSpec((per_step, LANES, SSM_STATE), lambda i: (i, 0, 0))]
                * len(per_channel)
                + [pl.BlockSpec((slab, batch), lambda i: (i, 0))] * len(states_t))
    shapes = [(2 * LANES, 2 * BLOCK_STATE), (2 * BLOCK_STATE, 2 * LANES), (2 * LANES, 2 * LANES),
              (1, BLOCK_STATE), (1, BLOCK_STATE)]
    dtypes = [BF16, BF16, BF16, F32, F32]
    wb, wc, wd, a2_re, a2_im, *states = pl.pallas_call(
        _s5_weights_kernel,
        grid=(n_steps,), in_specs=in_specs,
        out_specs=([pl.BlockSpec((per_step,) + s, lambda i: (i, 0, 0)) for s in shapes]
                   + [pl.BlockSpec((batch, slab), lambda i: (0, i))] * len(states_t)),
        out_shape=([jax.ShapeDtypeStruct((N_BLOCKS,) + s, d) for s, d in zip(shapes, dtypes)]
                   + [jax.ShapeDtypeStruct((batch, STATE_W), F32)] * len(states_t)),
        name="s5_block_diag_weights",
    )(*ins)
    return a2_re, a2_im, wb, wc, wd, states


PROMPT_T = 32
SAMPLE_B = 32


def kernel(x_prompt, x_sample, state_ssm_re, state_ssm_im, state_pool, meta_tokens, norm_gain, w_in, b_gate, ssm_a_re, ssm_a_im, ssm_log_dt, ssm_b_re, ssm_b_im, ssm_c_re, ssm_c_im, ssm_d, w_glu, b_glu, pool_mix, pool_scale, w_branch_ssm, w_branch_pool, w_out, final_norm_gain):
    batch, seq, _ = x_prompt.shape
    dec_batch, dec_seq, _ = x_sample.shape
    depth = norm_gain.shape[0]
    assert depth == 1
    l = 0
    states_t = [jnp.transpose(s[l], (1, 2, 0)).reshape(STATE_W, dec_batch)
                for s in (state_ssm_re, state_ssm_im)]
    a2_re, a2_im, wb, wc, wd, (s0_re, s0_im) = _ssm_weights(
        ssm_a_re[l], ssm_a_im[l], ssm_log_dt[l], ssm_b_re[l], ssm_b_im[l],
        ssm_c_re[l], ssm_c_im[l], states_t)
    row = lambda v: v.reshape(1, -1).astype(F32)
    weights = (
        row(norm_gain[l]), w_in[l].astype(BF16), row(b_gate[l]),
        a2_re, a2_im, wb, wc, wd, row(ssm_d[l]),
        w_glu[l].astype(BF16), row(b_glu[l]), pool_mix[l].astype(BF16), row(pool_scale[l]),
        w_branch_ssm[l].astype(BF16), w_branch_pool[l].astype(BF16), w_out[l].astype(BF16),
        row(final_norm_gain),
    )

    assert batch == SUBLANES and N_META <= PROMPT_T
    y_prompt, p_re, p_im, p_buf = _layer_call(
        x_prompt, None, weights, prefix=meta_tokens.astype(x_prompt.dtype),
        n_t=PROMPT_T, n_b=batch, carry=True, pos0=N_META - PROMPT_T, name="prompt_layer")

    sample_state = (s0_re, s0_im, jnp.swapaxes(state_pool[l], 0, 1))
    y_sample, s_re, s_im, s_buf = _layer_call(
        x_sample, sample_state, weights,
        n_t=dec_seq, n_b=SAMPLE_B, carry=False, pos0=PAST_LEN, name="sample_layer")

    state_shape = (1, -1, SSM_GROUPS, SSM_STATE)
    return (y_prompt, y_sample,
            p_re[None], p_im[None], jnp.swapaxes(p_buf, 0, 1)[None],
            s_re.reshape(state_shape), s_im.reshape(state_shape), jnp.swapaxes(s_buf, 0, 1)[None])
```

```python
import functools

import jax
import jax.numpy as jnp
from jax import lax
from jax.experimental import pallas as pl
from jax.experimental.pallas import tpu as pltpu

D_MODEL = 1024
N_META = 16
SSM_GROUP_CH = 16
SSM_GROUPS = D_MODEL // SSM_GROUP_CH
SSM_STATE = 64
POOL_WINDOWS = (2, 4, 8, 16)
POOL_GROUP_CH = D_MODEL // len(POOL_WINDOWS)
POOL_BUF = max(POOL_WINDOWS) - 1
EPS = 1e-6
PAST_LEN = 16384

SUBLANES = 8
LANES = 128
GROUPS_PER_BLOCK = LANES // SSM_GROUP_CH
N_BLOCKS = SSM_GROUPS // GROUPS_PER_BLOCK
BLOCK_STATE = GROUPS_PER_BLOCK * SSM_STATE
STATE_W = SSM_GROUPS * SSM_STATE
VMEM_LIMIT_BYTES = 56 * 1024 * 1024

F32 = jnp.float32
BF16 = jnp.bfloat16


def _rmsnorm(x, gain):
    ms = jnp.mean(x * x, axis=-1, keepdims=True)
    return (x * lax.rsqrt(ms + EPS)) * gain


_sigmoid = jax.nn.sigmoid


def _silu(x):
    return x * _sigmoid(x)


def _layer_kernel(*refs, n_t, n_b, carry, pos0, zero_state):
    if carry:
        prefix_ref, x_ref, x_next_ref, *refs = refs
    else:
        x_ref, *refs = refs
    if not zero_state:
        s0re_ref, s0im_ref, buf0_ref, *refs = refs
    (gain_ref, w_in_ref, b_gate_ref, a2_re_ref, a2_im_ref, wb_ref, wc_ref, wd_ref, d_ref,
     w_glu_ref, b_glu_ref, pmix_ref, pscale_ref, w_bs_ref, w_bp_ref, w_out_ref, fgain_ref,
     y_ref, ore_ref, oim_ref, obuf_ref,
     sre_ref, sim_ref, ext_ref, bu_ref, u2_ref, xn_ref, us_ref, zs_ref, gs_ref, zp_ref,
     gp_ref, ys_ref, yp_ref) = refs
    m = n_t * n_b
    n_pairs = n_t // 2
    half = n_pairs * n_b
    step = pl.program_id(0)
    hist_rows = POOL_BUF * n_b

    def load_state():
        if zero_state:
            sre_ref[...] = jnp.zeros(sre_ref.shape, F32)
            sim_ref[...] = jnp.zeros(sim_ref.shape, F32)
            ext_ref[0:hist_rows, :] = jnp.zeros((hist_rows, D_MODEL), F32)
        else:
            sre_ref[...] = s0re_ref[...]
            sim_ref[...] = s0im_ref[...]
            ext_ref[0:hist_rows, :] = buf0_ref[...].reshape(hist_rows, D_MODEL)

    def rows_tb(src_ref):
        return jnp.swapaxes(src_ref[...], 0, 1).reshape(m, D_MODEL)

    def proj(k):
        return jnp.dot(xn_ref[...], w_in_ref[:, k * D_MODEL:(k + 1) * D_MODEL],
                       preferred_element_type=F32)

    def prepare_norm_rows(h):
        xn_ref[...] = _rmsnorm(h, gain_ref[...]).astype(BF16)

    def prepare_norm(src_ref):
        prepare_norm_rows(rows_tb(src_ref))

    def prefix_rows():
        tok = prefix_ref[...]
        n_tok = tok.shape[0]
        rows = jnp.broadcast_to(tok[:, None, :], (n_tok, n_b, D_MODEL)).reshape(n_tok * n_b, D_MODEL)
        return jnp.concatenate([jnp.zeros((m - n_tok * n_b, D_MODEL), F32), rows], axis=0)

    def prepare_u():
        us_ref[...] = proj(0).reshape(n_pairs, 2 * n_b, D_MODEL)

    def prepare_pairs():
        for blk in range(N_BLOCKS):
            lanes = slice(blk * LANES, (blk + 1) * LANES)
            st = slice(blk * BLOCK_STATE, (blk + 1) * BLOCK_STATE)
            u_t0 = us_ref[:, 0:n_b, lanes].reshape(half, LANES)
            u_t1 = us_ref[:, n_b:, lanes].reshape(half, LANES)
            u2_ref[blk] = jnp.concatenate([u_t0, u_t1], axis=1).astype(BF16)
            bu_ref[blk, 0:n_b, 0:BLOCK_STATE] = sre_ref[:, st]
            bu_ref[blk, 0:n_b, BLOCK_STATE:] = sim_ref[:, st]

    def prepare_slabs():
        for blk in range(N_BLOCKS):
            bu_ref[blk, n_b:, :] = jnp.dot(u2_ref[blk], wb_ref[blk], preferred_element_type=F32)

    def prepare(src_ref):
        prepare_norm(src_ref)
        prepare_u()
        prepare_pairs()
        prepare_slabs()

    if carry:
        @pl.when(step == 0)
        def _():
            load_state()
            prepare_norm_rows(prefix_rows())
            prepare_u()
            prepare_pairs()
            prepare_slabs()
    else:
        load_state()
        prepare(x_ref)

    piece = 2 * LANES

    def proj_piece(k, p):
        cols = slice(p * piece, (p + 1) * piece)
        v = jnp.dot(xn_ref[...], w_in_ref[:, k * D_MODEL + p * piece:k * D_MODEL + (p + 1) * piece],
                    preferred_element_type=F32)
        if k == 1:
            zs_ref[:, cols] = _silu(v)
        elif k == 4:
            gs_ref[:, cols] = _sigmoid(v + b_gate_ref[:, cols])
        elif k == 2:
            ext_ref[hist_rows:hist_rows + m, cols] = v
        elif k == 3:
            zp_ref[:, cols] = _silu(v)
        else:
            gp_ref[:, cols] = _sigmoid(
                v + b_gate_ref[:, D_MODEL + p * piece:D_MODEL + (p + 1) * piece])

    pieces = [(k, p) for k in (1, 4, 2, 3, 5) for p in range(D_MODEL // piece)]

    for blk in range(N_BLOCKS):
        lanes = slice(blk * LANES, (blk + 1) * LANES)
        st = slice(blk * BLOCK_STATE, (blk + 1) * BLOCK_STATE)
        for k, p in pieces[blk * len(pieces) // N_BLOCKS:(blk + 1) * len(pieces) // N_BLOCKS]:
            proj_piece(k, p)
        a_re = jnp.broadcast_to(a2_re_ref[blk], (SUBLANES, BLOCK_STATE))
        a_im = jnp.broadcast_to(a2_im_ref[blk], (SUBLANES, BLOCK_STATE))
        for j in range(n_b // SUBLANES):
            tile = slice(j * SUBLANES, (j + 1) * SUBLANES)
            s_re = sre_ref[tile, st]
            s_im = sim_ref[tile, st]
            for k in range(n_pairs):
                rows = slice((k + 1) * n_b + j * SUBLANES, (k + 1) * n_b + (j + 1) * SUBLANES)
                n_re = a_re * s_re - a_im * s_im + bu_ref[blk, rows, 0:BLOCK_STATE]
                n_im = a_re * s_im + a_im * s_re + bu_ref[blk, rows, BLOCK_STATE:]
                bu_ref[blk, rows, 0:BLOCK_STATE] = n_re
                bu_ref[blk, rows, BLOCK_STATE:] = n_im
                s_re, s_im = n_re, n_im
            sre_ref[tile, st] = s_re
            sim_ref[tile, st] = s_im
        y2 = (jnp.dot(bu_ref[blk, 0:half, :].astype(BF16), wc_ref[blk], preferred_element_type=F32)
              + jnp.dot(u2_ref[blk], wd_ref[blk], preferred_element_type=F32))
        ys_ref[:, 0:n_b, lanes] = y2[:, 0:LANES].reshape(n_pairs, n_b, LANES)
        ys_ref[:, n_b:, lanes] = y2[:, LANES:].reshape(n_pairs, n_b, LANES)

    if pos0 + 1 >= max(POOL_WINDOWS):
        pos = None
    else:
        t_loc = lax.broadcasted_iota(jnp.int32, (m, 1), 0) // n_b
        pos = pos0 + step * n_t + t_loc
    for gi, w in enumerate(POOL_WINDOWS):
        cols = slice(gi * POOL_GROUP_CH, (gi + 1) * POOL_GROUP_CH)
        s = ext_ref[(POOL_BUF + 1 - w) * n_b:hist_rows + m, cols]
        span = 1
        while span < w:
            keep = s.shape[0] - span * n_b
            s = s[:keep] + s[span * n_b:]
            span *= 2
        if pos is None:
            mean = s * (1.0 / w)
        else:
            cnt = jnp.maximum(jnp.minimum(pos + 1, w), 1)
            mean = s * (1.0 / cnt.astype(F32))
        pooled = mean - ext_ref[hist_rows:hist_rows + m, cols]
        yp_ref[:, cols] = jnp.dot(pooled.astype(BF16), pmix_ref[gi],
                                  preferred_element_type=F32) * pscale_ref[:, cols]
    for k in range(POOL_BUF):
        ext_ref[k * n_b:(k + 1) * n_b, :] = ext_ref[(k + n_t) * n_b:(k + n_t + 1) * n_b, :]

    y = ys_ref[...].reshape(m, D_MODEL) + d_ref[...] * us_ref[...].reshape(m, D_MODEL)
    y = jax.nn.gelu(y)
    b = yp_ref[...] * zp_ref[...]
    acc_p = jnp.dot(b.astype(BF16), w_bp_ref[...], preferred_element_type=F32)
    glu = jnp.dot(y.astype(BF16), w_glu_ref[...], preferred_element_type=F32)
    if carry:
        prepare_norm(x_next_ref)
    a = y * _sigmoid(glu + b_glu_ref[...]) * zs_ref[...]
    acc_s = jnp.dot(a.astype(BF16), w_bs_ref[...], preferred_element_type=F32)
    if carry:
        prepare_u()
        prepare_pairs()
    merged = gs_ref[...] * acc_s + gp_ref[...] * acc_p
    delta = jnp.dot(merged.astype(BF16), w_out_ref[...], preferred_element_type=F32)
    if carry:
        prepare_slabs()
    delta_bt = jnp.swapaxes(delta.reshape(n_t, n_b, D_MODEL), 0, 1)
    y_ref[...] = _rmsnorm(x_ref[...] + delta_bt, fgain_ref[...])

    def write_state():
        if carry:
            for g in range(SSM_GROUPS):
                ore_ref[:, g, :] = sre_ref[:, g * SSM_STATE:(g + 1) * SSM_STATE]
                oim_ref[:, g, :] = sim_ref[:, g * SSM_STATE:(g + 1) * SSM_STATE]
        else:
            ore_ref[...] = sre_ref[...]
            oim_ref[...] = sim_ref[...]
        obuf_ref[...] = ext_ref[0:hist_rows, :].reshape(POOL_BUF, n_b, D_MODEL)

    if carry:
        pl.when(step == pl.num_programs(0) - 1)(write_state)
    else:
        write_state()


def _layer_call(x, state, weights, *, n_t, n_b, carry, pos0, name, prefix=None):
    batch, time, _ = x.shape
    m = n_t * n_b
    x_block = (n_b, n_t, D_MODEL)
    buf_block = (POOL_BUF, n_b, D_MODEL)
    if carry:
        assert batch == n_b and time % n_t == 0 and prefix.shape[0] <= n_t
        n_blocks = time // n_t
        n_steps = n_blocks + 1
        x_specs = [pl.BlockSpec(prefix.shape, lambda i: (0, 0)),
                   pl.BlockSpec(x_block, lambda i: (0, jnp.maximum(i - 1, 0), 0)),
                   pl.BlockSpec(x_block, lambda i: (0, jnp.minimum(i, n_blocks - 1), 0))]
        y_map = lambda i: (0, jnp.maximum(i - 1, 0), 0)
        per_step = lambda i: (0, 0)
        per_step3 = lambda i: (0, 0, 0)
        xs = (prefix, x, x)
        state_out = (batch, SSM_GROUPS, SSM_STATE)
        state_out_spec = pl.BlockSpec(state_out, per_step3)
    else:
        assert time == n_t and batch % n_b == 0
        n_steps = batch // n_b
        x_specs = [pl.BlockSpec(x_block, lambda i: (i, 0, 0))]
        y_map = lambda i: (i, 0, 0)
        per_step = lambda i: (i, 0)
        per_step3 = lambda i: (0, i, 0)
        xs = (x,)
        state_out = (batch, STATE_W)
        state_out_spec = pl.BlockSpec((n_b, STATE_W), per_step)
    whole = pl.BlockSpec(memory_space=pltpu.VMEM)
    state_specs = [
        pl.BlockSpec((n_b, STATE_W), per_step),
        pl.BlockSpec((n_b, STATE_W), per_step),
        pl.BlockSpec(buf_block, per_step3),
    ]
    kernel = functools.partial(_layer_kernel, n_t=n_t, n_b=n_b, carry=carry, pos0=pos0,
                               zero_state=state is None)
    return pl.pallas_call(
        kernel,
        grid=(n_steps,),
        in_specs=x_specs + ([] if state is None else state_specs) + [whole] * len(weights),
        out_specs=[pl.BlockSpec(x_block, y_map), state_out_spec, state_out_spec, state_specs[2]],
        out_shape=[
            jax.ShapeDtypeStruct((batch, time, D_MODEL), F32),
            jax.ShapeDtypeStruct(state_out, F32),
            jax.ShapeDtypeStruct(state_out, F32),
            jax.ShapeDtypeStruct((POOL_BUF, batch, D_MODEL), F32),
        ],
        scratch_shapes=[
            pltpu.VMEM((n_b, STATE_W), F32),
            pltpu.VMEM((n_b, STATE_W), F32),
            pltpu.VMEM(((POOL_BUF + n_t) * n_b, D_MODEL), F32),
            pltpu.VMEM((N_BLOCKS, m // 2 + n_b, 2 * BLOCK_STATE), F32),
            pltpu.VMEM((N_BLOCKS, m // 2, 2 * LANES), BF16),
            pltpu.VMEM((m, D_MODEL), BF16),
            pltpu.VMEM((n_t // 2, 2 * n_b, D_MODEL), F32),
            pltpu.VMEM((m, D_MODEL), F32),
            pltpu.VMEM((m, D_MODEL), F32),
            pltpu.VMEM((m, D_MODEL), F32),
            pltpu.VMEM((m, D_MODEL), F32),
            pltpu.VMEM((n_t // 2, 2 * n_b, D_MODEL), F32),
            pltpu.VMEM((m, D_MODEL), F32),
        ],
        compiler_params=pltpu.CompilerParams(
            dimension_semantics=("arbitrary",), vmem_limit_bytes=VMEM_LIMIT_BYTES),
        name=name,
    )(*xs, *(() if state is None else state), *weights)


PREP_BLOCKS_PER_STEP = 2


def _s5_weights_kernel(a_re_ref, a_im_ref, log_dt_ref, bt_re_ref, bt_im_ref, c_re_ref, c_im_ref,
                       sre_t_ref, sim_t_ref,
                       wb_ref, wc_ref, wd_ref, a2_re_ref, a2_im_ref, sre_ref, sim_ref):
    gpb, n_c, n_p = GROUPS_PER_BLOCK, SSM_GROUP_CH, SSM_STATE
    lane = lax.broadcasted_iota(jnp.int32, (n_p, BLOCK_STATE), 1)
    sel = jnp.where(jnp.bitwise_and(lane, n_p - 1)
                    == lax.broadcasted_iota(jnp.int32, (n_p, BLOCK_STATE), 0), 1.0, 0.0).astype(BF16)

    def group_of(shape, axis, width):
        idx = jnp.right_shift(lax.broadcasted_iota(jnp.int32, shape, axis), width.bit_length() - 1)
        return jnp.bitwise_and(idx, gpb - 1)

    n_stack = 4
    same_group = (group_of((n_stack * LANES, BLOCK_STATE), 0, n_c)
                  == group_of((n_stack * LANES, BLOCK_STATE), 1, n_p))
    same_group_cc = (group_of((LANES, 2 * LANES), 0, n_c) == group_of((LANES, 2 * LANES), 1, n_c))

    def per_row(v):
        return jnp.broadcast_to(v[:, None, :], (gpb, n_c, n_p)).reshape(LANES, n_p)

    def expand(mats):
        stacked = jnp.concatenate(mats, axis=0).astype(BF16)
        tiled = jnp.where(same_group, jnp.dot(stacked, sel, preferred_element_type=F32), 0.0)
        return [tiled[k * LANES:(k + 1) * LANES] for k in range(n_stack)]

    def state_row(v):
        tiled = jnp.dot(v, sel.astype(F32), precision=lax.Precision.HIGHEST,
                        preferred_element_type=F32)
        own = (jnp.right_shift(lax.broadcasted_iota(jnp.int32, (gpb, BLOCK_STATE), 1),
                               n_p.bit_length() - 1)
               == lax.broadcasted_iota(jnp.int32, (gpb, BLOCK_STATE), 0))
        return jnp.sum(jnp.where(own, tiled, 0.0), axis=0, keepdims=True)

    def gram(x, y):
        return lax.dot_general(x, y, (((1,), (1,)), ((), ())), precision=lax.Precision.HIGHEST,
                               preferred_element_type=F32)

    for blk in range(PREP_BLOCKS_PER_STEP):
        a_re, a_im = a_re_ref[blk], a_im_ref[blk]
        first_group = (pl.program_id(0) * PREP_BLOCKS_PER_STEP + blk) * gpb
        own_dt = (lax.broadcasted_iota(jnp.int32, (gpb, SSM_GROUPS), 1)
                  == first_group + lax.broadcasted_iota(jnp.int32, (gpb, SSM_GROUPS), 0))
        dt = jnp.exp(jnp.sum(jnp.where(own_dt, log_dt_ref[...], 0.0), axis=1, keepdims=True))
        mag = jnp.exp(dt * a_re)
        ang = dt * a_im
        abar_re = mag * jnp.cos(ang)
        abar_im = mag * jnp.sin(ang)
        den = a_re * a_re + a_im * a_im
        nr = abar_re - 1.0
        ni = abar_im
        q_re = (nr * a_re + ni * a_im) / den
        q_im = (ni * a_re - nr * a_im) / den
        a2_re = abar_re * abar_re - abar_im * abar_im
        a2_im = 2.0 * abar_re * abar_im
        a2_re_ref[blk] = state_row(a2_re)
        a2_im_ref[blk] = state_row(a2_im)

        ar, ai = per_row(abar_re), per_row(abar_im)
        ar2, ai2 = per_row(a2_re), per_row(a2_im)
        qr, qi = per_row(q_re), per_row(q_im)
        bt_re, bt_im = bt_re_ref[blk], bt_im_ref[blk]
        c_re, c_im = c_re_ref[blk], c_im_ref[blk]
        bbar_re = qr * bt_re - qi * bt_im
        bbar_im = qr * bt_im + qi * bt_re
        ab_re = ar * bbar_re - ai * bbar_im
        ab_im = ar * bbar_im + ai * bbar_re
        ca_re = c_re * ar - c_im * ai
        ca_im = c_re * ai + c_im * ar
        ca2_re = c_re * ar2 - c_im * ai2
        ca2_im = c_re * ai2 + c_im * ar2
        direct = (gram(bbar_re, jnp.concatenate([c_re, ca_re], axis=0))
                  - gram(bbar_im, jnp.concatenate([c_im, ca_im], axis=0)))
        direct = jnp.where(same_group_cc, direct, 0.0).astype(BF16)

        e_ab_re, e_ab_im, e_bb_re, e_bb_im = expand([ab_re, ab_im, bbar_re, bbar_im])
        wb_ref[blk, 0:LANES, 0:BLOCK_STATE] = e_ab_re.astype(BF16)
        wb_ref[blk, 0:LANES, BLOCK_STATE:] = e_ab_im.astype(BF16)
        wb_ref[blk, LANES:, 0:BLOCK_STATE] = e_bb_re.astype(BF16)
        wb_ref[blk, LANES:, BLOCK_STATE:] = e_bb_im.astype(BF16)
        e_ca_re, e_ca2_re, e_ca_im, e_ca2_im = expand([ca_re, ca2_re, -ca_im, -ca2_im])
        wc_ref[blk, 0:BLOCK_STATE, 0:LANES] = e_ca_re.T.astype(BF16)
        wc_ref[blk, 0:BLOCK_STATE, LANES:] = e_ca2_re.T.astype(BF16)
        wc_ref[blk, BLOCK_STATE:, 0:LANES] = e_ca_im.T.astype(BF16)
        wc_ref[blk, BLOCK_STATE:, LANES:] = e_ca2_im.T.astype(BF16)
        wd_ref[blk, 0:LANES, :] = direct
        wd_ref[blk, LANES:, 0:LANES] = jnp.zeros((LANES, LANES), BF16)
        wd_ref[blk, LANES:, LANES:] = direct[:, 0:LANES]

    sre_ref[...] = sre_t_ref[...].T
    sim_ref[...] = sim_t_ref[...].T


def _ssm_weights(a_re, a_im, log_dt, b_re, b_im, c_re, c_im, states_t):
    gpb, per_step = GROUPS_PER_BLOCK, PREP_BLOCKS_PER_STEP
    n_steps = N_BLOCKS // per_step
    batch = states_t[0].shape[1]
    slab = STATE_W // n_steps
    per_state = [a_re, a_im]
    per_channel = [jnp.swapaxes(b_re, 1, 2), jnp.swapaxes(b_im, 1, 2), c_re, c_im]
    ins = ([v.reshape(N_BLOCKS, gpb, SSM_STATE) for v in per_state] + [log_dt]
           + [v.reshape(N_BLOCKS, LANES, SSM_STATE) for v in per_channel] + list(states_t))
    in_specs = ([pl.BlockSpec((per_step, gpb, SSM_STATE), lambda i: (i, 0, 0))] * len(per_state)
                + [pl.BlockSpec((1, SSM_GROUPS), lambda i: (0, 0))]
                + [pl.BlockSpec((per_step, LANES, SSM_STATE), lambda i: (i, 0, 0))]
                * len(per_channel)
                + [pl.BlockSpec((slab, batch), lambda i: (i, 0))] * len(states_t))
    shapes = [(2 * LANES, 2 * BLOCK_STATE), (2 * BLOCK_STATE, 2 * LANES), (2 * LANES, 2 * LANES),
              (1, BLOCK_STATE), (1, BLOCK_STATE)]
    dtypes = [BF16, BF16, BF16, F32, F32]
    wb, wc, wd, a2_re, a2_im, *states = pl.pallas_call(
        _s5_weights_kernel,
        grid=(n_steps,), in_specs=in_specs,
        out_specs=([pl.BlockSpec((per_step,) + s, lambda i: (i, 0, 0)) for s in shapes]
                   + [pl.BlockSpec((batch, slab), lambda i: (0, i))] * len(states_t)),
        out_shape=([jax.ShapeDtypeStruct((N_BLOCKS,) + s, d) for s, d in zip(shapes, dtypes)]
                   + [jax.ShapeDtypeStruct((batch, STATE_W), F32)] * len(states_t)),
        name="s5_block_diag_weights",
    )(*ins)
    return a2_re, a2_im, wb, wc, wd, states


PROMPT_T = 32
SAMPLE_B = 32


def kernel(x_prompt, x_sample, state_ssm_re, state_ssm_im, state_pool, meta_tokens, norm_gain, w_in, b_gate, ssm_a_re, ssm_a_im, ssm_log_dt, ssm_b_re, ssm_b_im, ssm_c_re, ssm_c_im, ssm_d, w_glu, b_glu, pool_mix, pool_scale, w_branch_ssm, w_branch_pool, w_out, final_norm_gain):
    batch, seq, _ = x_prompt.shape
    dec_batch, dec_seq, _ = x_sample.shape
    depth = norm_gain.shape[0]
    assert depth == 1
    l = 0
    states_t = [jnp.transpose(s[l], (1, 2, 0)).reshape(STATE_W, dec_batch)
                for s in (state_ssm_re, state_ssm_im)]
    a2_re, a2_im, wb, wc, wd, (s0_re, s0_im) = _ssm_weights(
        ssm_a_re[l], ssm_a_im[l], ssm_log_dt.reshape(depth, 1, SSM_GROUPS)[l], ssm_b_re[l], ssm_b_im[l],
        ssm_c_re[l], ssm_c_im[l], states_t)
    row = lambda v: v.reshape(1, -1).astype(F32)
    weights = (
        row(norm_gain[l]), w_in[l].astype(BF16), row(b_gate[l]),
        a2_re, a2_im, wb, wc, wd, row(ssm_d[l]),
        w_glu[l].astype(BF16), row(b_glu[l]), pool_mix[l].astype(BF16), row(pool_scale[l]),
        w_branch_ssm[l].astype(BF16), w_branch_pool[l].astype(BF16), w_out[l].astype(BF16),
        row(final_norm_gain),
    )

    assert batch == SUBLANES and N_META <= PROMPT_T
    y_prompt, p_re, p_im, p_buf = _layer_call(
        x_prompt, None, weights, prefix=meta_tokens.astype(x_prompt.dtype),
        n_t=PROMPT_T, n_b=batch, carry=True, pos0=N_META - PROMPT_T, name="prompt_layer")

    sample_state = (s0_re, s0_im, jnp.swapaxes(state_pool[l], 0, 1))
    y_sample, s_re, s_im, s_buf = _layer_call(
        x_sample, sample_state, weights,
        n_t=dec_seq, n_b=SAMPLE_B, carry=False, pos0=PAST_LEN, name="sample_layer")

    state_shape = (1, -1, SSM_GROUPS, SSM_STATE)
    return (y_prompt, y_sample,
            p_re[None], p_im[None], jnp.swapaxes(p_buf, 0, 1)[None],
            s_re.reshape(state_shape), s_im.reshape(state_shape), jnp.swapaxes(s_buf, 0, 1)[None])
```

```python
import functools

import jax
import jax.numpy as jnp
from jax import lax
from jax.experimental import pallas as pl
from jax.experimental.pallas import tpu as pltpu

D_MODEL = 1024
N_META = 16
SSM_GROUP_CH = 16
SSM_GROUPS = D_MODEL // SSM_GROUP_CH
SSM_STATE = 64
POOL_WINDOWS = (2, 4, 8, 16)
POOL_GROUP_CH = D_MODEL // len(POOL_WINDOWS)
POOL_BUF = max(POOL_WINDOWS) - 1
EPS = 1e-6
PAST_LEN = 16384

SUBLANES = 8
LANES = 128
GROUPS_PER_BLOCK = LANES // SSM_GROUP_CH
N_BLOCKS = SSM_GROUPS // GROUPS_PER_BLOCK
BLOCK_STATE = GROUPS_PER_BLOCK * SSM_STATE
STATE_W = SSM_GROUPS * SSM_STATE
VMEM_LIMIT_BYTES = 56 * 1024 * 1024

F32 = jnp.float32
BF16 = jnp.bfloat16


def _rmsnorm(x, gain):
    ms = jnp.mean(x * x, axis=-1, keepdims=True)
    return (x * lax.rsqrt(ms + EPS)) * gain


_sigmoid = jax.nn.sigmoid


def _silu(x):
    return x * _sigmoid(x)


def _layer_kernel(*refs, n_t, n_b, carry, pos0, zero_state):
    n_side = 2 if carry else 0
    if carry:
        prefix_ref, x_ref, x_next_ref, *refs = refs
        side_in_refs, refs = refs[:n_side], refs[n_side:]
    else:
        x_ref, *refs = refs
    if not zero_state:
        s0re_ref, s0im_ref, buf0_ref, *refs = refs
    (gain_ref, w_in_ref, b_gate_ref, a2_re_ref, a2_im_ref, wb_ref, wc_ref, wd_ref, d_ref,
     w_glu_ref, b_glu_ref, pmix_ref, pscale_ref, w_bs_ref, w_bp_ref, w_out_ref, fgain_ref,
     y_ref, ore_ref, oim_ref, obuf_ref, *refs) = refs
    side_out_refs, refs = refs[:n_side], refs[n_side:]
    (sre_ref, sim_ref, ext_ref, bu_ref, u2_ref, xn_ref, us_ref, zs_ref, gs_ref, zp_ref,
     gp_ref, ys_ref, yp_ref) = refs
    m = n_t * n_b
    n_pairs = n_t // 2
    half = n_pairs * n_b
    step = pl.program_id(0)
    hist_rows = POOL_BUF * n_b

    def load_state():
        if zero_state:
            sre_ref[...] = jnp.zeros(sre_ref.shape, F32)
            sim_ref[...] = jnp.zeros(sim_ref.shape, F32)
            ext_ref[0:hist_rows, :] = jnp.zeros((hist_rows, D_MODEL), F32)
        else:
            sre_ref[...] = s0re_ref[...]
            sim_ref[...] = s0im_ref[...]
            ext_ref[0:hist_rows, :] = buf0_ref[...].reshape(hist_rows, D_MODEL)

    def rows_tb(src_ref):
        return jnp.swapaxes(src_ref[...], 0, 1).reshape(m, D_MODEL)

    def proj(k):
        return jnp.dot(xn_ref[...], w_in_ref[:, k * D_MODEL:(k + 1) * D_MODEL],
                       preferred_element_type=F32)

    def prepare_norm_rows(h):
        xn_ref[...] = _rmsnorm(h, gain_ref[...]).astype(BF16)

    def prepare_norm(src_ref):
        prepare_norm_rows(rows_tb(src_ref))

    def prefix_rows():
        tok = prefix_ref[...]
        n_tok = tok.shape[0]
        rows = jnp.broadcast_to(tok[:, None, :], (n_tok, n_b, D_MODEL)).reshape(n_tok * n_b, D_MODEL)
        return jnp.concatenate([jnp.zeros((m - n_tok * n_b, D_MODEL), F32), rows], axis=0)

    def prepare_u():
        us_ref[...] = proj(0).reshape(n_pairs, 2 * n_b, D_MODEL)

    def prepare_pairs():
        for blk in range(N_BLOCKS):
            lanes = slice(blk * LANES, (blk + 1) * LANES)
            st = slice(blk * BLOCK_STATE, (blk + 1) * BLOCK_STATE)
            u_t0 = us_ref[:, 0:n_b, lanes].reshape(half, LANES)
            u_t1 = us_ref[:, n_b:, lanes].reshape(half, LANES)
            u2_ref[blk] = jnp.concatenate([u_t0, u_t1], axis=1).astype(BF16)
            bu_ref[blk, 0:n_b, 0:BLOCK_STATE] = sre_ref[:, st]
            bu_ref[blk, 0:n_b, BLOCK_STATE:] = sim_ref[:, st]

    def prepare_slabs():
        for blk in range(N_BLOCKS):
            bu_ref[blk, n_b:, :] = jnp.dot(u2_ref[blk], wb_ref[blk], preferred_element_type=F32)

    def prepare(src_ref):
        prepare_norm(src_ref)
        prepare_u()
        prepare_pairs()
        prepare_slabs()

    if carry:
        @pl.when(step == 0)
        def _():
            load_state()
            prepare_norm_rows(prefix_rows())
            prepare_u()
            prepare_pairs()
            prepare_slabs()
        for src_ref, dst_ref in zip(side_in_refs, side_out_refs):
            dst_ref[...] = src_ref[...].T
    else:
        load_state()
        prepare(x_ref)

    piece = 2 * LANES

    def proj_piece(k, p):
        cols = slice(p * piece, (p + 1) * piece)
        v = jnp.dot(xn_ref[...], w_in_ref[:, k * D_MODEL + p * piece:k * D_MODEL + (p + 1) * piece],
                    preferred_element_type=F32)
        if k == 1:
            zs_ref[:, cols] = _silu(v)
        elif k == 4:
            gs_ref[:, cols] = _sigmoid(v + b_gate_ref[:, cols])
        elif k == 2:
            ext_ref[hist_rows:hist_rows + m, cols] = v
        elif k == 3:
            zp_ref[:, cols] = _silu(v)
        else:
            gp_ref[:, cols] = _sigmoid(
                v + b_gate_ref[:, D_MODEL + p * piece:D_MODEL + (p + 1) * piece])

    pieces = [(k, p) for k in (1, 4, 2, 3, 5) for p in range(D_MODEL // piece)]

    for blk in range(N_BLOCKS):
        lanes = slice(blk * LANES, (blk + 1) * LANES)
        st = slice(blk * BLOCK_STATE, (blk + 1) * BLOCK_STATE)
        for k, p in pieces[blk * len(pieces) // N_BLOCKS:(blk + 1) * len(pieces) // N_BLOCKS]:
            proj_piece(k, p)
        a_re = jnp.broadcast_to(a2_re_ref[blk], (SUBLANES, BLOCK_STATE))
        a_im = jnp.broadcast_to(a2_im_ref[blk], (SUBLANES, BLOCK_STATE))
        for j in range(n_b // SUBLANES):
            tile = slice(j * SUBLANES, (j + 1) * SUBLANES)
            s_re = sre_ref[tile, st]
            s_im = sim_ref[tile, st]
            for k in range(n_pairs):
                rows = slice((k + 1) * n_b + j * SUBLANES, (k + 1) * n_b + (j + 1) * SUBLANES)
                n_re = a_re * s_re - a_im * s_im + bu_ref[blk, rows, 0:BLOCK_STATE]
                n_im = a_re * s_im + a_im * s_re + bu_ref[blk, rows, BLOCK_STATE:]
                bu_ref[blk, rows, 0:BLOCK_STATE] = n_re
                bu_ref[blk, rows, BLOCK_STATE:] = n_im
                s_re, s_im = n_re, n_im
            sre_ref[tile, st] = s_re
            sim_ref[tile, st] = s_im
        y2 = (jnp.dot(bu_ref[blk, 0:half, :].astype(BF16), wc_ref[blk], preferred_element_type=F32)
              + jnp.dot(u2_ref[blk], wd_ref[blk], preferred_element_type=F32))
        ys_ref[:, 0:n_b, lanes] = y2[:, 0:LANES].reshape(n_pairs, n_b, LANES)
        ys_ref[:, n_b:, lanes] = y2[:, LANES:].reshape(n_pairs, n_b, LANES)

    if pos0 + 1 >= max(POOL_WINDOWS):
        pos = None
    else:
        t_loc = lax.broadcasted_iota(jnp.int32, (m, 1), 0) // n_b
        pos = pos0 + step * n_t + t_loc
    for gi, w in enumerate(POOL_WINDOWS):
        cols = slice(gi * POOL_GROUP_CH, (gi + 1) * POOL_GROUP_CH)
        s = ext_ref[(POOL_BUF + 1 - w) * n_b:hist_rows + m, cols]
        span = 1
        while span < w:
            keep = s.shape[0] - span * n_b
            s = s[:keep] + s[span * n_b:]
            span *= 2
        if pos is None:
            mean = s * (1.0 / w)
        else:
            cnt = jnp.maximum(jnp.minimum(pos + 1, w), 1)
            mean = s * (1.0 / cnt.astype(F32))
        pooled = mean - ext_ref[hist_rows:hist_rows + m, cols]
        yp_ref[:, cols] = jnp.dot(pooled.astype(BF16), pmix_ref[gi],
                                  preferred_element_type=F32) * pscale_ref[:, cols]
    for k in range(POOL_BUF):
        ext_ref[k * n_b:(k + 1) * n_b, :] = ext_ref[(k + n_t) * n_b:(k + n_t + 1) * n_b, :]

    y = ys_ref[...].reshape(m, D_MODEL) + d_ref[...] * us_ref[...].reshape(m, D_MODEL)
    y = jax.nn.gelu(y)
    b = yp_ref[...] * zp_ref[...]
    acc_p = jnp.dot(b.astype(BF16), w_bp_ref[...], preferred_element_type=F32)
    glu = jnp.dot(y.astype(BF16), w_glu_ref[...], preferred_element_type=F32)
    if carry:
        prepare_norm(x_next_ref)
    a = y * _sigmoid(glu + b_glu_ref[...]) * zs_ref[...]
    acc_s = jnp.dot(a.astype(BF16), w_bs_ref[...], preferred_element_type=F32)
    if carry:
        prepare_u()
        prepare_pairs()
    merged = gs_ref[...] * acc_s + gp_ref[...] * acc_p
    delta = jnp.dot(merged.astype(BF16), w_out_ref[...], preferred_element_type=F32)
    if carry:
        prepare_slabs()
    delta_bt = jnp.swapaxes(delta.reshape(n_t, n_b, D_MODEL), 0, 1)
    y_ref[...] = _rmsnorm(x_ref[...] + delta_bt, fgain_ref[...])

    def write_state():
        if carry:
            for g in range(SSM_GROUPS):
                ore_ref[:, g, :] = sre_ref[:, g * SSM_STATE:(g + 1) * SSM_STATE]
                oim_ref[:, g, :] = sim_ref[:, g * SSM_STATE:(g + 1) * SSM_STATE]
        else:
            ore_ref[...] = sre_ref[...]
            oim_ref[...] = sim_ref[...]
        obuf_ref[...] = ext_ref[0:hist_rows, :].reshape(POOL_BUF, n_b, D_MODEL)

    if carry:
        pl.when(step == pl.num_programs(0) - 1)(write_state)
    else:
        write_state()


def _layer_call(x, state, weights, *, n_t, n_b, carry, pos0, name, prefix=None, side=()):
    batch, time, _ = x.shape
    m = n_t * n_b
    x_block = (n_b, n_t, D_MODEL)
    buf_block = (POOL_BUF, n_b, D_MODEL)
    side_specs, side_shapes = [], []
    if carry:
        assert batch == n_b and time % n_t == 0 and prefix.shape[0] <= n_t
        n_blocks = time // n_t
        n_steps = n_blocks + 1
        side_rows, side_cols = side[0].shape
        n_slabs = side_cols // LANES
        assert len(side) == 2 and side_cols % LANES == 0 and n_slabs <= n_steps
        x_specs = [pl.BlockSpec(prefix.shape, lambda i: (0, 0)),
                   pl.BlockSpec(x_block, lambda i: (0, jnp.maximum(i - 1, 0), 0)),
                   pl.BlockSpec(x_block, lambda i: (0, jnp.minimum(i, n_blocks - 1), 0))]
        x_specs += [pl.BlockSpec((side_rows, LANES),
                                 lambda i: (0, jnp.minimum(i, n_slabs - 1)))] * len(side)
        side_specs = [pl.BlockSpec((LANES, side_rows),
                                   lambda i: (jnp.minimum(i, n_slabs - 1), 0))] * len(side)
        side_shapes = [jax.ShapeDtypeStruct((side_cols, side_rows), F32)] * len(side)
        y_map = lambda i: (0, jnp.maximum(i - 1, 0), 0)
        per_step = lambda i: (0, 0)
        per_step3 = lambda i: (0, 0, 0)
        xs = (prefix, x, x, *side)
        state_out = (batch, SSM_GROUPS, SSM_STATE)
        state_out_spec = pl.BlockSpec(state_out, per_step3)
    else:
        assert time == n_t and batch % n_b == 0
        n_steps = batch // n_b
        x_specs = [pl.BlockSpec(x_block, lambda i: (i, 0, 0))]
        y_map = lambda i: (i, 0, 0)
        per_step = lambda i: (i, 0)
        per_step3 = lambda i: (0, i, 0)
        xs = (x,)
        state_out = (batch, STATE_W)
        state_out_spec = pl.BlockSpec((n_b, STATE_W), per_step)
    whole = pl.BlockSpec(memory_space=pltpu.VMEM)
    state_specs = [
        pl.BlockSpec((n_b, STATE_W), per_step),
        pl.BlockSpec((n_b, STATE_W), per_step),
        pl.BlockSpec(buf_block, per_step3),
    ]
    kernel = functools.partial(_layer_kernel, n_t=n_t, n_b=n_b, carry=carry, pos0=pos0,
                               zero_state=state is None)
    return pl.pallas_call(
        kernel,
        grid=(n_steps,),
        in_specs=x_specs + ([] if state is None else state_specs) + [whole] * len(weights),
        out_specs=[pl.BlockSpec(x_block, y_map), state_out_spec, state_out_spec, state_specs[2]]
        + side_specs,
        out_shape=[
            jax.ShapeDtypeStruct((batch, time, D_MODEL), F32),
            jax.ShapeDtypeStruct(state_out, F32),
            jax.ShapeDtypeStruct(state_out, F32),
            jax.ShapeDtypeStruct((POOL_BUF, batch, D_MODEL), F32),
        ] + side_shapes,
        scratch_shapes=[
            pltpu.VMEM((n_b, STATE_W), F32),
            pltpu.VMEM((n_b, STATE_W), F32),
            pltpu.VMEM(((POOL_BUF + n_t) * n_b, D_MODEL), F32),
            pltpu.VMEM((N_BLOCKS, m // 2 + n_b, 2 * BLOCK_STATE), F32),
            pltpu.VMEM((N_BLOCKS, m // 2, 2 * LANES), BF16),
            pltpu.VMEM((m, D_MODEL), BF16),
            pltpu.VMEM((n_t // 2, 2 * n_b, D_MODEL), F32),
            pltpu.VMEM((m, D_MODEL), F32),
            pltpu.VMEM((m, D_MODEL), F32),
            pltpu.VMEM((m, D_MODEL), F32),
            pltpu.VMEM((m, D_MODEL), F32),
            pltpu.VMEM((n_t // 2, 2 * n_b, D_MODEL), F32),
            pltpu.VMEM((m, D_MODEL), F32),
        ],
        compiler_params=pltpu.CompilerParams(
            dimension_semantics=("arbitrary",), vmem_limit_bytes=VMEM_LIMIT_BYTES),
        name=name,
    )(*xs, *(() if state is None else state), *weights)


PREP_BLOCKS_PER_STEP = 2


def _s5_weights_kernel(a_re_ref, a_im_ref, log_dt_ref, bt_re_ref, bt_im_ref, c_re_ref, c_im_ref,
                       sre_t_ref, sim_t_ref,
                       wb_ref, wc_ref, wd_ref, a2_re_ref, a2_im_ref, sre_ref, sim_ref):
    gpb, n_c, n_p = GROUPS_PER_BLOCK, SSM_GROUP_CH, SSM_STATE
    lane = lax.broadcasted_iota(jnp.int32, (n_p, BLOCK_STATE), 1)
    sel = jnp.where(jnp.bitwise_and(lane, n_p - 1)
                    == lax.broadcasted_iota(jnp.int32, (n_p, BLOCK_STATE), 0), 1.0, 0.0).astype(BF16)

    def group_of(shape, axis, width):
        idx = jnp.right_shift(lax.broadcasted_iota(jnp.int32, shape, axis), width.bit_length() - 1)
        return jnp.bitwise_and(idx, gpb - 1)

    n_stack = 4
    same_group = (group_of((n_stack * LANES, BLOCK_STATE), 0, n_c)
                  == group_of((n_stack * LANES, BLOCK_STATE), 1, n_p))
    same_group_cc = (group_of((LANES, 2 * LANES), 0, n_c) == group_of((LANES, 2 * LANES), 1, n_c))

    def per_row(v):
        return jnp.broadcast_to(v[:, None, :], (gpb, n_c, n_p)).reshape(LANES, n_p)

    def expand(mats):
        stacked = jnp.concatenate(mats, axis=0).astype(BF16)
        tiled = jnp.where(same_group, jnp.dot(stacked, sel, preferred_element_type=F32), 0.0)
        return [tiled[k * LANES:(k + 1) * LANES] for k in range(n_stack)]

    def state_row(v):
        tiled = jnp.dot(v, sel.astype(F32), precision=lax.Precision.HIGHEST,
                        preferred_element_type=F32)
        own = (jnp.right_shift(lax.broadcasted_iota(jnp.int32, (gpb, BLOCK_STATE), 1),
                               n_p.bit_length() - 1)
               == lax.broadcasted_iota(jnp.int32, (gpb, BLOCK_STATE), 0))
        return jnp.sum(jnp.where(own, tiled, 0.0), axis=0, keepdims=True)

    def gram(x, y):
        return lax.dot_general(x, y, (((1,), (1,)), ((), ())), precision=lax.Precision.HIGHEST,
                               preferred_element_type=F32)

    for blk in range(PREP_BLOCKS_PER_STEP):
        a_re, a_im = a_re_ref[blk], a_im_ref[blk]
        first_group = (pl.program_id(0) * PREP_BLOCKS_PER_STEP + blk) * gpb
        own_dt = (lax.broadcasted_iota(jnp.int32, (gpb, SSM_GROUPS), 1)
                  == first_group + lax.broadcasted_iota(jnp.int32, (gpb, SSM_GROUPS), 0))
        dt = jnp.exp(jnp.sum(jnp.where(own_dt, log_dt_ref[...], 0.0), axis=1, keepdims=True))
        mag = jnp.exp(dt * a_re)
        ang = dt * a_im
        abar_re = mag * jnp.cos(ang)
        abar_im = mag * jnp.sin(ang)
        den = a_re * a_re + a_im * a_im
        nr = abar_re - 1.0
        ni = abar_im
        q_re = (nr * a_re + ni * a_im) / den
        q_im = (ni * a_re - nr * a_im) / den
        a2_re = abar_re * abar_re - abar_im * abar_im
        a2_im = 2.0 * abar_re * abar_im
        a2_re_ref[blk] = state_row(a2_re)
        a2_im_ref[blk] = state_row(a2_im)

        ar, ai = per_row(abar_re), per_row(abar_im)
        ar2, ai2 = per_row(a2_re), per_row(a2_im)
        qr, qi = per_row(q_re), per_row(q_im)
        bt_re, bt_im = bt_re_ref[blk], bt_im_ref[blk]
        c_re, c_im = c_re_ref[blk], c_im_ref[blk]
        bbar_re = qr * bt_re - qi * bt_im
        bbar_im = qr * bt_im + qi * bt_re
        ab_re = ar * bbar_re - ai * bbar_im
        ab_im = ar * bbar_im + ai * bbar_re
        ca_re = c_re * ar - c_im * ai
        ca_im = c_re * ai + c_im * ar
        ca2_re = c_re * ar2 - c_im * ai2
        ca2_im = c_re * ai2 + c_im * ar2
        direct = (gram(bbar_re, jnp.concatenate([c_re, ca_re], axis=0))
                  - gram(bbar_im, jnp.concatenate([c_im, ca_im], axis=0)))
        direct = jnp.where(same_group_cc, direct, 0.0).astype(BF16)

        e_ab_re, e_ab_im, e_bb_re, e_bb_im = expand([ab_re, ab_im, bbar_re, bbar_im])
        wb_ref[blk, 0:LANES, 0:BLOCK_STATE] = e_ab_re.astype(BF16)
        wb_ref[blk, 0:LANES, BLOCK_STATE:] = e_ab_im.astype(BF16)
        wb_ref[blk, LANES:, 0:BLOCK_STATE] = e_bb_re.astype(BF16)
        wb_ref[blk, LANES:, BLOCK_STATE:] = e_bb_im.astype(BF16)
        e_ca_re, e_ca2_re, e_ca_im, e_ca2_im = expand([ca_re, ca2_re, -ca_im, -ca2_im])
        wc_ref[blk, 0:BLOCK_STATE, 0:LANES] = e_ca_re.T.astype(BF16)
        wc_ref[blk, 0:BLOCK_STATE, LANES:] = e_ca2_re.T.astype(BF16)
        wc_ref[blk, BLOCK_STATE:, 0:LANES] = e_ca_im.T.astype(BF16)
        wc_ref[blk, BLOCK_STATE:, LANES:] = e_ca2_im.T.astype(BF16)
        wd_ref[blk, 0:LANES, :] = direct
        wd_ref[blk, LANES:, 0:LANES] = jnp.zeros((LANES, LANES), BF16)
        wd_ref[blk, LANES:, LANES:] = direct[:, 0:LANES]

    sre_ref[...] = sre_t_ref[...].T
    sim_ref[...] = sim_t_ref[...].T


def _ssm_weights(a_re, a_im, log_dt, b_re, b_im, c_re, c_im, states_t):
    gpb, per_step = GROUPS_PER_BLOCK, PREP_BLOCKS_PER_STEP
    n_steps = N_BLOCKS // per_step
    batch = states_t[0].shape[1]
    slab = STATE_W // n_steps
    per_state = [a_re, a_im]
    per_channel = [jnp.swapaxes(b_re, 1, 2), jnp.swapaxes(b_im, 1, 2), c_re, c_im]
    ins = ([v.reshape(N_BLOCKS, gpb, SSM_STATE) for v in per_state] + [log_dt]
           + [v.reshape(N_BLOCKS, LANES, SSM_STATE) for v in per_channel] + list(states_t))
    in_specs = ([pl.BlockSpec((per_step, gpb, SSM_STATE), lambda i: (i, 0, 0))] * len(per_state)
                + [pl.BlockSpec((1, SSM_GROUPS), lambda i: (0, 0))]
                + [pl.BlockSpec((per_step, LANES, SSM_STATE), lambda i: (i, 0, 0))]
                * len(per_channel)
                + [pl.BlockSpec((slab, batch), lambda i: (i, 0))] * len(states_t))
    shapes = [(2 * LANES, 2 * BLOCK_STATE), (2 * BLOCK_STATE, 2 * LANES), (2 * LANES, 2 * LANES),
              (1, BLOCK_STATE), (1, BLOCK_STATE)]
    dtypes = [BF16, BF16, BF16, F32, F32]
    wb, wc, wd, a2_re, a2_im, *states = pl.pallas_call(
        _s5_weights_kernel,
        grid=(n_steps,), in_specs=in_specs,
        out_specs=([pl.BlockSpec((per_step,) + s, lambda i: (i, 0, 0)) for s in shapes]
                   + [pl.BlockSpec((batch, slab), lambda i: (0, i))] * len(states_t)),
        out_shape=([jax.ShapeDtypeStruct((N_BLOCKS,) + s, d) for s, d in zip(shapes, dtypes)]
                   + [jax.ShapeDtypeStruct((batch, STATE_W), F32)] * len(states_t)),
        name="s5_block_diag_weights",
    )(*ins)
    return a2_re, a2_im, wb, wc, wd, states


PROMPT_T = 32
SAMPLE_B = 32


def kernel(x_prompt, x_sample, state_ssm_re, state_ssm_im, state_pool, meta_tokens, norm_gain, w_in, b_gate, ssm_a_re, ssm_a_im, ssm_log_dt, ssm_b_re, ssm_b_im, ssm_c_re, ssm_c_im, ssm_d, w_glu, b_glu, pool_mix, pool_scale, w_branch_ssm, w_branch_pool, w_out, final_norm_gain):
    batch, seq, _ = x_prompt.shape
    dec_batch, dec_seq, _ = x_sample.shape
    depth = norm_gain.shape[0]
    assert depth == 1
    l = 0
    states_t = [jnp.transpose(s[l], (1, 2, 0)).reshape(STATE_W, dec_batch)
                for s in (state_ssm_re, state_ssm_im)]
    a2_re, a2_im, wb, wc, wd, (s0_re, s0_im) = _ssm_weights(
        ssm_a_re[l], ssm_a_im[l], ssm_log_dt.reshape(depth, 1, SSM_GROUPS)[l], ssm_b_re[l], ssm_b_im[l],
        ssm_c_re[l], ssm_c_im[l], states_t)
    row = lambda v: v.reshape(1, -1).astype(F32)
    weights = (
        row(norm_gain[l]), w_in[l].astype(BF16), row(b_gate[l]),
        a2_re, a2_im, wb, wc, wd, row(ssm_d[l]),
        w_glu[l].astype(BF16), row(b_glu[l]), pool_mix[l].astype(BF16), row(pool_scale[l]),
        w_branch_ssm[l].astype(BF16), w_branch_pool[l].astype(BF16), w_out[l].astype(BF16),
        row(final_norm_gain),
    )

    sample_state = (s0_re, s0_im, jnp.swapaxes(state_pool[l], 0, 1))
    y_sample, s_re, s_im, s_buf = _layer_call(
        x_sample, sample_state, weights,
        n_t=dec_seq, n_b=SAMPLE_B, carry=False, pos0=PAST_LEN, name="sample_layer")

    assert batch == SUBLANES and N_META <= PROMPT_T
    y_prompt, p_re, p_im, p_buf, s_re_t, s_im_t = _layer_call(
        x_prompt, None, weights, prefix=meta_tokens.astype(x_prompt.dtype), side=(s_re, s_im),
        n_t=PROMPT_T, n_b=batch, carry=True, pos0=N_META - PROMPT_T, name="prompt_layer")

    sample_states = [jnp.transpose(s.reshape(SSM_GROUPS, SSM_STATE, dec_batch), (2, 0, 1))[None]
                     for s in (s_re_t, s_im_t)]
    return (y_prompt, y_sample,
            p_re[None], p_im[None], jnp.swapaxes(p_buf, 0, 1)[None],
            *sample_states, jnp.swapaxes(s_buf, 0, 1)[None])
```

```python
import functools

import jax
import jax.numpy as jnp
from jax import lax
from jax.experimental import pallas as pl
from jax.experimental.pallas import tpu as pltpu

D_MODEL = 1024
N_META = 16
SSM_GROUP_CH = 16
SSM_GROUPS = D_MODEL // SSM_GROUP_CH
SSM_STATE = 64
POOL_WINDOWS = (2, 4, 8, 16)
POOL_GROUP_CH = D_MODEL // len(POOL_WINDOWS)
POOL_BUF = max(POOL_WINDOWS) - 1
EPS = 1e-6
PAST_LEN = 16384

SUBLANES = 8
LANES = 128
GROUPS_PER_BLOCK = LANES // SSM_GROUP_CH
N_BLOCKS = SSM_GROUPS // GROUPS_PER_BLOCK
BLOCK_STATE = GROUPS_PER_BLOCK * SSM_STATE
STATE_W = SSM_GROUPS * SSM_STATE
VMEM_LIMIT_BYTES = 56 * 1024 * 1024

F32 = jnp.float32
BF16 = jnp.bfloat16


def _rmsnorm(x, gain):
    ms = jnp.mean(x * x, axis=-1, keepdims=True)
    return (x * lax.rsqrt(ms + EPS)) * gain


_sigmoid = jax.nn.sigmoid


def _silu(x):
    return x * _sigmoid(x)


def _layer_kernel(*refs, n_t, n_b, carry, pos0, zero_state):
    n_side = 2 if carry else 0
    if carry:
        prefix_ref, x_ref, x_next_ref, *refs = refs
        side_in_refs, refs = refs[:n_side], refs[n_side:]
    else:
        x_ref, *refs = refs
    if not zero_state:
        s0re_ref, s0im_ref, buf0_ref, *refs = refs
    (gain_ref, w_in_ref, b_gate_ref, a2_re_ref, a2_im_ref, wb_ref, wc_ref, wd_ref, d_ref,
     w_glu_ref, b_glu_ref, pmix_ref, pscale_ref, w_bs_ref, w_bp_ref, w_out_ref, fgain_ref,
     y_ref, ore_ref, oim_ref, obuf_ref, *refs) = refs
    side_out_refs, refs = refs[:n_side], refs[n_side:]
    (sre_ref, sim_ref, ext_ref, bu_ref, u2_ref, xn_ref, us_ref, zs_ref, gs_ref, zp_ref,
     gp_ref, ys_ref, yp_ref) = refs
    m = n_t * n_b
    n_pairs = n_t // 2
    half = n_pairs * n_b
    step = pl.program_id(0)
    hist_rows = POOL_BUF * n_b

    def load_state():
        if zero_state:
            sre_ref[...] = jnp.zeros(sre_ref.shape, F32)
            sim_ref[...] = jnp.zeros(sim_ref.shape, F32)
            ext_ref[0:hist_rows, :] = jnp.zeros((hist_rows, D_MODEL), F32)
        else:
            sre_ref[...] = s0re_ref[...]
            sim_ref[...] = s0im_ref[...]
            ext_ref[0:hist_rows, :] = buf0_ref[...].reshape(hist_rows, D_MODEL)

    def rows_tb(src_ref):
        return jnp.swapaxes(src_ref[...], 0, 1).reshape(m, D_MODEL)

    def proj(k):
        return jnp.dot(xn_ref[...], w_in_ref[:, k * D_MODEL:(k + 1) * D_MODEL],
                       preferred_element_type=F32)

    def prepare_norm_rows(h):
        xn_ref[...] = _rmsnorm(h, gain_ref[...]).astype(BF16)

    def prepare_norm(src_ref):
        prepare_norm_rows(rows_tb(src_ref))

    def prefix_rows():
        tok = prefix_ref[...]
        n_tok = tok.shape[0]
        rows = jnp.broadcast_to(tok[:, None, :], (n_tok, n_b, D_MODEL)).reshape(n_tok * n_b, D_MODEL)
        return jnp.concatenate([jnp.zeros((m - n_tok * n_b, D_MODEL), F32), rows], axis=0)

    def prepare_u():
        us_ref[...] = proj(0).reshape(n_pairs, 2 * n_b, D_MODEL)

    def prepare_pairs():
        for blk in range(N_BLOCKS):
            lanes = slice(blk * LANES, (blk + 1) * LANES)
            st = slice(blk * BLOCK_STATE, (blk + 1) * BLOCK_STATE)
            u_t0 = us_ref[:, 0:n_b, lanes].reshape(half, LANES)
            u_t1 = us_ref[:, n_b:, lanes].reshape(half, LANES)
            u2_ref[blk] = jnp.concatenate([u_t0, u_t1], axis=1).astype(BF16)
            bu_ref[blk, 0:n_b, 0:BLOCK_STATE] = sre_ref[:, st]
            bu_ref[blk, 0:n_b, BLOCK_STATE:] = sim_ref[:, st]

    def prepare_slabs():
        for blk in range(N_BLOCKS):
            bu_ref[blk, n_b:, :] = jnp.dot(u2_ref[blk], wb_ref[blk], preferred_element_type=F32)

    def prepare(src_ref):
        prepare_norm(src_ref)
        prepare_u()
        prepare_pairs()
        prepare_slabs()

    if carry:
        @pl.when(step == 0)
        def _():
            load_state()
            prepare_norm_rows(prefix_rows())
            prepare_u()
            prepare_pairs()
            prepare_slabs()
        for src_ref, dst_ref in zip(side_in_refs, side_out_refs):
            dst_ref[...] = src_ref[...].T
    else:
        load_state()
        prepare(x_ref)

    piece = 2 * LANES

    def proj_piece(k, p):
        cols = slice(p * piece, (p + 1) * piece)
        v = jnp.dot(xn_ref[...], w_in_ref[:, k * D_MODEL + p * piece:k * D_MODEL + (p + 1) * piece],
                    preferred_element_type=F32)
        if k == 1:
            zs_ref[:, cols] = _silu(v)
        elif k == 4:
            gs_ref[:, cols] = _sigmoid(v + b_gate_ref[:, cols])
        elif k == 2:
            ext_ref[hist_rows:hist_rows + m, cols] = v
        elif k == 3:
            zp_ref[:, cols] = _silu(v)
        else:
            gp_ref[:, cols] = _sigmoid(
                v + b_gate_ref[:, D_MODEL + p * piece:D_MODEL + (p + 1) * piece])

    pieces = [(k, p) for k in (1, 4, 2, 3, 5) for p in range(D_MODEL // piece)]

    for blk in range(N_BLOCKS):
        lanes = slice(blk * LANES, (blk + 1) * LANES)
        st = slice(blk * BLOCK_STATE, (blk + 1) * BLOCK_STATE)
        for k, p in pieces[blk * len(pieces) // N_BLOCKS:(blk + 1) * len(pieces) // N_BLOCKS]:
            proj_piece(k, p)
        a_re = jnp.broadcast_to(a2_re_ref[blk], (SUBLANES, BLOCK_STATE))
        a_im = jnp.broadcast_to(a2_im_ref[blk], (SUBLANES, BLOCK_STATE))
        for j in range(n_b // SUBLANES):
            tile = slice(j * SUBLANES, (j + 1) * SUBLANES)
            s_re = sre_ref[tile, st]
            s_im = sim_ref[tile, st]
            for k in range(n_pairs):
                rows = slice((k + 1) * n_b + j * SUBLANES, (k + 1) * n_b + (j + 1) * SUBLANES)
                n_re = a_re * s_re - a_im * s_im + bu_ref[blk, rows, 0:BLOCK_STATE]
                n_im = a_re * s_im + a_im * s_re + bu_ref[blk, rows, BLOCK_STATE:]
                bu_ref[blk, rows, 0:BLOCK_STATE] = n_re
                bu_ref[blk, rows, BLOCK_STATE:] = n_im
                s_re, s_im = n_re, n_im
            sre_ref[tile, st] = s_re
            sim_ref[tile, st] = s_im
        y2 = (jnp.dot(bu_ref[blk, 0:half, :].astype(BF16), wc_ref[blk], preferred_element_type=F32)
              + jnp.dot(u2_ref[blk], wd_ref[blk], preferred_element_type=F32))
        ys_ref[:, 0:n_b, lanes] = y2[:, 0:LANES].reshape(n_pairs, n_b, LANES)
        ys_ref[:, n_b:, lanes] = y2[:, LANES:].reshape(n_pairs, n_b, LANES)

    if pos0 + 1 >= max(POOL_WINDOWS):
        pos = None
    else:
        t_loc = lax.broadcasted_iota(jnp.int32, (m, 1), 0) // n_b
        pos = pos0 + step * n_t + t_loc
    for gi, w in enumerate(POOL_WINDOWS):
        cols = slice(gi * POOL_GROUP_CH, (gi + 1) * POOL_GROUP_CH)
        s = ext_ref[(POOL_BUF + 1 - w) * n_b:hist_rows + m, cols]
        span = 1
        while span < w:
            keep = s.shape[0] - span * n_b
            s = s[:keep] + s[span * n_b:]
            span *= 2
        if pos is None:
            mean = s * (1.0 / w)
        else:
            cnt = jnp.maximum(jnp.minimum(pos + 1, w), 1)
            mean = s * (1.0 / cnt.astype(F32))
        pooled = mean - ext_ref[hist_rows:hist_rows + m, cols]
        yp_ref[:, cols] = jnp.dot(pooled.astype(BF16), pmix_ref[gi],
                                  preferred_element_type=F32) * pscale_ref[:, cols]
    for k in range(POOL_BUF):
        ext_ref[k * n_b:(k + 1) * n_b, :] = ext_ref[(k + n_t) * n_b:(k + n_t + 1) * n_b, :]

    y = ys_ref[...].reshape(m, D_MODEL) + d_ref[...] * us_ref[...].reshape(m, D_MODEL)
    y = jax.nn.gelu(y)
    b = yp_ref[...] * zp_ref[...]
    acc_p = jnp.dot(b.astype(BF16), w_bp_ref[...], preferred_element_type=F32)
    glu = jnp.dot(y.astype(BF16), w_glu_ref[...], preferred_element_type=F32)
    if carry:
        prepare_norm(x_next_ref)
    a = y * _sigmoid(glu + b_glu_ref[...]) * zs_ref[...]
    acc_s = jnp.dot(a.astype(BF16), w_bs_ref[...], preferred_element_type=F32)
    if carry:
        prepare_u()
        prepare_pairs()
    merged = gs_ref[...] * acc_s + gp_ref[...] * acc_p
    delta = jnp.dot(merged.astype(BF16), w_out_ref[...], preferred_element_type=F32)
    if carry:
        prepare_slabs()
    delta_bt = jnp.swapaxes(delta.reshape(n_t, n_b, D_MODEL), 0, 1)
    y_ref[...] = _rmsnorm(x_ref[...] + delta_bt, fgain_ref[...])

    def write_state():
        if carry:
            for g in range(SSM_GROUPS):
                ore_ref[:, g, :] = sre_ref[:, g * SSM_STATE:(g + 1) * SSM_STATE]
                oim_ref[:, g, :] = sim_ref[:, g * SSM_STATE:(g + 1) * SSM_STATE]
        else:
            ore_ref[...] = sre_ref[...]
            oim_ref[...] = sim_ref[...]
        obuf_ref[...] = ext_ref[0:hist_rows, :].reshape(POOL_BUF, n_b, D_MODEL)

    if carry:
        pl.when(step == pl.num_programs(0) - 1)(write_state)
    else:
        write_state()


def _layer_call(x, state, weights, *, n_t, n_b, carry, pos0, name, prefix=None, side=()):
    batch, time, _ = x.shape
    m = n_t * n_b
    x_block = (n_b, n_t, D_MODEL)
    buf_block = (POOL_BUF, n_b, D_MODEL)
    side_specs, side_shapes = [], []
    if carry:
        assert batch == n_b and time % n_t == 0 and prefix.shape[0] <= n_t
        n_blocks = time // n_t
        n_steps = n_blocks + 1
        side_rows, side_cols = side[0].shape
        n_slabs = side_cols // LANES
        assert len(side) == 2 and side_cols % LANES == 0 and n_slabs <= n_steps
        x_specs = [pl.BlockSpec(prefix.shape, lambda i: (0, 0)),
                   pl.BlockSpec(x_block, lambda i: (0, jnp.maximum(i - 1, 0), 0)),
                   pl.BlockSpec(x_block, lambda i: (0, jnp.minimum(i, n_blocks - 1), 0))]
        x_specs += [pl.BlockSpec((side_rows, LANES),
                                 lambda i: (0, jnp.minimum(i, n_slabs - 1)))] * len(side)
        side_specs = [pl.BlockSpec((LANES, side_rows),
                                   lambda i: (jnp.minimum(i, n_slabs - 1), 0))] * len(side)
        side_shapes = [jax.ShapeDtypeStruct((side_cols, side_rows), F32)] * len(side)
        y_map = lambda i: (0, jnp.maximum(i - 1, 0), 0)
        per_step = lambda i: (0, 0)
        per_step3 = lambda i: (0, 0, 0)
        xs = (prefix, x, x, *side)
        state_out = (batch, SSM_GROUPS, SSM_STATE)
        state_out_spec = pl.BlockSpec(state_out, per_step3)
    else:
        assert time == n_t and batch % n_b == 0
        n_steps = batch // n_b
        x_specs = [pl.BlockSpec(x_block, lambda i: (i, 0, 0))]
        y_map = lambda i: (i, 0, 0)
        per_step = lambda i: (i, 0)
        per_step3 = lambda i: (0, i, 0)
        xs = (x,)
        state_out = (batch, STATE_W)
        state_out_spec = pl.BlockSpec((n_b, STATE_W), per_step)
    whole = pl.BlockSpec(memory_space=pltpu.VMEM)
    state_specs = [
        pl.BlockSpec((n_b, STATE_W), per_step),
        pl.BlockSpec((n_b, STATE_W), per_step),
        pl.BlockSpec(buf_block, per_step3),
    ]
    kernel = functools.partial(_layer_kernel, n_t=n_t, n_b=n_b, carry=carry, pos0=pos0,
                               zero_state=state is None)
    return pl.pallas_call(
        kernel,
        grid=(n_steps,),
        in_specs=x_specs + ([] if state is None else state_specs) + [whole] * len(weights),
        out_specs=[pl.BlockSpec(x_block, y_map), state_out_spec, state_out_spec, state_specs[2]]
        + side_specs,
        out_shape=[
            jax.ShapeDtypeStruct((batch, time, D_MODEL), F32),
            jax.ShapeDtypeStruct(state_out, F32),
            jax.ShapeDtypeStruct(state_out, F32),
            jax.ShapeDtypeStruct((POOL_BUF, batch, D_MODEL), F32),
        ] + side_shapes,
        scratch_shapes=[
            pltpu.VMEM((n_b, STATE_W), F32),
            pltpu.VMEM((n_b, STATE_W), F32),
            pltpu.VMEM(((POOL_BUF + n_t) * n_b, D_MODEL), F32),
            pltpu.VMEM((N_BLOCKS, m // 2 + n_b, 2 * BLOCK_STATE), F32),
            pltpu.VMEM((N_BLOCKS, m // 2, 2 * LANES), BF16),
            pltpu.VMEM((m, D_MODEL), BF16),
            pltpu.VMEM((n_t // 2, 2 * n_b, D_MODEL), F32),
            pltpu.VMEM((m, D_MODEL), F32),
            pltpu.VMEM((m, D_MODEL), F32),
            pltpu.VMEM((m, D_MODEL), F32),
            pltpu.VMEM((m, D_MODEL), F32),
            pltpu.VMEM((n_t // 2, 2 * n_b, D_MODEL), F32),
            pltpu.VMEM((m, D_MODEL), F32),
        ],
        compiler_params=pltpu.CompilerParams(
            dimension_semantics=("arbitrary",), vmem_limit_bytes=VMEM_LIMIT_BYTES),
        name=name,
    )(*xs, *(() if state is None else state), *weights)


PREP_BLOCKS_PER_STEP = 2


def _s5_weights_kernel(*refs, n_dense):
    (a_re_ref, a_im_ref, log_dt_ref, bt_re_ref, bt_im_ref, c_re_ref, c_im_ref,
     sre_t_ref, sim_t_ref, *refs) = refs
    dense_refs, refs = refs[:n_dense], refs[n_dense:]
    wb_ref, wc_ref, wd_ref, a2_re_ref, a2_im_ref, sre_ref, sim_ref, *dense_bf16_refs = refs
    gpb, n_c, n_p = GROUPS_PER_BLOCK, SSM_GROUP_CH, SSM_STATE
    lane = lax.broadcasted_iota(jnp.int32, (n_p, BLOCK_STATE), 1)
    sel = jnp.where(jnp.bitwise_and(lane, n_p - 1)
                    == lax.broadcasted_iota(jnp.int32, (n_p, BLOCK_STATE), 0), 1.0, 0.0).astype(BF16)

    def group_of(shape, axis, width):
        idx = jnp.right_shift(lax.broadcasted_iota(jnp.int32, shape, axis), width.bit_length() - 1)
        return jnp.bitwise_and(idx, gpb - 1)

    n_stack = 4
    same_group = (group_of((n_stack * LANES, BLOCK_STATE), 0, n_c)
                  == group_of((n_stack * LANES, BLOCK_STATE), 1, n_p))
    same_group_cc = (group_of((LANES, 2 * LANES), 0, n_c) == group_of((LANES, 2 * LANES), 1, n_c))

    def per_row(v):
        return jnp.broadcast_to(v[:, None, :], (gpb, n_c, n_p)).reshape(LANES, n_p)

    def expand(mats):
        stacked = jnp.concatenate(mats, axis=0).astype(BF16)
        tiled = jnp.where(same_group, jnp.dot(stacked, sel, preferred_element_type=F32), 0.0)
        return [tiled[k * LANES:(k + 1) * LANES] for k in range(n_stack)]

    def state_row(v):
        tiled = jnp.dot(v, sel.astype(F32), precision=lax.Precision.HIGHEST,
                        preferred_element_type=F32)
        own = (jnp.right_shift(lax.broadcasted_iota(jnp.int32, (gpb, BLOCK_STATE), 1),
                               n_p.bit_length() - 1)
               == lax.broadcasted_iota(jnp.int32, (gpb, BLOCK_STATE), 0))
        return jnp.sum(jnp.where(own, tiled, 0.0), axis=0, keepdims=True)

    def gram(x, y):
        return lax.dot_general(x, y, (((1,), (1,)), ((), ())), precision=lax.Precision.HIGHEST,
                               preferred_element_type=F32)

    for blk in range(PREP_BLOCKS_PER_STEP):
        a_re, a_im = a_re_ref[blk], a_im_ref[blk]
        first_group = (pl.program_id(0) * PREP_BLOCKS_PER_STEP + blk) * gpb
        own_dt = (lax.broadcasted_iota(jnp.int32, (gpb, SSM_GROUPS), 1)
                  == first_group + lax.broadcasted_iota(jnp.int32, (gpb, SSM_GROUPS), 0))
        dt = jnp.exp(jnp.sum(jnp.where(own_dt, log_dt_ref[...], 0.0), axis=1, keepdims=True))
        mag = jnp.exp(dt * a_re)
        ang = dt * a_im
        abar_re = mag * jnp.cos(ang)
        abar_im = mag * jnp.sin(ang)
        den = a_re * a_re + a_im * a_im
        nr = abar_re - 1.0
        ni = abar_im
        q_re = (nr * a_re + ni * a_im) / den
        q_im = (ni * a_re - nr * a_im) / den
        a2_re = abar_re * abar_re - abar_im * abar_im
        a2_im = 2.0 * abar_re * abar_im
        a2_re_ref[blk] = state_row(a2_re)
        a2_im_ref[blk] = state_row(a2_im)

        ar, ai = per_row(abar_re), per_row(abar_im)
        ar2, ai2 = per_row(a2_re), per_row(a2_im)
        qr, qi = per_row(q_re), per_row(q_im)
        bt_re, bt_im = bt_re_ref[blk], bt_im_ref[blk]
        c_re, c_im = c_re_ref[blk], c_im_ref[blk]
        bbar_re = qr * bt_re - qi * bt_im
        bbar_im = qr * bt_im + qi * bt_re
        ab_re = ar * bbar_re - ai * bbar_im
        ab_im = ar * bbar_im + ai * bbar_re
        ca_re = c_re * ar - c_im * ai
        ca_im = c_re * ai + c_im * ar
        ca2_re = c_re * ar2 - c_im * ai2
        ca2_im = c_re * ai2 + c_im * ar2
        direct = (gram(bbar_re, jnp.concatenate([c_re, ca_re], axis=0))
                  - gram(bbar_im, jnp.concatenate([c_im, ca_im], axis=0)))
        direct = jnp.where(same_group_cc, direct, 0.0).astype(BF16)

        e_ab_re, e_ab_im, e_bb_re, e_bb_im = expand([ab_re, ab_im, bbar_re, bbar_im])
        wb_ref[blk, 0:LANES, 0:BLOCK_STATE] = e_ab_re.astype(BF16)
        wb_ref[blk, 0:LANES, BLOCK_STATE:] = e_ab_im.astype(BF16)
        wb_ref[blk, LANES:, 0:BLOCK_STATE] = e_bb_re.astype(BF16)
        wb_ref[blk, LANES:, BLOCK_STATE:] = e_bb_im.astype(BF16)
        e_ca_re, e_ca2_re, e_ca_im, e_ca2_im = expand([ca_re, ca2_re, -ca_im, -ca2_im])
        wc_ref[blk, 0:BLOCK_STATE, 0:LANES] = e_ca_re.T.astype(BF16)
        wc_ref[blk, 0:BLOCK_STATE, LANES:] = e_ca2_re.T.astype(BF16)
        wc_ref[blk, BLOCK_STATE:, 0:LANES] = e_ca_im.T.astype(BF16)
        wc_ref[blk, BLOCK_STATE:, LANES:] = e_ca2_im.T.astype(BF16)
        wd_ref[blk, 0:LANES, :] = direct
        wd_ref[blk, LANES:, 0:LANES] = jnp.zeros((LANES, LANES), BF16)
        wd_ref[blk, LANES:, LANES:] = direct[:, 0:LANES]

    sre_ref[...] = sre_t_ref[...].T
    sim_ref[...] = sim_t_ref[...].T
    for src_ref, dst_ref in zip(dense_refs, dense_bf16_refs):
        dst_ref[...] = src_ref[...].astype(BF16)


def _ssm_weights(a_re, a_im, log_dt, b_re, b_im, c_re, c_im, states_t, dense):
    gpb, per_step = GROUPS_PER_BLOCK, PREP_BLOCKS_PER_STEP
    n_steps = N_BLOCKS // per_step
    batch = states_t[0].shape[1]
    slab = STATE_W // n_steps
    per_state = [a_re, a_im]
    per_channel = [jnp.swapaxes(b_re, 1, 2), jnp.swapaxes(b_im, 1, 2), c_re, c_im]
    ins = ([v.reshape(N_BLOCKS, gpb, SSM_STATE) for v in per_state] + [log_dt]
           + [v.reshape(N_BLOCKS, LANES, SSM_STATE) for v in per_channel] + list(states_t))
    in_specs = ([pl.BlockSpec((per_step, gpb, SSM_STATE), lambda i: (i, 0, 0))] * len(per_state)
                + [pl.BlockSpec((1, SSM_GROUPS), lambda i: (0, 0))]
                + [pl.BlockSpec((per_step, LANES, SSM_STATE), lambda i: (i, 0, 0))]
                * len(per_channel)
                + [pl.BlockSpec((slab, batch), lambda i: (i, 0))] * len(states_t))
    dense_specs = [pl.BlockSpec((v.shape[0] // n_steps,) + v.shape[1:],
                                lambda i, nd=v.ndim: (i,) + (0,) * (nd - 1)) for v in dense]
    assert all(v.shape[0] % n_steps == 0 for v in dense)
    shapes = [(2 * LANES, 2 * BLOCK_STATE), (2 * BLOCK_STATE, 2 * LANES), (2 * LANES, 2 * LANES),
              (1, BLOCK_STATE), (1, BLOCK_STATE)]
    dtypes = [BF16, BF16, BF16, F32, F32]
    wb, wc, wd, a2_re, a2_im, s_re, s_im, *dense_bf16 = pl.pallas_call(
        functools.partial(_s5_weights_kernel, n_dense=len(dense)),
        grid=(n_steps,), in_specs=in_specs + dense_specs,
        out_specs=([pl.BlockSpec((per_step,) + s, lambda i: (i, 0, 0)) for s in shapes]
                   + [pl.BlockSpec((batch, slab), lambda i: (0, i))] * len(states_t)
                   + dense_specs),
        out_shape=([jax.ShapeDtypeStruct((N_BLOCKS,) + s, d) for s, d in zip(shapes, dtypes)]
                   + [jax.ShapeDtypeStruct((batch, STATE_W), F32)] * len(states_t)
                   + [jax.ShapeDtypeStruct(v.shape, BF16) for v in dense]),
        name="s5_block_diag_weights",
    )(*ins, *dense)
    return a2_re, a2_im, wb, wc, wd, (s_re, s_im), dense_bf16


PROMPT_T = 32
SAMPLE_B = 32


def kernel(x_prompt, x_sample, state_ssm_re, state_ssm_im, state_pool, meta_tokens, norm_gain, w_in, b_gate, ssm_a_re, ssm_a_im, ssm_log_dt, ssm_b_re, ssm_b_im, ssm_c_re, ssm_c_im, ssm_d, w_glu, b_glu, pool_mix, pool_scale, w_branch_ssm, w_branch_pool, w_out, final_norm_gain):
    batch, seq, _ = x_prompt.shape
    dec_batch, dec_seq, _ = x_sample.shape
    depth = norm_gain.shape[0]
    assert depth == 1
    l = 0
    states_t = [jnp.transpose(s[l], (1, 2, 0)).reshape(STATE_W, dec_batch)
                for s in (state_ssm_re, state_ssm_im)]
    dense = [w_glu[l], pool_mix[l], w_branch_ssm[l], w_branch_pool[l], w_out[l]]
    a2_re, a2_im, wb, wc, wd, (s0_re, s0_im), dense_bf16 = _ssm_weights(
        ssm_a_re[l], ssm_a_im[l], ssm_log_dt.reshape(depth, 1, SSM_GROUPS)[l], ssm_b_re[l], ssm_b_im[l],
        ssm_c_re[l], ssm_c_im[l], states_t, dense)
    w_glu_bf16, pool_mix_bf16, w_bs_bf16, w_bp_bf16, w_out_bf16 = dense_bf16
    row = lambda v: v.reshape(1, -1).astype(F32)
    weights = (
        row(norm_gain[l]), w_in[l].astype(BF16), row(b_gate[l]),
        a2_re, a2_im, wb, wc, wd, row(ssm_d[l]),
        w_glu_bf16, row(b_glu[l]), pool_mix_bf16, row(pool_scale[l]),
        w_bs_bf16, w_bp_bf16, w_out_bf16,
        row(final_norm_gain),
    )

    sample_state = (s0_re, s0_im, jnp.swapaxes(state_pool[l], 0, 1))
    y_sample, s_re, s_im, s_buf = _layer_call(
        x_sample, sample_state, weights,
        n_t=dec_seq, n_b=SAMPLE_B, carry=False, pos0=PAST_LEN, name="sample_layer")

    assert batch == SUBLANES and N_META <= PROMPT_T
    y_prompt, p_re, p_im, p_buf, s_re_t, s_im_t = _layer_call(
        x_prompt, None, weights, prefix=meta_tokens.astype(x_prompt.dtype), side=(s_re, s_im),
        n_t=PROMPT_T, n_b=batch, carry=True, pos0=N_META - PROMPT_T, name="prompt_layer")

    sample_states = [jnp.transpose(s.reshape(SSM_GROUPS, SSM_STATE, dec_batch), (2, 0, 1))[None]
                     for s in (s_re_t, s_im_t)]
    return (y_prompt, y_sample,
            p_re[None], p_im[None], jnp.swapaxes(p_buf, 0, 1)[None],
            *sample_states, jnp.swapaxes(s_buf, 0, 1)[None])
```

```python
import functools

import jax
import jax.numpy as jnp
from jax import lax
from jax.experimental import pallas as pl
from jax.experimental.pallas import tpu as pltpu

D_MODEL = 1024
N_META = 16
SSM_GROUP_CH = 16
SSM_GROUPS = D_MODEL // SSM_GROUP_CH
SSM_STATE = 64
POOL_WINDOWS = (2, 4, 8, 16)
POOL_GROUP_CH = D_MODEL // len(POOL_WINDOWS)
POOL_BUF = max(POOL_WINDOWS) - 1
EPS = 1e-6
PAST_LEN = 16384

SUBLANES = 8
LANES = 128
GROUPS_PER_BLOCK = LANES // SSM_GROUP_CH
N_BLOCKS = SSM_GROUPS // GROUPS_PER_BLOCK
BLOCK_STATE = GROUPS_PER_BLOCK * SSM_STATE
STATE_W = SSM_GROUPS * SSM_STATE
VMEM_LIMIT_BYTES = 56 * 1024 * 1024

F32 = jnp.float32
BF16 = jnp.bfloat16


def _rmsnorm(x, gain):
    ms = jnp.mean(x * x, axis=-1, keepdims=True)
    return (x * lax.rsqrt(ms + EPS)) * gain


_sigmoid = jax.nn.sigmoid


def _silu(x):
    return x * _sigmoid(x)


def _layer_kernel(*refs, n_t, n_b, carry, pos0, zero_state):
    n_side = 2 if carry else 0
    if carry:
        prefix_ref, x_ref, x_next_ref, *refs = refs
        side_in_refs, refs = refs[:n_side], refs[n_side:]
    else:
        x_ref, *refs = refs
    if not zero_state:
        s0re_ref, s0im_ref, buf0_ref, *refs = refs
    (gain_ref, w_in_ref, b_gate_ref, a2_re_ref, a2_im_ref, wb_ref, wc_ref, wd_ref, d_ref,
     w_glu_ref, b_glu_ref, pmix_ref, pscale_ref, w_bs_ref, w_bp_ref, w_out_ref, fgain_ref,
     y_ref, ore_ref, oim_ref, obuf_ref, *refs) = refs
    side_out_refs, refs = refs[:n_side], refs[n_side:]
    (sre_ref, sim_ref, ext_ref, bu_ref, u2_ref, xn_ref, us_ref, zs_ref, gs_ref, zp_ref,
     gp_ref, ys_ref, yp_ref) = refs
    m = n_t * n_b
    n_pairs = n_t // 2
    half = n_pairs * n_b
    step = pl.program_id(0)
    hist_rows = POOL_BUF * n_b

    def load_state():
        if zero_state:
            sre_ref[...] = jnp.zeros(sre_ref.shape, F32)
            sim_ref[...] = jnp.zeros(sim_ref.shape, F32)
            ext_ref[0:hist_rows, :] = jnp.zeros((hist_rows, D_MODEL), F32)
        else:
            sre_ref[...] = s0re_ref[...]
            sim_ref[...] = s0im_ref[...]
            ext_ref[0:hist_rows, :] = buf0_ref[...].reshape(hist_rows, D_MODEL)

    def rows_tb(src_ref):
        return jnp.swapaxes(src_ref[...], 0, 1).reshape(m, D_MODEL)

    def proj(k):
        return jnp.dot(xn_ref[...], w_in_ref[:, k * D_MODEL:(k + 1) * D_MODEL],
                       preferred_element_type=F32)

    def prepare_norm_rows(h):
        xn_ref[...] = _rmsnorm(h, gain_ref[...]).astype(BF16)

    def prepare_norm(src_ref):
        prepare_norm_rows(rows_tb(src_ref))

    def prefix_rows():
        tok = prefix_ref[...]
        n_tok = tok.shape[0]
        rows = jnp.broadcast_to(tok[:, None, :], (n_tok, n_b, D_MODEL)).reshape(n_tok * n_b, D_MODEL)
        return jnp.concatenate([jnp.zeros((m - n_tok * n_b, D_MODEL), F32), rows], axis=0)

    def prepare_u():
        us_ref[...] = proj(0).reshape(n_pairs, 2 * n_b, D_MODEL)

    def prepare_pairs():
        for blk in range(N_BLOCKS):
            lanes = slice(blk * LANES, (blk + 1) * LANES)
            st = slice(blk * BLOCK_STATE, (blk + 1) * BLOCK_STATE)
            u_t0 = us_ref[:, 0:n_b, lanes].reshape(half, LANES)
            u_t1 = us_ref[:, n_b:, lanes].reshape(half, LANES)
            u2_ref[blk] = jnp.concatenate([u_t0, u_t1], axis=1).astype(BF16)
            bu_ref[blk, 0:n_b, 0:BLOCK_STATE] = sre_ref[:, st]
            bu_ref[blk, 0:n_b, BLOCK_STATE:] = sim_ref[:, st]

    def prepare_slabs():
        for blk in range(N_BLOCKS):
            bu_ref[blk, n_b:, :] = jnp.dot(u2_ref[blk], wb_ref[blk], preferred_element_type=F32)

    def prepare(src_ref):
        prepare_norm(src_ref)
        prepare_u()
        prepare_pairs()
        prepare_slabs()

    if carry:
        @pl.when(step == 0)
        def _():
            load_state()
            prepare_norm_rows(prefix_rows())
            prepare_u()
            prepare_pairs()
            prepare_slabs()
        for src_ref, dst_ref in zip(side_in_refs, side_out_refs):
            dst_ref[...] = src_ref[...].T
    else:
        load_state()
        prepare(x_ref)

    piece = 2 * LANES

    def proj_piece(k, p):
        cols = slice(p * piece, (p + 1) * piece)
        v = jnp.dot(xn_ref[...], w_in_ref[:, k * D_MODEL + p * piece:k * D_MODEL + (p + 1) * piece],
                    preferred_element_type=F32)
        if k == 1:
            zs_ref[:, cols] = _silu(v)
        elif k == 4:
            gs_ref[:, cols] = _sigmoid(v + b_gate_ref[:, cols])
        elif k == 2:
            ext_ref[hist_rows:hist_rows + m, cols] = v
        elif k == 3:
            zp_ref[:, cols] = _silu(v)
        else:
            gp_ref[:, cols] = _sigmoid(
                v + b_gate_ref[:, D_MODEL + p * piece:D_MODEL + (p + 1) * piece])

    pieces = [(k, p) for k in (1, 4, 2, 3, 5) for p in range(D_MODEL // piece)]

    for blk in range(N_BLOCKS):
        lanes = slice(blk * LANES, (blk + 1) * LANES)
        st = slice(blk * BLOCK_STATE, (blk + 1) * BLOCK_STATE)
        for k, p in pieces[blk * len(pieces) // N_BLOCKS:(blk + 1) * len(pieces) // N_BLOCKS]:
            proj_piece(k, p)
        a_re = jnp.broadcast_to(a2_re_ref[blk], (SUBLANES, BLOCK_STATE))
        a_im = jnp.broadcast_to(a2_im_ref[blk], (SUBLANES, BLOCK_STATE))
        for j in range(n_b // SUBLANES):
            tile = slice(j * SUBLANES, (j + 1) * SUBLANES)
            s_re = sre_ref[tile, st]
            s_im = sim_ref[tile, st]
            for k in range(n_pairs):
                rows = slice((k + 1) * n_b + j * SUBLANES, (k + 1) * n_b + (j + 1) * SUBLANES)
                n_re = a_re * s_re - a_im * s_im + bu_ref[blk, rows, 0:BLOCK_STATE]
                n_im = a_re * s_im + a_im * s_re + bu_ref[blk, rows, BLOCK_STATE:]
                bu_ref[blk, rows, 0:BLOCK_STATE] = n_re
                bu_ref[blk, rows, BLOCK_STATE:] = n_im
                s_re, s_im = n_re, n_im
            sre_ref[tile, st] = s_re
            sim_ref[tile, st] = s_im
        y2 = (jnp.dot(bu_ref[blk, 0:half, :].astype(BF16), wc_ref[blk], preferred_element_type=F32)
              + jnp.dot(u2_ref[blk], wd_ref[blk], preferred_element_type=F32))
        ys_ref[:, 0:n_b, lanes] = y2[:, 0:LANES].reshape(n_pairs, n_b, LANES)
        ys_ref[:, n_b:, lanes] = y2[:, LANES:].reshape(n_pairs, n_b, LANES)

    if pos0 + 1 >= max(POOL_WINDOWS):
        pos = None
    else:
        t_loc = lax.broadcasted_iota(jnp.int32, (m, 1), 0) // n_b
        pos = pos0 + step * n_t + t_loc
    for gi, w in enumerate(POOL_WINDOWS):
        cols = slice(gi * POOL_GROUP_CH, (gi + 1) * POOL_GROUP_CH)
        s = ext_ref[(POOL_BUF + 1 - w) * n_b:hist_rows + m, cols]
        span = 1
        while span < w:
            keep = s.shape[0] - span * n_b
            s = s[:keep] + s[span * n_b:]
            span *= 2
        if pos is None:
            mean = s * (1.0 / w)
        else:
            cnt = jnp.maximum(jnp.minimum(pos + 1, w), 1)
            mean = s * (1.0 / cnt.astype(F32))
        pooled = mean - ext_ref[hist_rows:hist_rows + m, cols]
        yp_ref[:, cols] = jnp.dot(pooled.astype(BF16), pmix_ref[gi],
                                  preferred_element_type=F32) * pscale_ref[:, cols]
    for k in range(POOL_BUF):
        ext_ref[k * n_b:(k + 1) * n_b, :] = ext_ref[(k + n_t) * n_b:(k + n_t + 1) * n_b, :]

    y = ys_ref[...].reshape(m, D_MODEL) + d_ref[...] * us_ref[...].reshape(m, D_MODEL)
    y = jax.nn.gelu(y)
    b = yp_ref[...] * zp_ref[...]
    acc_p = jnp.dot(b.astype(BF16), w_bp_ref[...], preferred_element_type=F32)
    glu = jnp.dot(y.astype(BF16), w_glu_ref[...], preferred_element_type=F32)
    if carry:
        prepare_norm(x_next_ref)
    a = y * _sigmoid(glu + b_glu_ref[...]) * zs_ref[...]
    acc_s = jnp.dot(a.astype(BF16), w_bs_ref[...], preferred_element_type=F32)
    if carry:
        prepare_u()
        prepare_pairs()
    merged = gs_ref[...] * acc_s + gp_ref[...] * acc_p
    delta = jnp.dot(merged.astype(BF16), w_out_ref[...], preferred_element_type=F32)
    if carry:
        prepare_slabs()
    delta_bt = jnp.swapaxes(delta.reshape(n_t, n_b, D_MODEL), 0, 1)
    y_ref[...] = _rmsnorm(x_ref[...] + delta_bt, fgain_ref[...])

    def write_state():
        if carry:
            for g in range(SSM_GROUPS):
                ore_ref[:, g, :] = sre_ref[:, g * SSM_STATE:(g + 1) * SSM_STATE]
                oim_ref[:, g, :] = sim_ref[:, g * SSM_STATE:(g + 1) * SSM_STATE]
        else:
            ore_ref[...] = sre_ref[...]
            oim_ref[...] = sim_ref[...]
        obuf_ref[...] = ext_ref[0:hist_rows, :].reshape(POOL_BUF, n_b, D_MODEL)

    if carry:
        pl.when(step == pl.num_programs(0) - 1)(write_state)
    else:
        write_state()


def _layer_call(x, state, weights, *, n_t, n_b, carry, pos0, name, prefix=None, side=()):
    batch, time, _ = x.shape
    m = n_t * n_b
    x_block = (n_b, n_t, D_MODEL)
    buf_block = (POOL_BUF, n_b, D_MODEL)
    side_specs, side_shapes = [], []
    if carry:
        assert batch == n_b and time % n_t == 0 and prefix.shape[0] <= n_t
        n_blocks = time // n_t
        n_steps = n_blocks + 1
        side_rows, side_cols = side[0].shape
        n_slabs = side_cols // LANES
        assert len(side) == 2 and side_cols % LANES == 0 and n_slabs <= n_steps
        x_specs = [pl.BlockSpec(prefix.shape, lambda i: (0, 0)),
                   pl.BlockSpec(x_block, lambda i: (0, jnp.maximum(i - 1, 0), 0)),
                   pl.BlockSpec(x_block, lambda i: (0, jnp.minimum(i, n_blocks - 1), 0))]
        x_specs += [pl.BlockSpec((side_rows, LANES),
                                 lambda i: (0, jnp.minimum(i, n_slabs - 1)))] * len(side)
        side_specs = [pl.BlockSpec((LANES, side_rows),
                                   lambda i: (jnp.minimum(i, n_slabs - 1), 0))] * len(side)
        side_shapes = [jax.ShapeDtypeStruct((side_cols, side_rows), F32)] * len(side)
        y_map = lambda i: (0, jnp.maximum(i - 1, 0), 0)
        per_step = lambda i: (0, 0)
        per_step3 = lambda i: (0, 0, 0)
        xs = (prefix, x, x, *side)
        state_out = (batch, SSM_GROUPS, SSM_STATE)
        state_out_spec = pl.BlockSpec(state_out, per_step3)
    else:
        assert time == n_t and batch % n_b == 0
        n_steps = batch // n_b
        x_specs = [pl.BlockSpec(x_block, lambda i: (i, 0, 0))]
        y_map = lambda i: (i, 0, 0)
        per_step = lambda i: (i, 0)
        per_step3 = lambda i: (0, i, 0)
        xs = (x,)
        state_out = (batch, STATE_W)
        state_out_spec = pl.BlockSpec((n_b, STATE_W), per_step)
    whole = pl.BlockSpec(memory_space=pltpu.VMEM)
    state_specs = [
        pl.BlockSpec((n_b, STATE_W), per_step),
        pl.BlockSpec((n_b, STATE_W), per_step),
        pl.BlockSpec(buf_block, per_step3),
    ]
    kernel = functools.partial(_layer_kernel, n_t=n_t, n_b=n_b, carry=carry, pos0=pos0,
                               zero_state=state is None)
    return pl.pallas_call(
        kernel,
        grid=(n_steps,),
        in_specs=x_specs + ([] if state is None else state_specs) + [whole] * len(weights),
        out_specs=[pl.BlockSpec(x_block, y_map), state_out_spec, state_out_spec, state_specs[2]]
        + side_specs,
        out_shape=[
            jax.ShapeDtypeStruct((batch, time, D_MODEL), F32),
            jax.ShapeDtypeStruct(state_out, F32),
            jax.ShapeDtypeStruct(state_out, F32),
            jax.ShapeDtypeStruct((POOL_BUF, batch, D_MODEL), F32),
        ] + side_shapes,
        scratch_shapes=[
            pltpu.VMEM((n_b, STATE_W), F32),
            pltpu.VMEM((n_b, STATE_W), F32),
            pltpu.VMEM(((POOL_BUF + n_t) * n_b, D_MODEL), F32),
            pltpu.VMEM((N_BLOCKS, m // 2 + n_b, 2 * BLOCK_STATE), F32),
            pltpu.VMEM((N_BLOCKS, m // 2, 2 * LANES), BF16),
            pltpu.VMEM((m, D_MODEL), BF16),
            pltpu.VMEM((n_t // 2, 2 * n_b, D_MODEL), F32),
            pltpu.VMEM((m, D_MODEL), F32),
            pltpu.VMEM((m, D_MODEL), F32),
            pltpu.VMEM((m, D_MODEL), F32),
            pltpu.VMEM((m, D_MODEL), F32),
            pltpu.VMEM((n_t // 2, 2 * n_b, D_MODEL), F32),
            pltpu.VMEM((m, D_MODEL), F32),
        ],
        compiler_params=pltpu.CompilerParams(
            dimension_semantics=("arbitrary",), vmem_limit_bytes=VMEM_LIMIT_BYTES),
        name=name,
    )(*xs, *(() if state is None else state), *weights)


PREP_BLOCKS_PER_STEP = 2


def _s5_weights_kernel(*refs, n_dense):
    (a_re_ref, a_im_ref, log_dt_ref, bt_re_ref, bt_im_ref, c_re_ref, c_im_ref,
     sre_t_ref, sim_t_ref, *refs) = refs
    dense_refs, refs = refs[:n_dense], refs[n_dense:]
    wb_ref, wc_ref, wd_ref, a2_re_ref, a2_im_ref, sre_ref, sim_ref, *dense_bf16_refs = refs
    gpb, n_c, n_p = GROUPS_PER_BLOCK, SSM_GROUP_CH, SSM_STATE
    lane = lax.broadcasted_iota(jnp.int32, (n_p, BLOCK_STATE), 1)
    sel = jnp.where(jnp.bitwise_and(lane, n_p - 1)
                    == lax.broadcasted_iota(jnp.int32, (n_p, BLOCK_STATE), 0), 1.0, 0.0).astype(BF16)

    def group_of(shape, axis, width):
        idx = jnp.right_shift(lax.broadcasted_iota(jnp.int32, shape, axis), width.bit_length() - 1)
        return jnp.bitwise_and(idx, gpb - 1)

    n_stack = 4
    same_group = (group_of((n_stack * LANES, BLOCK_STATE), 0, n_c)
                  == group_of((n_stack * LANES, BLOCK_STATE), 1, n_p))
    same_group_cc = (group_of((LANES, 2 * LANES), 0, n_c) == group_of((LANES, 2 * LANES), 1, n_c))

    def per_row(v):
        return jnp.broadcast_to(v[:, None, :], (gpb, n_c, n_p)).reshape(LANES, n_p)

    def expand(mats):
        stacked = jnp.concatenate(mats, axis=0).astype(BF16)
        tiled = jnp.where(same_group, jnp.dot(stacked, sel, preferred_element_type=F32), 0.0)
        return [tiled[k * LANES:(k + 1) * LANES] for k in range(n_stack)]

    def state_row(v):
        tiled = jnp.dot(v, sel.astype(F32), precision=lax.Precision.HIGHEST,
                        preferred_element_type=F32)
        own = (jnp.right_shift(lax.broadcasted_iota(jnp.int32, (gpb, BLOCK_STATE), 1),
                               n_p.bit_length() - 1)
               == lax.broadcasted_iota(jnp.int32, (gpb, BLOCK_STATE), 0))
        return jnp.sum(jnp.where(own, tiled, 0.0), axis=0, keepdims=True)

    def gram(x, y):
        return lax.dot_general(x, y, (((1,), (1,)), ((), ())), precision=lax.Precision.HIGHEST,
                               preferred_element_type=F32)

    for blk in range(PREP_BLOCKS_PER_STEP):
        a_re, a_im = a_re_ref[blk], a_im_ref[blk]
        first_group = (pl.program_id(0) * PREP_BLOCKS_PER_STEP + blk) * gpb
        own_dt = (lax.broadcasted_iota(jnp.int32, (gpb, SSM_GROUPS), 1)
                  == first_group + lax.broadcasted_iota(jnp.int32, (gpb, SSM_GROUPS), 0))
        dt = jnp.exp(jnp.sum(jnp.where(own_dt, log_dt_ref[...], 0.0), axis=1, keepdims=True))
        mag = jnp.exp(dt * a_re)
        ang = dt * a_im
        abar_re = mag * jnp.cos(ang)
        abar_im = mag * jnp.sin(ang)
        den = a_re * a_re + a_im * a_im
        nr = abar_re - 1.0
        ni = abar_im
        q_re = (nr * a_re + ni * a_im) / den
        q_im = (ni * a_re - nr * a_im) / den
        a2_re = abar_re * abar_re - abar_im * abar_im
        a2_im = 2.0 * abar_re * abar_im
        a2_re_ref[blk] = state_row(a2_re)
        a2_im_ref[blk] = state_row(a2_im)

        ar, ai = per_row(abar_re), per_row(abar_im)
        ar2, ai2 = per_row(a2_re), per_row(a2_im)
        qr, qi = per_row(q_re), per_row(q_im)
        bt_re, bt_im = bt_re_ref[blk], bt_im_ref[blk]
        c_re, c_im = c_re_ref[blk], c_im_ref[blk]
        bbar_re = qr * bt_re - qi * bt_im
        bbar_im = qr * bt_im + qi * bt_re
        ab_re = ar * bbar_re - ai * bbar_im
        ab_im = ar * bbar_im + ai * bbar_re
        ca_re = c_re * ar - c_im * ai
        ca_im = c_re * ai + c_im * ar
        ca2_re = c_re * ar2 - c_im * ai2
        ca2_im = c_re * ai2 + c_im * ar2
        direct = (gram(bbar_re, jnp.concatenate([c_re, ca_re], axis=0))
                  - gram(bbar_im, jnp.concatenate([c_im, ca_im], axis=0)))
        direct = jnp.where(same_group_cc, direct, 0.0).astype(BF16)

        e_ab_re, e_ab_im, e_bb_re, e_bb_im = expand([ab_re, ab_im, bbar_re, bbar_im])
        wb_ref[blk, 0:LANES, 0:BLOCK_STATE] = e_ab_re.astype(BF16)
        wb_ref[blk, 0:LANES, BLOCK_STATE:] = e_ab_im.astype(BF16)
        wb_ref[blk, LANES:, 0:BLOCK_STATE] = e_bb_re.astype(BF16)
        wb_ref[blk, LANES:, BLOCK_STATE:] = e_bb_im.astype(BF16)
        e_ca_re, e_ca2_re, e_ca_im, e_ca2_im = expand([ca_re, ca2_re, -ca_im, -ca2_im])
        wc_ref[blk, 0:BLOCK_STATE, 0:LANES] = e_ca_re.T.astype(BF16)
        wc_ref[blk, 0:BLOCK_STATE, LANES:] = e_ca2_re.T.astype(BF16)
        wc_ref[blk, BLOCK_STATE:, 0:LANES] = e_ca_im.T.astype(BF16)
        wc_ref[blk, BLOCK_STATE:, LANES:] = e_ca2_im.T.astype(BF16)
        wd_ref[blk, 0:LANES, :] = direct
        wd_ref[blk, LANES:, 0:LANES] = jnp.zeros((LANES, LANES), BF16)
        wd_ref[blk, LANES:, LANES:] = direct[:, 0:LANES]

    sre_ref[...] = sre_t_ref[...].T
    sim_ref[...] = sim_t_ref[...].T
    for src_ref, dst_ref in zip(dense_refs, dense_bf16_refs):
        dst_ref[...] = src_ref[...].astype(BF16)


def _ssm_weights(a_re, a_im, log_dt, b_re, b_im, c_re, c_im, states_t, dense):
    gpb, per_step = GROUPS_PER_BLOCK, PREP_BLOCKS_PER_STEP
    n_steps = N_BLOCKS // per_step
    batch = states_t[0].shape[1]
    slab = STATE_W // n_steps
    per_state = [a_re, a_im]
    per_channel = [jnp.swapaxes(b_re, 1, 2), jnp.swapaxes(b_im, 1, 2), c_re, c_im]
    ins = ([v.reshape(N_BLOCKS, gpb, SSM_STATE) for v in per_state] + [log_dt]
           + [v.reshape(N_BLOCKS, LANES, SSM_STATE) for v in per_channel] + list(states_t))
    in_specs = ([pl.BlockSpec((per_step, gpb, SSM_STATE), lambda i: (i, 0, 0))] * len(per_state)
                + [pl.BlockSpec((1, SSM_GROUPS), lambda i: (0, 0))]
                + [pl.BlockSpec((per_step, LANES, SSM_STATE), lambda i: (i, 0, 0))]
                * len(per_channel)
                + [pl.BlockSpec((slab, batch), lambda i: (i, 0))] * len(states_t))
    dense_specs = [pl.BlockSpec((v.shape[0] // n_steps,) + v.shape[1:],
                                lambda i, nd=v.ndim: (i,) + (0,) * (nd - 1)) for v in dense]
    assert all(v.shape[0] % n_steps == 0 for v in dense)
    shapes = [(2 * LANES, 2 * BLOCK_STATE), (2 * BLOCK_STATE, 2 * LANES), (2 * LANES, 2 * LANES),
              (1, BLOCK_STATE), (1, BLOCK_STATE)]
    dtypes = [BF16, BF16, BF16, F32, F32]
    wb, wc, wd, a2_re, a2_im, s_re, s_im, *dense_bf16 = pl.pallas_call(
        functools.partial(_s5_weights_kernel, n_dense=len(dense)),
        grid=(n_steps,), in_specs=in_specs + dense_specs,
        out_specs=([pl.BlockSpec((per_step,) + s, lambda i: (i, 0, 0)) for s in shapes]
                   + [pl.BlockSpec((batch, slab), lambda i: (0, i))] * len(states_t)
                   + dense_specs),
        out_shape=([jax.ShapeDtypeStruct((N_BLOCKS,) + s, d) for s, d in zip(shapes, dtypes)]
                   + [jax.ShapeDtypeStruct((batch, STATE_W), F32)] * len(states_t)
                   + [jax.ShapeDtypeStruct(v.shape, BF16) for v in dense]),
        name="s5_block_diag_weights",
    )(*ins, *dense)
    return a2_re, a2_im, wb, wc, wd, (s_re, s_im), dense_bf16


PROMPT_T = 32
SAMPLE_B = 32


def kernel(x_prompt, x_sample, state_ssm_re, state_ssm_im, state_pool, meta_tokens, norm_gain, w_in, b_gate, ssm_a_re, ssm_a_im, ssm_log_dt, ssm_b_re, ssm_b_im, ssm_c_re, ssm_c_im, ssm_d, w_glu, b_glu, pool_mix, pool_scale, w_branch_ssm, w_branch_pool, w_out, final_norm_gain):
    batch, seq, _ = x_prompt.shape
    dec_batch, dec_seq, _ = x_sample.shape
    depth = norm_gain.shape[0]
    assert depth == 1
    l = 0
    states_t = [jnp.transpose(s[l], (1, 2, 0)).reshape(STATE_W, dec_batch)
                for s in (state_ssm_re, state_ssm_im)]
    dense = [w_in[l], w_glu[l], pool_mix[l], w_branch_ssm[l], w_branch_pool[l], w_out[l]]
    a2_re, a2_im, wb, wc, wd, (s0_re, s0_im), dense_bf16 = _ssm_weights(
        ssm_a_re[l], ssm_a_im[l], ssm_log_dt.reshape(depth, 1, SSM_GROUPS)[l], ssm_b_re[l], ssm_b_im[l],
        ssm_c_re[l], ssm_c_im[l], states_t, dense)
    w_in_bf16, w_glu_bf16, pool_mix_bf16, w_bs_bf16, w_bp_bf16, w_out_bf16 = dense_bf16
    row = lambda v: v.reshape(1, -1).astype(F32)
    weights = (
        row(norm_gain[l]), w_in_bf16, row(b_gate[l]),
        a2_re, a2_im, wb, wc, wd, row(ssm_d[l]),
        w_glu_bf16, row(b_glu[l]), pool_mix_bf16, row(pool_scale[l]),
        w_bs_bf16, w_bp_bf16, w_out_bf16,
        row(final_norm_gain),
    )

    sample_state = (s0_re, s0_im, jnp.swapaxes(state_pool[l], 0, 1))
    y_sample, s_re, s_im, s_buf = _layer_call(
        x_sample, sample_state, weights,
        n_t=dec_seq, n_b=SAMPLE_B, carry=False, pos0=PAST_LEN, name="sample_layer")

    assert batch == SUBLANES and N_META <= PROMPT_T
    y_prompt, p_re, p_im, p_buf, s_re_t, s_im_t = _layer_call(
        x_prompt, None, weights, prefix=meta_tokens.astype(x_prompt.dtype), side=(s_re, s_im),
        n_t=PROMPT_T, n_b=batch, carry=True, pos0=N_META - PROMPT_T, name="prompt_layer")

    sample_states = [jnp.transpose(s.reshape(SSM_GROUPS, SSM_STATE, dec_batch), (2, 0, 1))[None]
                     for s in (s_re_t, s_im_t)]
    return (y_prompt, y_sample,
            p_re[None], p_im[None], jnp.swapaxes(p_buf, 0, 1)[None],
            *sample_states, jnp.swapaxes(s_buf, 0, 1)[None])
```
